```python
import jax, jax.numpy as jnp
from jax import lax
import numpy as np

D_MODEL = 1024
BATCH = 16
SEQ = 256
DEPTH = 4
DEC_BATCH = 8
DEC_SEQ = 4096
PAST_LEN = 512

GRID_W = 64
N_HEADS = 8
HEAD_DK = 64
HEAD_DV = 128
QK_W = N_HEADS * HEAD_DK
V_W = N_HEADS * HEAD_DV
PROJ_W = 2 * QK_W + 2 * V_W + 4 * N_HEADS
CHUNK = 64
ROPE_BASE = 10000.0
N_GROUPS = 4
D_FF = 3584
N_EXPERTS = 8
TOP_K = 2
N_REC = (DEPTH + 1) // 2
N_FOU = DEPTH // 2
N_DENSE = (DEPTH + 1) // 2
N_MOE = DEPTH // 2
EPS = 1e-6
FORGET_BIAS = 3.0

kernel_name = 'hybrid_mlstm_fnet_dit_step'

F32 = jnp.float32


def rmsnorm(x, g):
    xf = x.astype(F32)
    y = xf * lax.rsqrt(jnp.mean(xf * xf, axis=-1, keepdims=True) + EPS)
    return (y * g.astype(F32)).astype(x.dtype)


def rope_2d(x):
    T = x.shape[1]
    rows = T // GRID_W
    row = jnp.broadcast_to(jnp.arange(rows, dtype=F32)[:, None], (rows, GRID_W)).reshape(T)
    col = jnp.broadcast_to(jnp.arange(GRID_W, dtype=F32)[None, :], (rows, GRID_W)).reshape(T)
    nf = HEAD_DK // 4
    inv = ROPE_BASE ** (-jnp.arange(nf, dtype=F32) / nf)

    def rot(xp, pos):
        ang = pos[:, None] * inv
        cos = jnp.cos(ang)[None, :, None, :]
        sin = jnp.sin(ang)[None, :, None, :]
        x1, x2 = xp[..., :nf], xp[..., nf:]
        return jnp.concatenate([x1 * cos - x2 * sin, x1 * sin + x2 * cos], axis=-1)

    half = HEAD_DK // 2
    return jnp.concatenate([rot(x[..., :half], row), rot(x[..., half:], col)], axis=-1)


def mlstm_scan(q, k, v, log_i, log_f, C0, n0, m0):
    Bsz, T = q.shape[0], q.shape[1]
    nc = T // CHUNK

    def to_chunks(a):
        a = a.reshape((Bsz, nc, CHUNK) + a.shape[2:])
        return jnp.moveaxis(a, (1, 2), (0, 3))

    tri = jnp.tril(jnp.ones((CHUNK, CHUNK), dtype=bool))

    def step(carry, inp):
        C, n, m = carry
        qc, kc, vc, ic, fc = inp
        b = jnp.cumsum(fc, axis=-1)
        logD = b[..., :, None] - b[..., None, :] + ic[..., None, :]
        logD = jnp.where(tri, logD, -jnp.inf)
        log_prev = b + m[..., None]
        m_t = jnp.maximum(log_prev, jnp.max(logD, axis=-1))
        Dm = jnp.exp(logD - m_t[..., None])
        prev_scale = jnp.exp(log_prev - m_t)
        s = jnp.einsum('bhtk,bhsk->bhts', qc, kc) * Dm
        num = prev_scale[..., None] * jnp.einsum('bhtk,bhkv->bhtv', qc, C) + jnp.einsum('bhts,bhsv->bhtv', s, vc)
        den = prev_scale * jnp.einsum('bhtk,bhk->bht', qc, n) + jnp.sum(s, axis=-1)
        h = num / jnp.maximum(jnp.abs(den), jnp.exp(-m_t))[..., None]
        bL = b[..., -1]
        log_w = bL[..., None] - b + ic
        m_new = jnp.maximum(bL + m, jnp.max(log_w, axis=-1))
        w = jnp.exp(log_w - m_new[..., None])
        decay = jnp.exp(bL + m - m_new)
        C_new = decay[..., None, None] * C + jnp.einsum('bhs,bhsk,bhsv->bhkv', w, kc, vc)
        n_new = decay[..., None] * n + jnp.einsum('bhs,bhsk->bhk', w, kc)
        return (C_new, n_new, m_new), h

    xs = tuple(to_chunks(a) for a in (q, k, v, log_i, log_f))
    (C, n, m), h = lax.scan(step, (C0, n0, m0), xs)
    h = jnp.moveaxis(h, (0, 3), (1, 2)).reshape(Bsz, T, N_HEADS, HEAD_DV)
    return h, (C, n, m)


def mlstm_mixer(hn, init_f, init_b, use_rope, w_in, b_gate, head_g, w_out):
    Bsz, T = hn.shape[0], hn.shape[1]
    p = jnp.einsum('btd,de->bte', hn, w_in).astype(F32)
    q, k, v, o, g = jnp.split(p, [QK_W, 2 * QK_W, 2 * QK_W + V_W, 2 * QK_W + 2 * V_W], axis=-1)
    q = q.reshape(Bsz, T, N_HEADS, HEAD_DK) * (HEAD_DK ** -0.5)
    k = k.reshape(Bsz, T, N_HEADS, HEAD_DK)
    v = v.reshape(Bsz, T, N_HEADS, HEAD_DV)
    if use_rope:
        q = rope_2d(q)
        k = rope_2d(k)
    g = g.reshape(Bsz, T, 4, N_HEADS) + b_gate.astype(F32)
    log_i_f, log_f_f = g[:, :, 0], jax.nn.log_sigmoid(g[:, :, 1])
    log_i_b, log_f_b = g[:, :, 2], jax.nn.log_sigmoid(g[:, :, 3])
    h_f, st_f = mlstm_scan(q, k, v, log_i_f, log_f_f, *init_f)
    flip = lambda a: jnp.flip(a, axis=1)
    h_b, st_b = mlstm_scan(flip(q), flip(k), flip(v), flip(log_i_b), flip(log_f_b), *init_b)
    hs = h_f + flip(h_b)
    hs = hs * lax.rsqrt(jnp.mean(hs * hs, axis=-1, keepdims=True) + EPS) * head_g.astype(F32)
    hs = hs.reshape(Bsz, T, V_W) * jax.nn.sigmoid(o)
    y = jnp.einsum('btv,vd->btd', hs.astype(hn.dtype), w_out)
    return y, st_f, st_b


def fourier_mixer(hn, w, b):
    Bsz, T, D = hn.shape
    hg = hn.astype(F32).reshape(Bsz, T, N_GROUPS, D // N_GROUPS)
    z = jnp.fft.fft2(hg, axes=(1, 3), norm='ortho').real.reshape(Bsz, T, D)
    return jnp.einsum('btd,de->bte', z.astype(hn.dtype), w) + b


def swiglu(h, w1, w3, w2):
    return jnp.einsum('btf,fd->btd', jax.nn.silu(jnp.einsum('btd,df->btf', h, w1)) * jnp.einsum('btd,df->btf', h, w3), w2)


def moe_swiglu(h, router, w1, w3, w2):
    logits = jnp.einsum('btd,de->bte', h, router).astype(F32)
    top_v, top_i = lax.top_k(logits, TOP_K)
    probs = jax.nn.softmax(top_v, axis=-1)
    combine = jnp.sum(jax.nn.one_hot(top_i, N_EXPERTS, dtype=F32) * probs[..., None], axis=-2)
    y = jnp.zeros(h.shape, F32)
    for e in range(N_EXPERTS):
        y = y + combine[..., e:e + 1] * swiglu(h, w1[e], w3[e], w2[e]).astype(F32)
    return y.astype(h.dtype)


def trunk(x, cond, cache, use_rope, w_mod, b_mod, norm_g, final_g, ml_w_in, ml_b_gate, ml_head_g, ml_w_out,
          fn_w, fn_b, ffn_w1, ffn_w3, ffn_w2, moe_router, moe_w1, moe_w3, moe_w2):
    Bsz = x.shape[0]
    Cs, ns, ms = [], [], []
    for i in range(DEPTH):
        j = i // 2
        mod = (jnp.einsum('bd,de->be', jax.nn.silu(cond), w_mod[i]) + b_mod[i]).reshape(cond.shape[0], 6, D_MODEL)
        sh1, sc1, g1, sh2, sc2, g2 = [mod[:, None, t] for t in range(6)]
        hn = rmsnorm(x, norm_g[i, 0]) * (1 + sc1) + sh1
        if i % 2 == 0:
            if cache is None:
                zC = jnp.zeros((Bsz, N_HEADS, HEAD_DK, HEAD_DV), F32)
                zn = jnp.zeros((Bsz, N_HEADS, HEAD_DK), F32)
                zm = jnp.zeros((Bsz, N_HEADS), F32)
                init_f, init_b = (zC, zn, zm), (zC, zn, zm)
            else:
                sC, sn, sm = cache
                init_f = (sC[:, j, 0].astype(F32), sn[:, j, 0].astype(F32), sm[:, j, 0].astype(F32))
                init_b = (sC[:, j, 1].astype(F32), sn[:, j, 1].astype(F32), sm[:, j, 1].astype(F32))
            y, st_f, st_b = mlstm_mixer(hn, init_f, init_b, use_rope, ml_w_in[j], ml_b_gate[j], ml_head_g[j], ml_w_out[j])
            Cs.append(jnp.stack([st_f[0], st_b[0]], axis=1))
            ns.append(jnp.stack([st_f[1], st_b[1]], axis=1))
            ms.append(jnp.stack([st_f[2], st_b[2]], axis=1))
        else:
            y = fourier_mixer(hn, fn_w[j], fn_b[j])
        x = x + g1 * y
        hn = rmsnorm(x, norm_g[i, 1]) * (1 + sc2) + sh2
        if i % 2 == 0:
            y = swiglu(hn, ffn_w1[j], ffn_w3[j], ffn_w2[j])
        else:
            y = moe_swiglu(hn, moe_router[j], moe_w1[j], moe_w3[j], moe_w2[j])
        x = x + g2 * y
    x = rmsnorm(x, final_g)
    states = (jnp.stack(Cs, axis=1).astype(x.dtype), jnp.stack(ns, axis=1).astype(x.dtype), jnp.stack(ms, axis=1).astype(x.dtype))
    return x, states


def setup_inputs(seed: int = 0) -> dict:
    key = jax.random.key(seed)
    ks = jax.random.split(key, 24)
    nrm = lambda k, shape, s: jax.random.normal(k, shape, F32) * s
    D = D_MODEL
    gate_offset = jnp.array([0.0, FORGET_BIAS, 0.0, FORGET_BIAS], F32)[None, :, None]
    return {
        'x_prompt': nrm(ks[0], (BATCH, SEQ, D), 1.0),
        'x_sample': nrm(ks[1], (DEC_BATCH, DEC_SEQ, D), 1.0),
        'state_C': nrm(ks[2], (DEC_BATCH, N_REC, 2, N_HEADS, HEAD_DK, HEAD_DV), 0.5),
        'state_n': nrm(ks[3], (DEC_BATCH, N_REC, 2, N_HEADS, HEAD_DK), 0.5),
        'state_m': nrm(ks[4], (DEC_BATCH, N_REC, 2, N_HEADS), 1.0),
        'c': nrm(ks[5], (DEC_BATCH, D), 1.0),
        'c_ctx': nrm(ks[6], (D,), 1.0),
        'w_mod': nrm(ks[7], (DEPTH, D, 6 * D), 0.5 * D ** -0.5),
        'b_mod': nrm(ks[8], (DEPTH, 6 * D), 0.02),
        'norm_g': 1.0 + nrm(ks[9], (DEPTH, 2, D), 0.1),
        'final_g': 1.0 + nrm(ks[10], (D,), 0.1),
        'ml_w_in': nrm(ks[11], (N_REC, D, PROJ_W), D ** -0.5),
        'ml_b_gate': nrm(ks[12], (N_REC, 4, N_HEADS), 0.5) + gate_offset,
        'ml_head_g': 1.0 + nrm(ks[13], (N_REC, N_HEADS, HEAD_DV), 0.1),
        'ml_w_out': nrm(ks[14], (N_REC, V_W, D), V_W ** -0.5),
        'fn_w': nrm(ks[15], (N_FOU, D, D), D ** -0.5),
        'fn_b': nrm(ks[16], (N_FOU, D), 0.02),
        'ffn_w1': nrm(ks[17], (N_DENSE, D, D_FF), D ** -0.5),
        'ffn_w3': nrm(ks[18], (N_DENSE, D, D_FF), D ** -0.5),
        'ffn_w2': nrm(ks[19], (N_DENSE, D_FF, D), D_FF ** -0.5),
        'moe_router': nrm(ks[20], (N_MOE, D, N_EXPERTS), D ** -0.5),
        'moe_w1': nrm(ks[21], (N_MOE, N_EXPERTS, D, D_FF), D ** -0.5),
        'moe_w3': nrm(ks[22], (N_MOE, N_EXPERTS, D, D_FF), D ** -0.5),
        'moe_w2': nrm(ks[23], (N_MOE, N_EXPERTS, D_FF, D), D_FF ** -0.5),
    }


def reference(x_prompt, x_sample, state_C, state_n, state_m, c, c_ctx, w_mod, b_mod, norm_g, final_g,
              ml_w_in, ml_b_gate, ml_head_g, ml_w_out, fn_w, fn_b, ffn_w1, ffn_w3, ffn_w2,
              moe_router, moe_w1, moe_w3, moe_w2):
    y_prompt, (new_C, new_n, new_m) = trunk(
        x_prompt, c_ctx[None, :], None, False, w_mod, b_mod, norm_g, final_g, ml_w_in, ml_b_gate, ml_head_g,
        ml_w_out, fn_w, fn_b, ffn_w1, ffn_w3, ffn_w2, moe_router, moe_w1, moe_w3, moe_w2)
    y_sample, _ = trunk(
        x_sample, c, (state_C, state_n, state_m), True, w_mod, b_mod, norm_g, final_g, ml_w_in, ml_b_gate,
        ml_head_g, ml_w_out, fn_w, fn_b, ffn_w1, ffn_w3, ffn_w2, moe_router, moe_w1, moe_w3, moe_w2)
    return (y_prompt, y_sample, new_C, new_n, new_m)
```

```python
import functools

import numpy as np
import jax
import jax.numpy as jnp
from jax import lax
from jax.experimental import pallas as pl
from jax.experimental.pallas import tpu as pltpu

F32 = jnp.float32
BF16 = jnp.bfloat16
HIGHEST = lax.Precision.HIGHEST

EPS = 1e-6
N_HEADS = 8
HEAD_DK = 64
HEAD_DV = 128
QK_W = N_HEADS * HEAD_DK
V_W = N_HEADS * HEAD_DV
GRID_W = 64
ROPE_BASE = 10000.0
N_GROUPS = 4
N_GATES = 4 * N_HEADS
LANES = 128
SCAN_CHUNK = 128
VMEM_LIMIT = 56 * 1024 * 1024
NEG_INF = float("-inf")


def _params(*sem):
    return pltpu.CompilerParams(dimension_semantics=sem, vmem_limit_bytes=VMEM_LIMIT)


def _tile(n, pref):
    t = min(n, pref)
    assert n % t == 0, (n, pref)
    return t


def _norm_mod(x, gain, scale, shift):
    ms = jnp.mean(x * x, axis=-1, keepdims=True)
    return x * lax.rsqrt(ms + EPS) * gain * (1.0 + scale) + shift


def _dot(a, b):
    return jnp.dot(a, b, preferred_element_type=F32)


def _dot_nt(a, b):
    return lax.dot_general(a, b, (((1,), (1,)), ((), ())), preferred_element_type=F32)


def _log_sigmoid(x):
    return jnp.minimum(x, 0.0) - jnp.log1p(jnp.exp(-jnp.abs(x)))


def _mod_kernel(c_ref, w_ref, b_ref, o_ref):
    c = c_ref[...]
    s = c * jax.nn.sigmoid(c)
    o_ref[0] = jnp.dot(s, w_ref[0], preferred_element_type=F32, precision=HIGHEST) + b_ref[0]


def _mod_table(cond, w_mod, b_mod):
    depth, d, n = w_mod.shape
    rows = cond.shape[0]
    tn = _tile(n, 1536)
    return pl.pallas_call(
        _mod_kernel,
        grid=(depth, n // tn),
        in_specs=[pl.BlockSpec((rows, d), lambda l, j: (0, 0)),
                  pl.BlockSpec((1, d, tn), lambda l, j: (l, 0, j)),
                  pl.BlockSpec((1, 1, tn), lambda l, j: (l, 0, j))],
        out_specs=pl.BlockSpec((1, rows, tn), lambda l, j: (l, 0, j)),
        out_shape=jax.ShapeDtypeStruct((depth, rows, n), F32),
        compiler_params=_params("parallel", "parallel"),
        name="adaln_table",
    )(cond, w_mod, b_mod.reshape(depth, 1, n))


def _rope_tables(t_len):
    pos = np.arange(t_len)
    row = (pos // GRID_W).astype(np.float32)
    col = (pos % GRID_W).astype(np.float32)
    nf = HEAD_DK // 4
    inv = (np.float32(ROPE_BASE) ** (-np.arange(nf, dtype=np.float32) / nf)).astype(np.float32)
    d = np.arange(HEAD_DK)
    p = np.where(d[None, :] < HEAD_DK // 2, row[:, None], col[:, None]).astype(np.float32)
    ang = p * inv[d % nf][None, :]
    sign = np.where((d % (2 * nf)) < nf, -1.0, 1.0).astype(np.float32)
    return np.cos(ang).astype(np.float32), (np.sin(ang) * sign[None, :]).astype(np.float32)


def _inproj_kernel(*refs, use_rope):
    if use_rope:
        (x_ref, mod_ref, ng_ref, wq_ref, wkt_ref, wv_ref, wo_ref, wgt_ref, bg_ref, trif_ref, trib_ref,
         cq_ref, sq_ref, ck_ref, sk_ref, q_ref, kt_ref, v_ref, o_ref, r_ref) = refs
    else:
        (x_ref, mod_ref, ng_ref, wq_ref, wkt_ref, wv_ref, wo_ref, wgt_ref, bg_ref, trif_ref, trib_ref,
         q_ref, kt_ref, v_ref, o_ref, r_ref) = refs
    hn = _norm_mod(x_ref[0], ng_ref[...], mod_ref[0, 1:2, :], mod_ref[0, 0:1, :])
    hb = hn.astype(BF16)
    q = _dot(hb, wq_ref[...]) * (HEAD_DK ** -0.5)
    kt = _dot_nt(wkt_ref[...], hb)
    if use_rope:
        nf = HEAD_DK // 4
        lane = lax.broadcasted_iota(jnp.int32, (q.shape[0], LANES), 1)
        first_q = (lane % (2 * nf)) < nf
        sub = lax.broadcasted_iota(jnp.int32, (LANES, kt.shape[1]), 0)
        first_k = (sub % (2 * nf)) < nf
        cq, sq, ck, sk = cq_ref[...], sq_ref[...], ck_ref[...], sk_ref[...]
        for s in range(QK_W // LANES):
            qs = q[:, s * LANES:(s + 1) * LANES]
            sw = jnp.where(first_q, pltpu.roll(qs, LANES - nf, 1), pltpu.roll(qs, nf, 1))
            q_ref[0, :, s * LANES:(s + 1) * LANES] = (qs * cq + sw * sq).astype(BF16)
            ks = kt[s * LANES:(s + 1) * LANES, :]
            sw = jnp.where(first_k, pltpu.roll(ks, LANES - nf, 0), pltpu.roll(ks, nf, 0))
            kt_ref[0, s * LANES:(s + 1) * LANES, :] = (ks * ck + sw * sk).astype(BF16)
    else:
        q_ref[0] = q.astype(BF16)
        kt_ref[0] = kt.astype(BF16)
    v_ref[0] = _dot(hb, wv_ref[...]).astype(BF16)
    o_ref[0] = _dot(hb, wo_ref[...])
    gt = _dot_nt(wgt_ref[...], hb) + bg_ref[...]
    h = N_HEADS
    i_f, f_f = gt[0:h], _log_sigmoid(gt[h:2 * h])
    i_b, f_b = gt[2 * h:3 * h], _log_sigmoid(gt[3 * h:4 * h])
    b_f = jnp.dot(f_f, trif_ref[...], preferred_element_type=F32, precision=HIGHEST)
    b_b = jnp.dot(f_b, trib_ref[...], preferred_element_type=F32, precision=HIGHEST)
    r_ref[0, 0:h, :] = i_f - b_f
    r_ref[0, h:2 * h, :] = b_f
    r_ref[0, 2 * h:3 * h, :] = i_b - b_b
    r_ref[0, 3 * h:4 * h, :] = b_b


def _mlstm_inproj(x, mod, ng, w_in, b_gate, use_rope, per_batch_mod):
    bsz, t_len, d = x.shape
    tm = _tile(t_len, 512)
    wq = w_in[:, :QK_W].astype(BF16)
    wkt = w_in[:, QK_W:2 * QK_W].T.astype(BF16)
    wv = w_in[:, 2 * QK_W:2 * QK_W + V_W].astype(BF16)
    wo = w_in[:, 2 * QK_W + V_W:2 * QK_W + 2 * V_W].astype(BF16)
    wgt = w_in[:, 2 * QK_W + 2 * V_W:].T.astype(BF16)
    bg = b_gate.reshape(N_GATES, 1).astype(F32)
    pos = np.arange(tm)
    same = (pos[:, None] // SCAN_CHUNK) == (pos[None, :] // SCAN_CHUNK)
    trif = jnp.asarray((same & (pos[:, None] <= pos[None, :])).astype(np.float32))
    trib = jnp.asarray((same & (pos[:, None] >= pos[None, :])).astype(np.float32))
    bidx = (lambda b: b) if per_batch_mod else (lambda b: 0)
    const = lambda shp: pl.BlockSpec(shp, lambda b, i: (0,) * len(shp))
    in_specs = [pl.BlockSpec((1, tm, d), lambda b, i: (b, i, 0)),
                pl.BlockSpec((1, 6, d), lambda b, i: (bidx(b), 0, 0)),
                const((1, d)), const((d, QK_W)), const((QK_W, d)), const((d, V_W)), const((d, V_W)),
                const((N_GATES, d)), const((N_GATES, 1)), const((tm, tm)), const((tm, tm))]
    args = [x, mod, ng.reshape(1, d), wq, wkt, wv, wo, wgt, bg, trif, trib]
    if use_rope:
        cos, sin = _rope_tables(t_len)
        rep = LANES // HEAD_DK
        args += [jnp.asarray(np.tile(cos, (1, rep))), jnp.asarray(np.tile(sin, (1, rep))),
                 jnp.asarray(np.tile(cos.T, (rep, 1))), jnp.asarray(np.tile(sin.T, (rep, 1)))]
        in_specs += [pl.BlockSpec((tm, LANES), lambda b, i: (i, 0)), pl.BlockSpec((tm, LANES), lambda b, i: (i, 0)),
                     pl.BlockSpec((LANES, tm), lambda b, i: (0, i)), pl.BlockSpec((LANES, tm), lambda b, i: (0, i))]
    out_shape = (jax.ShapeDtypeStruct((bsz, t_len, QK_W), BF16),
                 jax.ShapeDtypeStruct((bsz, QK_W, t_len), BF16),
                 jax.ShapeDtypeStruct((bsz, t_len, V_W), BF16),
                 jax.ShapeDtypeStruct((bsz, t_len, V_W), F32),
                 jax.ShapeDtypeStruct((bsz, N_GATES, t_len), F32))
    out_specs = (pl.BlockSpec((1, tm, QK_W), lambda b, i: (b, i, 0)),
                 pl.BlockSpec((1, QK_W, tm), lambda b, i: (b, 0, i)),
                 pl.BlockSpec((1, tm, V_W), lambda b, i: (b, i, 0)),
                 pl.BlockSpec((1, tm, V_W), lambda b, i: (b, i, 0)),
                 pl.BlockSpec((1, N_GATES, tm), lambda b, i: (b, 0, i)))
    return pl.pallas_call(
        functools.partial(_inproj_kernel, use_rope=use_rope),
        grid=(bsz, t_len // tm), in_specs=in_specs, out_specs=out_specs, out_shape=out_shape,
        compiler_params=_params("parallel", "parallel"),
        name="mlstm_inproj",
    )(*args)


def _scan_kernel(q_ref, kt_ref, v_ref, r_ref, cn0_ref, m0_ref, h_ref, cn_out_ref, m_out_ref,
                 cn_scr, m_scr, *, reverse, n_chunks):
    L = SCAN_CHUNK
    j = pl.program_id(1)

    @pl.when(j == 0)
    def _():
        cn_scr[...] = cn0_ref[0]
        m_scr[...] = m0_ref[0]

    row_i = lax.broadcasted_iota(jnp.int32, (L, L), 0)
    col_i = lax.broadcasted_iota(jnp.int32, (L, L), 1)
    visible = (col_i >= row_i) if reverse else (col_i <= row_i)
    diag = col_i == row_i
    upper_lanes = lax.broadcasted_iota(jnp.int32, (L, LANES), 1) >= HEAD_DK
    ones = jnp.ones((L, LANES), BF16)
    end_lane = lax.broadcasted_iota(jnp.int32, (1, L), 1) == (0 if reverse else L - 1)
    order = range(n_chunks - 1, -1, -1) if reverse else range(n_chunks)
    for c in order:
        r0 = c * L
        for h in range(N_HEADS):
            p, e = h // 2, h % 2
            q_pair = q_ref[0, r0:r0 + L, p * LANES:(p + 1) * LANES]
            q_m = jnp.where(upper_lanes if e else jnp.logical_not(upper_lanes), q_pair, jnp.zeros_like(q_pair))
            kt_pair = kt_ref[0, p * LANES:(p + 1) * LANES, r0:r0 + L]
            kt_h = kt_ref[0, h * HEAD_DK:(h + 1) * HEAD_DK, r0:r0 + L]
            a_row = r_ref[0, h:h + 1, r0:r0 + L]
            b_row = r_ref[0, N_HEADS + h:N_HEADS + h + 1, r0:r0 + L]
            m = m_scr[h:h + 1, 0:1]
            am = jnp.where(visible, a_row, NEG_INF)
            u = jnp.maximum(m, jnp.max(am, axis=1, keepdims=True))
            b_col = jnp.sum(jnp.where(diag, b_row, 0.0), axis=1, keepdims=True)
            d_mat = jnp.exp(am - u)
            prev_scale = jnp.exp(m - u)
            guard = jnp.exp(-(b_col + u))
            s = (_dot(q_m, kt_pair) * d_mat).astype(BF16)
            v_aug = jnp.concatenate([v_ref[0, r0:r0 + L, h * HEAD_DV:(h + 1) * HEAD_DV], ones], axis=1)
            intra = _dot(s, v_aug)
            cn = cn_scr[p]
            inter = _dot(q_m, cn.astype(BF16))
            num = prev_scale * inter[:, :HEAD_DV] + intra[:, :HEAD_DV]
            den = prev_scale * inter[:, HEAD_DV:] + intra[:, HEAD_DV:]
            h_ref[0, r0:r0 + L, h * HEAD_DV:(h + 1) * HEAD_DV] = num / jnp.maximum(jnp.abs(den), guard)
            u_end = jnp.maximum(m, jnp.max(a_row, axis=1, keepdims=True))
            w_row = jnp.exp(a_row - u_end)
            decay = jnp.exp(m - u_end)
            kw = (kt_h.astype(F32) * w_row).astype(BF16)
            rows = slice(e * HEAD_DK, (e + 1) * HEAD_DK)
            cn_scr[p, rows, :] = decay * cn[rows, :] + _dot(kw, v_aug)
            b_end = jnp.sum(jnp.where(end_lane, b_row, 0.0), axis=1, keepdims=True)
            m_scr[h:h + 1, :] = jnp.broadcast_to(b_end + u_end, (1, LANES))

    @pl.when(j == pl.num_programs(1) - 1)
    def _():
        cn_out_ref[0] = cn_scr[...]
        m_out_ref[0] = m_scr[...]


def _mlstm_scan(q, kt, v, r, cn0, m0, reverse):
    bsz, t_len, _ = q.shape
    tb = _tile(t_len, 4 * SCAN_CHUNK)
    nblk = t_len // tb
    blk = (lambda j: nblk - 1 - j) if reverse else (lambda j: j)
    d = 1 if reverse else 0
    half = N_HEADS // 2
    return pl.pallas_call(
        functools.partial(_scan_kernel, reverse=reverse, n_chunks=tb // SCAN_CHUNK),
        grid=(bsz, nblk),
        in_specs=[pl.BlockSpec((1, tb, QK_W), lambda b, j: (b, blk(j), 0)),
                  pl.BlockSpec((1, QK_W, tb), lambda b, j: (b, 0, blk(j))),
                  pl.BlockSpec((1, tb, V_W), lambda b, j: (b, blk(j), 0)),
                  pl.BlockSpec((1, 2 * N_HEADS, tb), lambda b, j: (b, d, blk(j))),
                  pl.BlockSpec((1, half, 2 * HEAD_DK, 2 * HEAD_DV), lambda b, j: (b, 0, 0, 0)),
                  pl.BlockSpec((1, N_HEADS, LANES), lambda b, j: (b, 0, 0))],
        out_specs=(pl.BlockSpec((1, tb, V_W), lambda b, j: (b, blk(j), 0)),
                   pl.BlockSpec((1, half, 2 * HEAD_DK, 2 * HEAD_DV), lambda b, j: (b, 0, 0, 0)),
                   pl.BlockSpec((1, N_HEADS, LANES), lambda b, j: (b, 0, 0))),
        out_shape=(jax.ShapeDtypeStruct((bsz, t_len, V_W), F32),
                   jax.ShapeDtypeStruct((bsz, half, 2 * HEAD_DK, 2 * HEAD_DV), F32),
                   jax.ShapeDtypeStruct((bsz, N_HEADS, LANES), F32)),
        scratch_shapes=[pltpu.VMEM((half, 2 * HEAD_DK, 2 * HEAD_DV), F32),
                        pltpu.VMEM((N_HEADS, LANES), F32)],
        compiler_params=_params("parallel", "arbitrary"),
        name="mlstm_scan_bwd" if reverse else "mlstm_scan_fwd",
    )(q, kt, v, r, cn0, m0)


def _pack_state(c, n, m):
    bsz = c.shape[0]
    cn = jnp.concatenate([c, jnp.broadcast_to(n[..., None], n.shape + (HEAD_DV,))], axis=-1)
    cn = cn.reshape(bsz, N_HEADS // 2, 2 * HEAD_DK, 2 * HEAD_DV)
    return cn.astype(F32), jnp.broadcast_to(m[..., None], (bsz, N_HEADS, LANES)).astype(F32)


def _unpack_state(cn, m):
    bsz = cn.shape[0]
    cn = cn.reshape(bsz, N_HEADS, HEAD_DK, 2 * HEAD_DV)
    return cn[..., :HEAD_DV], cn[..., HEAD_DV], m[..., 0]


def _mlstm_out_kernel(hf_ref, hb_ref, o_ref, x_ref, mod_ref, hg_ref, w_ref, out_ref):
    hs = hf_ref[0] + hb_ref[0]
    parts = []
    for h in range(N_HEADS):
        z = hs[:, h * HEAD_DV:(h + 1) * HEAD_DV]
        parts.append(z * lax.rsqrt(jnp.mean(z * z, axis=-1, keepdims=True) + EPS))
    hn = jnp.concatenate(parts, axis=1) * hg_ref[...] * jax.nn.sigmoid(o_ref[0])
    y = _dot(hn.astype(BF16), w_ref[...])
    out_ref[0] = x_ref[0] + mod_ref[0, 2:3, :] * y


def _mlstm_out(hf, hb, o, x, mod, head_g, w_out, per_batch_mod):
    bsz, t_len, d = x.shape
    tm = _tile(t_len, 512)
    bidx = (lambda b: b) if per_batch_mod else (lambda b: 0)
    tok = lambda w: pl.BlockSpec((1, tm, w), lambda b, i: (b, i, 0))
    return pl.pallas_call(
        _mlstm_out_kernel,
        grid=(bsz, t_len // tm),
        in_specs=[tok(V_W), tok(V_W), tok(V_W), tok(d),
                  pl.BlockSpec((1, 6, d), lambda b, i: (bidx(b), 0, 0)),
                  pl.BlockSpec((1, V_W), lambda b, i: (0, 0)),
                  pl.BlockSpec((V_W, d), lambda b, i: (0, 0))],
        out_specs=tok(d),
        out_shape=jax.ShapeDtypeStruct((bsz, t_len, d), F32),
        compiler_params=_params("parallel", "parallel"),
        name="mlstm_out",
    )(hf, hb, o, x, mod, head_g.reshape(1, V_W).astype(F32), w_out.astype(BF16))


def _ffn_kernel(*refs, routed):
    if routed:
        x_ref, mod_ref, ng_ref, comb_ref, w1_ref, w3_ref, w2_ref, out_ref, hn_scr, acc_scr = refs
    else:
        x_ref, mod_ref, ng_ref, w1_ref, w3_ref, w2_ref, out_ref, hn_scr, acc_scr = refs
    e, f = pl.program_id(2), pl.program_id(3)

    @pl.when((e == 0) & (f == 0))
    def _():
        hn = _norm_mod(x_ref[0], ng_ref[...], mod_ref[0, 4:5, :], mod_ref[0, 3:4, :])
        hn_scr[...] = hn.astype(BF16)
        acc_scr[...] = jnp.zeros_like(acc_scr)

    hb = hn_scr[...]
    h1 = _dot(hb, w1_ref[0])
    h3 = _dot(hb, w3_ref[0])
    act = (h1 * jax.nn.sigmoid(h1) * h3).astype(BF16)
    y = _dot(act, w2_ref[0])
    if routed:
        comb = comb_ref[0]
        lane = lax.broadcasted_iota(jnp.int32, comb.shape, 1)
        y = y * jnp.sum(jnp.where(lane == e, comb, 0.0), axis=-1, keepdims=True)
    acc_scr[...] += y

    @pl.when((e == pl.num_programs(2) - 1) & (f == pl.num_programs(3) - 1))
    def _():
        out_ref[0] = x_ref[0] + mod_ref[0, 5:6, :] * acc_scr[...]


def _ffn(x, mod, ng, w1, w3, w2, comb, per_batch_mod):
    bsz, t_len, d = x.shape
    n_exp, _, d_ff = w1.shape
    tm = _tile(t_len, 1024)
    tf = _tile(d_ff, 512)
    routed = comb is not None
    bidx = (lambda b: b) if per_batch_mod else (lambda b: 0)
    in_specs = [pl.BlockSpec((1, tm, d), lambda b, i, e, f: (b, i, 0)),
                pl.BlockSpec((1, 6, d), lambda b, i, e, f: (bidx(b), 0, 0)),
                pl.BlockSpec((1, d), lambda b, i, e, f: (0, 0))]
    args = [x, mod, ng.reshape(1, d)]
    if routed:
        in_specs.append(pl.BlockSpec((1, tm, LANES), lambda b, i, e, f: (b, i, 0)))
        args.append(comb)
    in_specs += [pl.BlockSpec((1, d, tf), lambda b, i, e, f: (e, 0, f)),
                 pl.BlockSpec((1, d, tf), lambda b, i, e, f: (e, 0, f)),
                 pl.BlockSpec((1, tf, d), lambda b, i, e, f: (e, f, 0))]
    args += [w1, w3, w2]
    return pl.pallas_call(
        functools.partial(_ffn_kernel, routed=routed),
        grid=(bsz, t_len // tm, n_exp, d_ff // tf),
        in_specs=in_specs,
        out_specs=pl.BlockSpec((1, tm, d), lambda b, i, e, f: (b, i, 0)),
        out_shape=jax.ShapeDtypeStruct((bsz, t_len, d), F32),
        scratch_shapes=[pltpu.VMEM((tm, d), BF16), pltpu.VMEM((tm, d), F32)],
        compiler_params=_params("parallel", "parallel", "arbitrary", "arbitrary"),
        name="moe_swiglu" if routed else "dense_swiglu",
    )(*args)


def _router_kernel(x_ref, mod_ref, ng_ref, wr_ref, comb_ref, *, n_exp):
    hn = _norm_mod(x_ref[0], ng_ref[...], mod_ref[0, 4:5, :], mod_ref[0, 3:4, :])
    logits = jnp.dot(hn, wr_ref[...], preferred_element_type=F32, precision=HIGHEST)
    lane = lax.broadcasted_iota(jnp.int32, logits.shape, 1).astype(F32)
    logits = jnp.where(lane < n_exp, logits, NEG_INF)
    v1 = jnp.max(logits, axis=-1, keepdims=True)
    i1 = jnp.min(jnp.where(logits == v1, lane, float(LANES)), axis=-1, keepdims=True)
    rest = jnp.where(lane == i1, NEG_INF, logits)
    v2 = jnp.max(rest, axis=-1, keepdims=True)
    i2 = jnp.min(jnp.where(rest == v2, lane, float(LANES)), axis=-1, keepdims=True)
    e2 = jnp.exp(v2 - v1)
    den = 1.0 + e2
    comb_ref[0] = jnp.where(lane == i1, 1.0 / den, 0.0) + jnp.where(lane == i2, e2 / den, 0.0)


def _router(x, mod, ng, w_router, per_batch_mod):
    bsz, t_len, d = x.shape
    n_exp = w_router.shape[1]
    tm = _tile(t_len, 512)
    wr = jnp.zeros((d, LANES), F32).at[:, :n_exp].set(w_router)
    bidx = (lambda b: b) if per_batch_mod else (lambda b: 0)
    return pl.pallas_call(
        functools.partial(_router_kernel, n_exp=n_exp),
        grid=(bsz, t_len // tm),
        in_specs=[pl.BlockSpec((1, tm, d), lambda b, i: (b, i, 0)),
                  pl.BlockSpec((1, 6, d), lambda b, i: (bidx(b), 0, 0)),
                  pl.BlockSpec((1, d), lambda b, i: (0, 0)),
                  pl.BlockSpec((d, LANES), lambda b, i: (0, 0))],
        out_specs=pl.BlockSpec((1, tm, LANES), lambda b, i: (b, i, 0)),
        out_shape=jax.ShapeDtypeStruct((bsz, t_len, LANES), F32),
        compiler_params=_params("parallel", "parallel"),
        name="moe_router",
    )(x, mod, ng.reshape(1, d), wr)


def _dft_channel_kernel(x_ref, mod_ref, ng_ref, cs_ref, y_ref, *, gw):
    hn = _norm_mod(x_ref[0], ng_ref[...], mod_ref[0, 1:2, :], mod_ref[0, 0:1, :]).astype(BF16)
    for g in range(hn.shape[1] // gw):
        y = _dot(hn[:, g * gw:(g + 1) * gw], cs_ref[...])
        y_ref[0, 0, :, g * gw:(g + 1) * gw] = y[:, :gw].astype(BF16)
        y_ref[0, 1, :, g * gw:(g + 1) * gw] = y[:, gw:].astype(BF16)


def _dft_time_kernel(w_ref, y_ref, x_ref, mod_ref, fw_ref, fb_ref, out_ref, acc_scr):
    k = pl.program_id(2)

    @pl.when(k == 0)
    def _():
        acc_scr[...] = jnp.zeros_like(acc_scr)

    acc_scr[...] += _dot(w_ref[...], y_ref[0])

    @pl.when(k == pl.num_programs(2) - 1)
    def _():
        y = _dot(acc_scr[...].astype(BF16), fw_ref[...]) + fb_ref[...]
        out_ref[0] = x_ref[0] + mod_ref[0, 2:3, :] * y


def _dft_matrix(n, scale):
    idx = (np.arange(n)[:, None] * np.arange(n)[None, :]) % n
    ang = 2.0 * np.pi * idx.astype(np.float64) / n
    return np.cos(ang) * scale, np.sin(ang) * scale


def _fourier_mixer(x, mod, ng, fn_w, fn_b, per_batch_mod):
    bsz, t_len, d = x.shape
    gw = d // N_GROUPS
    cg, sg = _dft_matrix(gw, gw ** -0.5)
    cs = jnp.asarray(np.concatenate([cg, sg], axis=1).astype(np.float32)).astype(BF16)
    tm = _tile(t_len, 512)
    bidx = (lambda b: b) if per_batch_mod else (lambda b: 0)
    y = pl.pallas_call(
        functools.partial(_dft_channel_kernel, gw=gw),
        grid=(bsz, t_len // tm),
        in_specs=[pl.BlockSpec((1, tm, d), lambda b, i: (b, i, 0)),
                  pl.BlockSpec((1, 6, d), lambda b, i: (bidx(b), 0, 0)),
                  pl.BlockSpec((1, d), lambda b, i: (0, 0)),
                  pl.BlockSpec((gw, 2 * gw), lambda b, i: (0, 0))],
        out_specs=pl.BlockSpec((1, 2, tm, d), lambda b, i: (b, 0, i, 0)),
        out_shape=jax.ShapeDtypeStruct((bsz, 2, t_len, d), BF16),
        compiler_params=_params("parallel", "parallel"),
        name="dft_channel",
    )(x, mod, ng.reshape(1, d), cs)
    y = y.reshape(bsz, 2 * t_len, d)
    kk = jnp.arange(t_len, dtype=jnp.int32)[:, None]
    tt = jnp.arange(t_len, dtype=jnp.int32)[None, :]
    ang = (2.0 * np.pi / t_len) * ((kk * tt) % t_len).astype(F32)
    scale = t_len ** -0.5
    wt = jnp.concatenate([jnp.cos(ang) * scale, jnp.sin(ang) * (-scale)], axis=1).astype(BF16)
    tmk = _tile(t_len, 1024)
    tk = _tile(2 * t_len, 2048)
    return pl.pallas_call(
        _dft_time_kernel,
        grid=(bsz, t_len // tmk, 2 * t_len // tk),
        in_specs=[pl.BlockSpec((tmk, tk), lambda b, i, k: (i, k)),
                  pl.BlockSpec((1, tk, d), lambda b, i, k: (b, k, 0)),
                  pl.BlockSpec((1, tmk, d), lambda b, i, k: (b, i, 0)),
                  pl.BlockSpec((1, 6, d), lambda b, i, k: (bidx(b), 0, 0)),
                  pl.BlockSpec((d, d), lambda b, i, k: (0, 0)),
                  pl.BlockSpec((1, d), lambda b, i, k: (0, 0))],
        out_specs=pl.BlockSpec((1, tmk, d), lambda b, i, k: (b, i, 0)),
        out_shape=jax.ShapeDtypeStruct((bsz, t_len, d), F32),
        scratch_shapes=[pltpu.VMEM((tmk, d), F32)],
        compiler_params=_params("parallel", "parallel", "arbitrary"),
        name="dft_time",
    )(wt, y, x, mod, fn_w.astype(BF16), fn_b.reshape(1, d).astype(F32))


def _final_norm_kernel(x_ref, g_ref, o_ref):
    x = x_ref[0]
    o_ref[0] = x * lax.rsqrt(jnp.mean(x * x, axis=-1, keepdims=True) + EPS) * g_ref[...]


def _final_norm(x, g):
    bsz, t_len, d = x.shape
    tm = _tile(t_len, 1024)
    return pl.pallas_call(
        _final_norm_kernel,
        grid=(bsz, t_len // tm),
        in_specs=[pl.BlockSpec((1, tm, d), lambda b, i: (b, i, 0)), pl.BlockSpec((1, d), lambda b, i: (0, 0))],
        out_specs=pl.BlockSpec((1, tm, d), lambda b, i: (b, i, 0)),
        out_shape=jax.ShapeDtypeStruct((bsz, t_len, d), F32),
        compiler_params=_params("parallel", "parallel"),
        name="final_norm",
    )(x, g.reshape(1, d))


def _trunk(x, mod, cache, use_rope, per_batch_mod, p):
    bsz, t_len, d = x.shape
    depth = mod.shape[0]
    flat = (lambda a: a) if per_batch_mod else (lambda a: a.reshape(1, bsz * t_len, a.shape[-1]))
    unflat = (lambda a: a) if per_batch_mod else (lambda a: a.reshape(bsz, t_len, a.shape[-1]))
    states = []
    for i in range(depth):
        j = i // 2
        m_i = mod[i]
        if i % 2 == 0:
            q, kt, v, o, r = _mlstm_inproj(x, m_i, p["norm_g"][i, 0], p["ml_w_in"][j], p["ml_b_gate"][j],
                                           use_rope, per_batch_mod)
            outs = []
            for direction in range(2):
                if cache is None:
                    cn0 = jnp.zeros((bsz, N_HEADS // 2, 2 * HEAD_DK, 2 * HEAD_DV), F32)
                    m0 = jnp.zeros((bsz, N_HEADS, LANES), F32)
                else:
                    cn0, m0 = _pack_state(cache[0][:, j, direction], cache[1][:, j, direction],
                                          cache[2][:, j, direction])
                outs.append(_mlstm_scan(q, kt, v, r, cn0, m0, reverse=bool(direction)))
            states.append([_unpack_state(cn, m) for (_, cn, m) in outs])
            x = _mlstm_out(outs[0][0], outs[1][0], o, x, m_i, p["ml_head_g"][j], p["ml_w_out"][j], per_batch_mod)
            x = unflat(_ffn(flat(x), m_i, p["norm_g"][i, 1], p["ffn_w1"][j][None].astype(BF16),
                            p["ffn_w3"][j][None].astype(BF16), p["ffn_w2"][j][None].astype(BF16),
                            None, per_batch_mod))
        else:
            x = _fourier_mixer(x, m_i, p["norm_g"][i, 0], p["fn_w"][j], p["fn_b"][j], per_batch_mod)
            xf = flat(x)
            comb = _router(xf, m_i, p["norm_g"][i, 1], p["moe_router"][j], per_batch_mod)
            x = unflat(_ffn(xf, m_i, p["norm_g"][i, 1], p["moe_w1"][j].astype(BF16), p["moe_w3"][j].astype(BF16),
                            p["moe_w2"][j].astype(BF16), comb, per_batch_mod))
    y = unflat(_final_norm(flat(x), p["final_g"]))
    return y, states


def kernel(x_prompt, x_sample, state_C, state_n, state_m, c, c_ctx, w_mod, b_mod, norm_g, final_g,
           ml_w_in, ml_b_gate, ml_head_g, ml_w_out, fn_w, fn_b, ffn_w1, ffn_w3, ffn_w2,
           moe_router, moe_w1, moe_w3, moe_w2):
    p = dict(norm_g=norm_g, final_g=final_g, ml_w_in=ml_w_in, ml_b_gate=ml_b_gate, ml_head_g=ml_head_g,
             ml_w_out=ml_w_out, fn_w=fn_w, fn_b=fn_b, ffn_w1=ffn_w1, ffn_w3=ffn_w3, ffn_w2=ffn_w2,
             moe_router=moe_router, moe_w1=moe_w1, moe_w3=moe_w3, moe_w2=moe_w2)
    depth, d = w_mod.shape[0], w_mod.shape[1]
    n_dec = c.shape[0]
    rows = ((n_dec + 1 + 7) // 8) * 8
    cond = jnp.zeros((rows, d), F32).at[:n_dec].set(c).at[n_dec].set(c_ctx)
    mod = _mod_table(cond, w_mod, b_mod).reshape(depth, rows, 6, d)
    y_prompt, st = _trunk(x_prompt, mod[:, n_dec:n_dec + 1], None, False, False, p)
    y_sample, _ = _trunk(x_sample, mod[:, :n_dec], (state_C, state_n, state_m), True, True, p)
    new_c = jnp.stack([jnp.stack([s[0][0], s[1][0]], axis=1) for s in st], axis=1)
    new_n = jnp.stack([jnp.stack([s[0][1], s[1][1]], axis=1) for s in st], axis=1)
    new_m = jnp.stack([jnp.stack([s[0][2], s[1][2]], axis=1) for s in st], axis=1)
    return (y_prompt, y_sample, new_c.astype(x_prompt.dtype), new_n.astype(x_prompt.dtype),
            new_m.astype(x_prompt.dtype))
```

```python
import functools

import numpy as np
import jax
import jax.numpy as jnp
from jax import lax
from jax.experimental import pallas as pl
from jax.experimental.pallas import tpu as pltpu

F32 = jnp.float32
BF16 = jnp.bfloat16
HIGHEST = lax.Precision.HIGHEST

EPS = 1e-6
N_HEADS = 8
HEAD_DK = 64
HEAD_DV = 128
QK_W = N_HEADS * HEAD_DK
V_W = N_HEADS * HEAD_DV
GRID_W = 64
ROPE_BASE = 10000.0
N_GROUPS = 4
N_GATES = 4 * N_HEADS
LANES = 128
SCAN_CHUNK = 128
VMEM_LIMIT = 56 * 1024 * 1024
NEG_INF = float("-inf")


def _params(*sem):
    return pltpu.CompilerParams(dimension_semantics=sem, vmem_limit_bytes=VMEM_LIMIT)


def _tile(n, pref):
    t = min(n, pref)
    assert n % t == 0, (n, pref)
    return t


def _norm_mod(x, gain, scale, shift):
    ms = jnp.mean(x * x, axis=-1, keepdims=True)
    return x * lax.rsqrt(ms + EPS) * gain * (1.0 + scale) + shift


def _dot(a, b):
    return jnp.dot(a, b, preferred_element_type=F32)


def _dot_nt(a, b):
    return lax.dot_general(a, b, (((1,), (1,)), ((), ())), preferred_element_type=F32)


def _log_sigmoid(x):
    return jnp.minimum(x, 0.0) - jnp.log1p(jnp.exp(-jnp.abs(x)))


def _mod_kernel(c_ref, w_ref, b_ref, o_ref):
    c = c_ref[...]
    s = c * jax.nn.sigmoid(c)
    o_ref[0] = jnp.dot(s, w_ref[0], preferred_element_type=F32, precision=HIGHEST) + b_ref[0]


def _mod_table(cond, w_mod, b_mod):
    depth, d, n = w_mod.shape
    rows = cond.shape[0]
    tn = _tile(n, 1536)
    return pl.pallas_call(
        _mod_kernel,
        grid=(depth, n // tn),
        in_specs=[pl.BlockSpec((rows, d), lambda l, j: (0, 0)),
                  pl.BlockSpec((1, d, tn), lambda l, j: (l, 0, j)),
                  pl.BlockSpec((1, 1, tn), lambda l, j: (l, 0, j))],
        out_specs=pl.BlockSpec((1, rows, tn), lambda l, j: (l, 0, j)),
        out_shape=jax.ShapeDtypeStruct((depth, rows, n), F32),
        compiler_params=_params("parallel", "parallel"),
        name="adaln_table",
    )(cond, w_mod, b_mod.reshape(depth, 1, n))


def _rope_tables(t_len):
    pos = np.arange(t_len)
    row = (pos // GRID_W).astype(np.float32)
    col = (pos % GRID_W).astype(np.float32)
    nf = HEAD_DK // 4
    inv = (np.float32(ROPE_BASE) ** (-np.arange(nf, dtype=np.float32) / nf)).astype(np.float32)
    d = np.arange(HEAD_DK)
    p = np.where(d[None, :] < HEAD_DK // 2, row[:, None], col[:, None]).astype(np.float32)
    ang = p * inv[d % nf][None, :]
    sign = np.where((d % (2 * nf)) < nf, -1.0, 1.0).astype(np.float32)
    return np.cos(ang).astype(np.float32), (np.sin(ang) * sign[None, :]).astype(np.float32)


def _inproj_kernel(*refs, use_rope):
    if use_rope:
        (x_ref, mod_ref, ng_ref, wq_ref, wkt_ref, wv_ref, wo_ref, wgt_ref, bg_ref, trif_ref, trib_ref,
         cq_ref, sq_ref, ck_ref, sk_ref, q_ref, kt_ref, v_ref, o_ref, r_ref) = refs
    else:
        (x_ref, mod_ref, ng_ref, wq_ref, wkt_ref, wv_ref, wo_ref, wgt_ref, bg_ref, trif_ref, trib_ref,
         q_ref, kt_ref, v_ref, o_ref, r_ref) = refs
    hn = _norm_mod(x_ref[0], ng_ref[...], mod_ref[0, 1:2, :], mod_ref[0, 0:1, :])
    hb = hn.astype(BF16)
    q = _dot(hb, wq_ref[...]) * (HEAD_DK ** -0.5)
    kt = _dot_nt(wkt_ref[...], hb)
    if use_rope:
        nf = HEAD_DK // 4
        lane = lax.broadcasted_iota(jnp.int32, (q.shape[0], LANES), 1)
        first_q = (lane % (2 * nf)) < nf
        sub = lax.broadcasted_iota(jnp.int32, (LANES, kt.shape[1]), 0)
        first_k = (sub % (2 * nf)) < nf
        cq, sq, ck, sk = cq_ref[...], sq_ref[...], ck_ref[...], sk_ref[...]
        for s in range(QK_W // LANES):
            qs = q[:, s * LANES:(s + 1) * LANES]
            sw = jnp.where(first_q, pltpu.roll(qs, LANES - nf, 1), pltpu.roll(qs, nf, 1))
            q_ref[0, :, s * LANES:(s + 1) * LANES] = (qs * cq + sw * sq).astype(BF16)
            ks = kt[s * LANES:(s + 1) * LANES, :]
            sw = jnp.where(first_k, pltpu.roll(ks, LANES - nf, 0), pltpu.roll(ks, nf, 0))
            kt_ref[0, s * LANES:(s + 1) * LANES, :] = (ks * ck + sw * sk).astype(BF16)
    else:
        q_ref[0] = q.astype(BF16)
        kt_ref[0] = kt.astype(BF16)
    v_ref[0] = _dot(hb, wv_ref[...]).astype(BF16)
    o_ref[0] = _dot(hb, wo_ref[...])
    gt = _dot_nt(wgt_ref[...], hb) + bg_ref[...]
    h = N_HEADS
    i_f, f_f = gt[0:h], _log_sigmoid(gt[h:2 * h])
    i_b, f_b = gt[2 * h:3 * h], _log_sigmoid(gt[3 * h:4 * h])
    b_f = jnp.dot(f_f, trif_ref[...], preferred_element_type=F32, precision=HIGHEST)
    b_b = jnp.dot(f_b, trib_ref[...], preferred_element_type=F32, precision=HIGHEST)
    r_ref[0, 0:h, :] = i_f - b_f
    r_ref[0, h:2 * h, :] = b_f
    r_ref[0, 2 * h:3 * h, :] = i_b - b_b
    r_ref[0, 3 * h:4 * h, :] = b_b


def _mlstm_inproj(x, mod, ng, w_in, b_gate, use_rope, per_batch_mod):
    bsz, t_len, d = x.shape
    tm = _tile(t_len, 512)
    wq = w_in[:, :QK_W].astype(BF16)
    wkt = w_in[:, QK_W:2 * QK_W].T.astype(BF16)
    wv = w_in[:, 2 * QK_W:2 * QK_W + V_W].astype(BF16)
    wo = w_in[:, 2 * QK_W + V_W:2 * QK_W + 2 * V_W].astype(BF16)
    wgt = w_in[:, 2 * QK_W + 2 * V_W:].T.astype(BF16)
    bg = b_gate.reshape(N_GATES, 1).astype(F32)
    pos = np.arange(tm)
    same = (pos[:, None] // SCAN_CHUNK) == (pos[None, :] // SCAN_CHUNK)
    trif = jnp.asarray((same & (pos[:, None] <= pos[None, :])).astype(np.float32))
    trib = jnp.asarray((same & (pos[:, None] >= pos[None, :])).astype(np.float32))
    bidx = (lambda b: b) if per_batch_mod else (lambda b: 0)
    const = lambda shp: pl.BlockSpec(shp, lambda b, i: (0,) * len(shp))
    in_specs = [pl.BlockSpec((1, tm, d), lambda b, i: (b, i, 0)),
                pl.BlockSpec((1, 6, d), lambda b, i: (bidx(b), 0, 0)),
                const((1, d)), const((d, QK_W)), const((QK_W, d)), const((d, V_W)), const((d, V_W)),
                const((N_GATES, d)), const((N_GATES, 1)), const((tm, tm)), const((tm, tm))]
    args = [x, mod, ng.reshape(1, d), wq, wkt, wv, wo, wgt, bg, trif, trib]
    if use_rope:
        cos, sin = _rope_tables(t_len)
        rep = LANES // HEAD_DK
        args += [jnp.asarray(np.tile(cos, (1, rep))), jnp.asarray(np.tile(sin, (1, rep))),
                 jnp.asarray(np.tile(cos.T, (rep, 1))), jnp.asarray(np.tile(sin.T, (rep, 1)))]
        in_specs += [pl.BlockSpec((tm, LANES), lambda b, i: (i, 0)), pl.BlockSpec((tm, LANES), lambda b, i: (i, 0)),
                     pl.BlockSpec((LANES, tm), lambda b, i: (0, i)), pl.BlockSpec((LANES, tm), lambda b, i: (0, i))]
    out_shape = (jax.ShapeDtypeStruct((bsz, t_len, QK_W), BF16),
                 jax.ShapeDtypeStruct((bsz, QK_W, t_len), BF16),
                 jax.ShapeDtypeStruct((bsz, t_len, V_W), BF16),
                 jax.ShapeDtypeStruct((bsz, t_len, V_W), F32),
                 jax.ShapeDtypeStruct((bsz, N_GATES, t_len), F32))
    out_specs = (pl.BlockSpec((1, tm, QK_W), lambda b, i: (b, i, 0)),
                 pl.BlockSpec((1, QK_W, tm), lambda b, i: (b, 0, i)),
                 pl.BlockSpec((1, tm, V_W), lambda b, i: (b, i, 0)),
                 pl.BlockSpec((1, tm, V_W), lambda b, i: (b, i, 0)),
                 pl.BlockSpec((1, N_GATES, tm), lambda b, i: (b, 0, i)))
    return pl.pallas_call(
        functools.partial(_inproj_kernel, use_rope=use_rope),
        grid=(bsz, t_len // tm), in_specs=in_specs, out_specs=out_specs, out_shape=out_shape,
        compiler_params=_params("parallel", "parallel"),
        name="mlstm_inproj",
    )(*args)


def _scan_kernel(q_ref, kt_ref, v_ref, r_ref, cn0_ref, m0_ref, h_ref, cn_out_ref, m_out_ref,
                 cn_scr, m_scr, *, reverse, n_chunks):
    L = SCAN_CHUNK
    j = pl.program_id(1)

    @pl.when(j == 0)
    def _():
        cn_scr[...] = cn0_ref[0]
        m_scr[...] = m0_ref[0]

    row_i = lax.broadcasted_iota(jnp.int32, (L, L), 0)
    col_i = lax.broadcasted_iota(jnp.int32, (L, L), 1)
    visible = (col_i >= row_i) if reverse else (col_i <= row_i)
    diag = col_i == row_i
    upper_lanes = lax.broadcasted_iota(jnp.int32, (L, LANES), 1) >= HEAD_DK
    ones = jnp.ones((L, LANES), BF16)
    end_lane = lax.broadcasted_iota(jnp.int32, (1, L), 1) == (0 if reverse else L - 1)
    order = range(n_chunks - 1, -1, -1) if reverse else range(n_chunks)
    for c in order:
        r0 = c * L
        for h in range(N_HEADS):
            p, e = h // 2, h % 2
            q_pair = q_ref[0, r0:r0 + L, p * LANES:(p + 1) * LANES]
            q_m = jnp.where(upper_lanes if e else jnp.logical_not(upper_lanes), q_pair, jnp.zeros_like(q_pair))
            kt_pair = kt_ref[0, p * LANES:(p + 1) * LANES, r0:r0 + L]
            kt_h = kt_ref[0, h * HEAD_DK:(h + 1) * HEAD_DK, r0:r0 + L]
            a_row = r_ref[0, h:h + 1, r0:r0 + L]
            b_row = r_ref[0, N_HEADS + h:N_HEADS + h + 1, r0:r0 + L]
            m = m_scr[h:h + 1, 0:1]
            am = jnp.where(visible, a_row, NEG_INF)
            u = jnp.maximum(m, jnp.max(am, axis=1, keepdims=True))
            b_col = jnp.sum(jnp.where(diag, b_row, 0.0), axis=1, keepdims=True)
            d_mat = jnp.exp(am - u)
            prev_scale = jnp.exp(m - u)
            guard = jnp.exp(-(b_col + u))
            s = (_dot(q_m, kt_pair) * d_mat).astype(BF16)
            v_aug = jnp.concatenate([v_ref[0, r0:r0 + L, h * HEAD_DV:(h + 1) * HEAD_DV], ones], axis=1)
            intra = _dot(s, v_aug)
            cn = cn_scr[p]
            inter = _dot(q_m, cn.astype(BF16))
            num = prev_scale * inter[:, :HEAD_DV] + intra[:, :HEAD_DV]
            den = prev_scale * inter[:, HEAD_DV:] + intra[:, HEAD_DV:]
            h_ref[0, r0:r0 + L, h * HEAD_DV:(h + 1) * HEAD_DV] = num / jnp.maximum(jnp.abs(den), guard)
            u_end = jnp.maximum(m, jnp.max(a_row, axis=1, keepdims=True))
            w_row = jnp.exp(a_row - u_end)
            decay = jnp.exp(m - u_end)
            kw = (kt_h.astype(F32) * w_row).astype(BF16)
            rows = slice(e * HEAD_DK, (e + 1) * HEAD_DK)
            cn_scr[p, rows, :] = decay * cn[rows, :] + _dot(kw, v_aug)
            b_end = jnp.sum(jnp.where(end_lane, b_row, 0.0), axis=1, keepdims=True)
            m_scr[h:h + 1, :] = jnp.broadcast_to(b_end + u_end, (1, LANES))

    @pl.when(j == pl.num_programs(1) - 1)
    def _():
        cn_out_ref[0] = cn_scr[...]
        m_out_ref[0] = m_scr[...]


def _mlstm_scan(q, kt, v, r, cn0, m0, reverse):
    bsz, t_len, _ = q.shape
    tb = _tile(t_len, 4 * SCAN_CHUNK)
    nblk = t_len // tb
    blk = (lambda j: nblk - 1 - j) if reverse else (lambda j: j)
    d = 1 if reverse else 0
    half = N_HEADS // 2
    return pl.pallas_call(
        functools.partial(_scan_kernel, reverse=reverse, n_chunks=tb // SCAN_CHUNK),
        grid=(bsz, nblk),
        in_specs=[pl.BlockSpec((1, tb, QK_W), lambda b, j: (b, blk(j), 0)),
                  pl.BlockSpec((1, QK_W, tb), lambda b, j: (b, 0, blk(j))),
                  pl.BlockSpec((1, tb, V_W), lambda b, j: (b, blk(j), 0)),
                  pl.BlockSpec((1, 2 * N_HEADS, tb), lambda b, j: (b, d, blk(j))),
                  pl.BlockSpec((1, half, 2 * HEAD_DK, 2 * HEAD_DV), lambda b, j: (b, 0, 0, 0)),
                  pl.BlockSpec((1, N_HEADS, LANES), lambda b, j: (b, 0, 0))],
        out_specs=(pl.BlockSpec((1, tb, V_W), lambda b, j: (b, blk(j), 0)),
                   pl.BlockSpec((1, half, 2 * HEAD_DK, 2 * HEAD_DV), lambda b, j: (b, 0, 0, 0)),
                   pl.BlockSpec((1, N_HEADS, LANES), lambda b, j: (b, 0, 0))),
        out_shape=(jax.ShapeDtypeStruct((bsz, t_len, V_W), F32),
                   jax.ShapeDtypeStruct((bsz, half, 2 * HEAD_DK, 2 * HEAD_DV), F32),
                   jax.ShapeDtypeStruct((bsz, N_HEADS, LANES), F32)),
        scratch_shapes=[pltpu.VMEM((half, 2 * HEAD_DK, 2 * HEAD_DV), F32),
                        pltpu.VMEM((N_HEADS, LANES), F32)],
        compiler_params=_params("parallel", "arbitrary"),
        name="mlstm_scan_bwd" if reverse else "mlstm_scan_fwd",
    )(q, kt, v, r, cn0, m0)


def _pack_state(c, n, m):
    bsz = c.shape[0]
    cn = jnp.concatenate([c, jnp.broadcast_to(n[..., None], n.shape + (HEAD_DV,))], axis=-1)
    cn = cn.reshape(bsz, N_HEADS // 2, 2 * HEAD_DK, 2 * HEAD_DV)
    return cn.astype(F32), jnp.broadcast_to(m[..., None], (bsz, N_HEADS, LANES)).astype(F32)


def _unpack_state(cn, m):
    bsz = cn.shape[0]
    cn = cn.reshape(bsz, N_HEADS, HEAD_DK, 2 * HEAD_DV)
    return cn[..., :HEAD_DV], cn[..., HEAD_DV], m[..., 0]


def _mlstm_out_kernel(hf_ref, hb_ref, o_ref, x_ref, mod_ref, hg_ref, w_ref, out_ref):
    hs = hf_ref[0] + hb_ref[0]
    parts = []
    for h in range(N_HEADS):
        z = hs[:, h * HEAD_DV:(h + 1) * HEAD_DV]
        parts.append(z * lax.rsqrt(jnp.mean(z * z, axis=-1, keepdims=True) + EPS))
    hn = jnp.concatenate(parts, axis=1) * hg_ref[...] * jax.nn.sigmoid(o_ref[0])
    y = _dot(hn.astype(BF16), w_ref[...])
    out_ref[0] = x_ref[0] + mod_ref[0, 2:3, :] * y


def _mlstm_out(hf, hb, o, x, mod, head_g, w_out, per_batch_mod):
    bsz, t_len, d = x.shape
    tm = _tile(t_len, 512)
    bidx = (lambda b: b) if per_batch_mod else (lambda b: 0)
    tok = lambda w: pl.BlockSpec((1, tm, w), lambda b, i: (b, i, 0))
    return pl.pallas_call(
        _mlstm_out_kernel,
        grid=(bsz, t_len // tm),
        in_specs=[tok(V_W), tok(V_W), tok(V_W), tok(d),
                  pl.BlockSpec((1, 6, d), lambda b, i: (bidx(b), 0, 0)),
                  pl.BlockSpec((1, V_W), lambda b, i: (0, 0)),
                  pl.BlockSpec((V_W, d), lambda b, i: (0, 0))],
        out_specs=tok(d),
        out_shape=jax.ShapeDtypeStruct((bsz, t_len, d), F32),
        compiler_params=_params("parallel", "parallel"),
        name="mlstm_out",
    )(hf, hb, o, x, mod, head_g.reshape(1, V_W).astype(F32), w_out.astype(BF16))


def _ffn_kernel(x_ref, mod_ref, ng_ref, w1_ref, w3_ref, w2_ref, out_ref, hn_scr, acc_scr):
    f = pl.program_id(2)

    @pl.when(f == 0)
    def _():
        hn = _norm_mod(x_ref[0], ng_ref[...], mod_ref[0, 4:5, :], mod_ref[0, 3:4, :])
        hn_scr[...] = hn.astype(BF16)
        acc_scr[...] = jnp.zeros_like(acc_scr)

    hb = hn_scr[...]
    h1 = _dot(hb, w1_ref[...])
    h3 = _dot(hb, w3_ref[...])
    act = (h1 * jax.nn.sigmoid(h1) * h3).astype(BF16)
    acc_scr[...] += _dot(act, w2_ref[...])

    @pl.when(f == pl.num_programs(2) - 1)
    def _():
        out_ref[0] = x_ref[0] + mod_ref[0, 5:6, :] * acc_scr[...]


def _ffn(x, mod, ng, w1, w3, w2, per_batch_mod):
    bsz, t_len, d = x.shape
    d_ff = w1.shape[1]
    tm = _tile(t_len, 1024)
    tf = _tile(d_ff, 512)
    bidx = (lambda b: b) if per_batch_mod else (lambda b: 0)
    return pl.pallas_call(
        _ffn_kernel,
        grid=(bsz, t_len // tm, d_ff // tf),
        in_specs=[pl.BlockSpec((1, tm, d), lambda b, i, f: (b, i, 0)),
                  pl.BlockSpec((1, 6, d), lambda b, i, f: (bidx(b), 0, 0)),
                  pl.BlockSpec((1, d), lambda b, i, f: (0, 0)),
                  pl.BlockSpec((d, tf), lambda b, i, f: (0, f)),
                  pl.BlockSpec((d, tf), lambda b, i, f: (0, f)),
                  pl.BlockSpec((tf, d), lambda b, i, f: (f, 0))],
        out_specs=pl.BlockSpec((1, tm, d), lambda b, i, f: (b, i, 0)),
        out_shape=jax.ShapeDtypeStruct((bsz, t_len, d), F32),
        scratch_shapes=[pltpu.VMEM((tm, d), BF16), pltpu.VMEM((tm, d), F32)],
        compiler_params=_params("parallel", "parallel", "arbitrary"),
        name="dense_swiglu",
    )(x, mod, ng.reshape(1, d), w1.astype(BF16), w3.astype(BF16), w2.astype(BF16))


ROUTE_TILE = 512
GROUP_TILE = 512
INFO_E1, INFO_E2, INFO_P1, INFO_P2, INFO_R1, INFO_R2 = range(6)


def _lane_pick(rec, lane, k):
    return jnp.sum(jnp.where(lane == k, rec, 0.0), axis=-1, keepdims=True)


def _router_kernel(x_ref, mod_ref, ng_ref, wr_ref, tri_ref, hn_ref, info_ref, cnt_ref, carry, *, n_exp):
    @pl.when((pl.program_id(0) == 0) & (pl.program_id(1) == 0))
    def _():
        carry[...] = jnp.zeros_like(carry)

    hn = _norm_mod(x_ref[0], ng_ref[...], mod_ref[0, 4:5, :], mod_ref[0, 3:4, :])
    hn_ref[0] = hn
    logits = jnp.dot(hn, wr_ref[...], preferred_element_type=F32, precision=HIGHEST)
    lane = lax.broadcasted_iota(jnp.int32, logits.shape, 1).astype(F32)
    logits = jnp.where(lane < n_exp, logits, NEG_INF)
    v1 = jnp.max(logits, axis=-1, keepdims=True)
    i1 = jnp.min(jnp.where(logits == v1, lane, float(LANES)), axis=-1, keepdims=True)
    rest = jnp.where(lane == i1, NEG_INF, logits)
    v2 = jnp.max(rest, axis=-1, keepdims=True)
    i2 = jnp.min(jnp.where(rest == v2, lane, float(LANES)), axis=-1, keepdims=True)
    e2 = jnp.exp(v2 - v1)
    den = 1.0 + e2
    chosen = jnp.where((lane == i1) | (lane == i2), 1.0, 0.0)
    before = _dot(tri_ref[...], chosen.astype(BF16)) + carry[...]
    r1 = jnp.sum(jnp.where(lane == i1, before, 0.0), axis=-1, keepdims=True)
    r2 = jnp.sum(jnp.where(lane == i2, before, 0.0), axis=-1, keepdims=True)
    total = carry[...] + jnp.sum(chosen, axis=0, keepdims=True)
    carry[...] = total
    cnt_ref[...] = total
    rec = jnp.zeros_like(logits)
    for k, val in ((INFO_E1, i1), (INFO_E2, i2), (INFO_P1, 1.0 / den), (INFO_P2, e2 / den),
                   (INFO_R1, r1), (INFO_R2, r2)):
        rec = jnp.where(lane == k, val, rec)
    info_ref[0] = rec


def _router(x, mod, ng, w_router, per_batch_mod):
    bsz, t_len, d = x.shape
    n_exp = w_router.shape[1]
    tm = _tile(t_len, ROUTE_TILE)
    wr = jnp.zeros((d, LANES), F32).at[:, :n_exp].set(w_router)
    pos = np.arange(tm)
    tri = jnp.asarray((pos[None, :] < pos[:, None]).astype(np.float32)).astype(BF16)
    bidx = (lambda b: b) if per_batch_mod else (lambda b: 0)
    return pl.pallas_call(
        functools.partial(_router_kernel, n_exp=n_exp),
        grid=(bsz, t_len // tm),
        in_specs=[pl.BlockSpec((1, tm, d), lambda b, i: (b, i, 0)),
                  pl.BlockSpec((1, 6, d), lambda b, i: (bidx(b), 0, 0)),
                  pl.BlockSpec((1, d), lambda b, i: (0, 0)),
                  pl.BlockSpec((d, LANES), lambda b, i: (0, 0)),
                  pl.BlockSpec((tm, tm), lambda b, i: (0, 0))],
        out_specs=(pl.BlockSpec((1, tm, d), lambda b, i: (b, i, 0)),
                   pl.BlockSpec((1, tm, LANES), lambda b, i: (b, i, 0)),
                   pl.BlockSpec((1, LANES), lambda b, i: (0, 0))),
        out_shape=(jax.ShapeDtypeStruct((bsz, t_len, d), F32),
                   jax.ShapeDtypeStruct((bsz, t_len, LANES), F32),
                   jax.ShapeDtypeStruct((1, LANES), F32)),
        scratch_shapes=[pltpu.VMEM((1, LANES), F32)],
        compiler_params=_params("arbitrary", "arbitrary"),
        name="moe_router",
    )(x, mod, ng.reshape(1, d), wr, tri)


def _dispatch_kernel(zmask_ref, pos_ref, hn_ref, xs_ref, zeros, sem, *, tb, n_row_tiles):
    def row_copy(t, slot):
        return pltpu.make_async_copy(hn_ref.at[pl.ds(t, 1)], xs_ref.at[pl.ds(slot, 1)], sem)

    def tile_fill(r):
        row0 = pl.multiple_of(r * GROUP_TILE, GROUP_TILE)
        return pltpu.make_async_copy(zeros, xs_ref.at[pl.ds(row0, GROUP_TILE)], sem)

    @pl.when(pl.program_id(0) == 0)
    def _():
        zeros[...] = jnp.zeros_like(zeros)

        def fill(r, c):
            @pl.when(zmask_ref[r] != 0)
            def _():
                tile_fill(r).start()
            return c

        def fill_done(r, c):
            @pl.when(zmask_ref[r] != 0)
            def _():
                tile_fill(r).wait()
            return c

        lax.fori_loop(0, n_row_tiles, fill, 0)
        lax.fori_loop(0, n_row_tiles, fill_done, 0)

    base = pl.program_id(0) * tb

    def issue(r, c):
        row_copy(base + r, pos_ref[0, 0, r]).start()
        row_copy(base + r, pos_ref[0, 1, r]).start()
        return c

    def drain(r, c):
        row_copy(0, 0).wait()
        row_copy(0, 0).wait()
        return c

    lax.fori_loop(0, tb, issue, 0, unroll=8)
    lax.fori_loop(0, tb, drain, 0, unroll=8)


def _dispatch(zmask, pos, hn, s_max):
    n, d = hn.shape
    n_tiles, _, tb = pos.shape
    return pl.pallas_call(
        functools.partial(_dispatch_kernel, tb=tb, n_row_tiles=zmask.shape[0]),
        grid_spec=pltpu.PrefetchScalarGridSpec(
            num_scalar_prefetch=1,
            grid=(n_tiles,),
            in_specs=[pl.BlockSpec((1, 2, tb), lambda i, zm: (i, 0, 0), memory_space=pltpu.SMEM),
                      pl.BlockSpec(memory_space=pl.ANY)],
            out_specs=pl.BlockSpec(memory_space=pl.ANY),
            scratch_shapes=[pltpu.VMEM((GROUP_TILE, d), F32), pltpu.SemaphoreType.DMA(())]),
        out_shape=jax.ShapeDtypeStruct((s_max, d), F32),
        compiler_params=_params("arbitrary"),
        name="moe_dispatch",
    )(zmask, pos, hn)


def _group_ffn_kernel(te_ref, tx_ref, tv_ref, x_ref, w1_ref, w3_ref, w2_ref, o_ref, xb_scr, acc_scr):
    r, f = pl.program_id(0), pl.program_id(1)
    last = pl.num_programs(1) - 1

    @pl.when(tv_ref[r] != 0)
    def _():
        @pl.when(f == 0)
        def _():
            xb_scr[...] = x_ref[...].astype(BF16)

        xb = xb_scr[...]
        h1 = _dot(xb, w1_ref[0])
        h3 = _dot(xb, w3_ref[0])
        act = (h1 * jax.nn.sigmoid(h1) * h3).astype(BF16)
        y = _dot(act, w2_ref[0])

        @pl.when(f == 0)
        def _():
            acc_scr[...] = y

        @pl.when(f != 0)
        def _():
            acc_scr[...] += y

        @pl.when(f == last)
        def _():
            o_ref[...] = acc_scr[...]

    @pl.when((tv_ref[r] == 0) & (f == last))
    def _():
        o_ref[...] = jnp.zeros_like(o_ref)


def _group_ffn(te, tx, tv, xs, w1, w3, w2):
    s_max, d = xs.shape
    d_ff = w1.shape[2]
    tf = _tile(d_ff, 512)
    nf = d_ff // tf
    fidx = lambda r, f, tv: f * tv[r] + (nf - 1) * (1 - tv[r])
    return pl.pallas_call(
        _group_ffn_kernel,
        grid_spec=pltpu.PrefetchScalarGridSpec(
            num_scalar_prefetch=3,
            grid=(s_max // GROUP_TILE, nf),
            in_specs=[pl.BlockSpec((GROUP_TILE, d), lambda r, f, te, tx, tv: (tx[r], 0)),
                      pl.BlockSpec((1, d, tf), lambda r, f, te, tx, tv: (te[r], 0, fidx(r, f, tv))),
                      pl.BlockSpec((1, d, tf), lambda r, f, te, tx, tv: (te[r], 0, fidx(r, f, tv))),
                      pl.BlockSpec((1, tf, d), lambda r, f, te, tx, tv: (te[r], fidx(r, f, tv), 0))],
            out_specs=pl.BlockSpec((GROUP_TILE, d), lambda r, f, te, tx, tv: (r, 0)),
            scratch_shapes=[pltpu.VMEM((GROUP_TILE, d), BF16), pltpu.VMEM((GROUP_TILE, d), F32)]),
        out_shape=jax.ShapeDtypeStruct((s_max, d), F32),
        compiler_params=_params("parallel", "arbitrary"),
        name="moe_group_swiglu",
    )(te, tx, tv, xs, w1, w3, w2)


def _combine_kernel(pos_ref, x_ref, mod_ref, info_ref, ys_ref, out_ref, buf, sem, *, tb):
    def row_copy(slot, k, r):
        return pltpu.make_async_copy(ys_ref.at[pl.ds(slot, 1)], buf.at[k, pl.ds(r, 1)], sem)

    def issue(r, c):
        row_copy(pos_ref[0, 0, r], 0, r).start()
        row_copy(pos_ref[0, 1, r], 1, r).start()
        return c

    def drain(r, c):
        row_copy(0, 0, 0).wait()
        row_copy(0, 1, 0).wait()
        return c

    lax.fori_loop(0, tb, issue, 0, unroll=8)
    lax.fori_loop(0, tb, drain, 0, unroll=8)
    rec = info_ref[...]
    lane = lax.broadcasted_iota(jnp.int32, rec.shape, 1)
    y = _lane_pick(rec, lane, INFO_P1) * buf[0] + _lane_pick(rec, lane, INFO_P2) * buf[1]
    out_ref[...] = x_ref[...] + mod_ref[0, 5:6, :] * y


def _combine(pos, x, mod, info, ys, t_len, per_batch_mod):
    n, d = x.shape
    n_tiles, _, tb = pos.shape
    bidx = (lambda i: (i * tb) // t_len) if per_batch_mod else (lambda i: 0)
    return pl.pallas_call(
        functools.partial(_combine_kernel, tb=tb),
        grid=(n_tiles,),
        in_specs=[pl.BlockSpec((1, 2, tb), lambda i: (i, 0, 0), memory_space=pltpu.SMEM),
                  pl.BlockSpec((tb, d), lambda i: (i, 0)),
                  pl.BlockSpec((1, 6, d), lambda i: (bidx(i), 0, 0)),
                  pl.BlockSpec((tb, LANES), lambda i: (i, 0)),
                  pl.BlockSpec(memory_space=pl.ANY)],
        out_specs=pl.BlockSpec((tb, d), lambda i: (i, 0)),
        out_shape=jax.ShapeDtypeStruct((n, d), F32),
        scratch_shapes=[pltpu.VMEM((2, tb, d), F32), pltpu.SemaphoreType.DMA(())],
        compiler_params=_params("arbitrary"),
        name="moe_combine",
    )(pos, x, mod, info, ys)


def _moe(x, mod, ng, w_router, w1, w3, w2, per_batch_mod):
    bsz, t_len, d = x.shape
    n = bsz * t_len
    n_exp = w_router.shape[1]
    hn, info, cnt = _router(x, mod, ng, w_router, per_batch_mod)
    info = info.reshape(n, LANES)
    e1, e2 = info[:, INFO_E1].astype(jnp.int32), info[:, INFO_E2].astype(jnp.int32)
    r1, r2 = info[:, INFO_R1].astype(jnp.int32), info[:, INFO_R2].astype(jnp.int32)
    counts = cnt[0, :n_exp].astype(jnp.int32)
    padded = ((counts + GROUP_TILE - 1) // GROUP_TILE) * GROUP_TILE
    ends = jnp.cumsum(padded)
    starts = ends - padded
    tb = _tile(n, ROUTE_TILE)
    pos = jnp.stack([(starts[e1] + r1).reshape(n // tb, tb), (starts[e2] + r2).reshape(n // tb, tb)], axis=1)
    s_max = 2 * n + n_exp * GROUP_TILE
    tile_row = jnp.arange(s_max // GROUP_TILE, dtype=jnp.int32) * GROUP_TILE
    tv = (tile_row < ends[-1]).astype(jnp.int32)
    te = jnp.minimum(jnp.searchsorted(ends, tile_row, side="right"), n_exp - 1).astype(jnp.int32)
    tx = (jnp.minimum(tile_row, ends[-1] - GROUP_TILE) // GROUP_TILE).astype(jnp.int32)
    region_end = ((tile_row + GROUP_TILE)[:, None] == ends[None, :]) & (padded > 0)[None, :]
    zmask = jnp.maximum(1 - tv, jnp.any(region_end, axis=1).astype(jnp.int32))
    xs = _dispatch(zmask, pos, hn.reshape(n, d), s_max)
    ys = _group_ffn(te, tx, tv, xs, w1.astype(BF16), w3.astype(BF16), w2.astype(BF16))
    out = _combine(pos, x.reshape(n, d), mod, info, ys, t_len, per_batch_mod)
    return out.reshape(bsz, t_len, d)


def _dft_channel_kernel(x_ref, mod_ref, ng_ref, cs_ref, y_ref, *, gw):
    hn = _norm_mod(x_ref[0], ng_ref[...], mod_ref[0, 1:2, :], mod_ref[0, 0:1, :]).astype(BF16)
    for g in range(hn.shape[1] // gw):
        y = _dot(hn[:, g * gw:(g + 1) * gw], cs_ref[...])
        y_ref[0, 0, :, g * gw:(g + 1) * gw] = y[:, :gw].astype(BF16)
        y_ref[0, 1, :, g * gw:(g + 1) * gw] = y[:, gw:].astype(BF16)


def _dft_time_kernel(w_ref, y_ref, x_ref, mod_ref, fw_ref, fb_ref, out_ref, acc_scr):
    k = pl.program_id(2)

    @pl.when(k == 0)
    def _():
        acc_scr[...] = jnp.zeros_like(acc_scr)

    acc_scr[...] += _dot(w_ref[...], y_ref[0])

    @pl.when(k == pl.num_programs(2) - 1)
    def _():
        y = _dot(acc_scr[...].astype(BF16), fw_ref[...]) + fb_ref[...]
        out_ref[0] = x_ref[0] + mod_ref[0, 2:3, :] * y


def _dft_matrix(n, scale):
    idx = (np.arange(n)[:, None] * np.arange(n)[None, :]) % n
    ang = 2.0 * np.pi * idx.astype(np.float64) / n
    return np.cos(ang) * scale, np.sin(ang) * scale


def _fourier_mixer(x, mod, ng, fn_w, fn_b, per_batch_mod):
    bsz, t_len, d = x.shape
    gw = d // N_GROUPS
    cg, sg = _dft_matrix(gw, gw ** -0.5)
    cs = jnp.asarray(np.concatenate([cg, sg], axis=1).astype(np.float32)).astype(BF16)
    tm = _tile(t_len, 512)
    bidx = (lambda b: b) if per_batch_mod else (lambda b: 0)
    y = pl.pallas_call(
        functools.partial(_dft_channel_kernel, gw=gw),
        grid=(bsz, t_len // tm),
        in_specs=[pl.BlockSpec((1, tm, d), lambda b, i: (b, i, 0)),
                  pl.BlockSpec((1, 6, d), lambda b, i: (bidx(b), 0, 0)),
                  pl.BlockSpec((1, d), lambda b, i: (0, 0)),
                  pl.BlockSpec((gw, 2 * gw), lambda b, i: (0, 0))],
        out_specs=pl.BlockSpec((1, 2, tm, d), lambda b, i: (b, 0, i, 0)),
        out_shape=jax.ShapeDtypeStruct((bsz, 2, t_len, d), BF16),
        compiler_params=_params("parallel", "parallel"),
        name="dft_channel",
    )(x, mod, ng.reshape(1, d), cs)
    y = y.reshape(bsz, 2 * t_len, d)
    kk = jnp.arange(t_len, dtype=jnp.int32)[:, None]
    tt = jnp.arange(t_len, dtype=jnp.int32)[None, :]
    ang = (2.0 * np.pi / t_len) * ((kk * tt) % t_len).astype(F32)
    scale = t_len ** -0.5
    wt = jnp.concatenate([jnp.cos(ang) * scale, jnp.sin(ang) * (-scale)], axis=1).astype(BF16)
    tmk = _tile(t_len, 1024)
    tk = _tile(2 * t_len, 2048)
    return pl.pallas_call(
        _dft_time_kernel,
        grid=(bsz, t_len // tmk, 2 * t_len // tk),
        in_specs=[pl.BlockSpec((tmk, tk), lambda b, i, k: (i, k)),
                  pl.BlockSpec((1, tk, d), lambda b, i, k: (b, k, 0)),
                  pl.BlockSpec((1, tmk, d), lambda b, i, k: (b, i, 0)),
                  pl.BlockSpec((1, 6, d), lambda b, i, k: (bidx(b), 0, 0)),
                  pl.BlockSpec((d, d), lambda b, i, k: (0, 0)),
                  pl.BlockSpec((1, d), lambda b, i, k: (0, 0))],
        out_specs=pl.BlockSpec((1, tmk, d), lambda b, i, k: (b, i, 0)),
        out_shape=jax.ShapeDtypeStruct((bsz, t_len, d), F32),
        scratch_shapes=[pltpu.VMEM((tmk, d), F32)],
        compiler_params=_params("parallel", "parallel", "arbitrary"),
        name="dft_time",
    )(wt, y, x, mod, fn_w.astype(BF16), fn_b.reshape(1, d).astype(F32))


def _final_norm_kernel(x_ref, g_ref, o_ref):
    x = x_ref[0]
    o_ref[0] = x * lax.rsqrt(jnp.mean(x * x, axis=-1, keepdims=True) + EPS) * g_ref[...]


def _final_norm(x, g):
    bsz, t_len, d = x.shape
    tm = _tile(t_len, 1024)
    return pl.pallas_call(
        _final_norm_kernel,
        grid=(bsz, t_len // tm),
        in_specs=[pl.BlockSpec((1, tm, d), lambda b, i: (b, i, 0)), pl.BlockSpec((1, d), lambda b, i: (0, 0))],
        out_specs=pl.BlockSpec((1, tm, d), lambda b, i: (b, i, 0)),
        out_shape=jax.ShapeDtypeStruct((bsz, t_len, d), F32),
        compiler_params=_params("parallel", "parallel"),
        name="final_norm",
    )(x, g.reshape(1, d))


def _trunk(x, mod, cache, use_rope, per_batch_mod, p):
    bsz, t_len, d = x.shape
    depth = mod.shape[0]
    flat = (lambda a: a) if per_batch_mod else (lambda a: a.reshape(1, bsz * t_len, a.shape[-1]))
    unflat = (lambda a: a) if per_batch_mod else (lambda a: a.reshape(bsz, t_len, a.shape[-1]))
    states = []
    for i in range(depth):
        j = i // 2
        m_i = mod[i]
        if i % 2 == 0:
            q, kt, v, o, r = _mlstm_inproj(x, m_i, p["norm_g"][i, 0], p["ml_w_in"][j], p["ml_b_gate"][j],
                                           use_rope, per_batch_mod)
            outs = []
            for direction in range(2):
                if cache is None:
                    cn0 = jnp.zeros((bsz, N_HEADS // 2, 2 * HEAD_DK, 2 * HEAD_DV), F32)
                    m0 = jnp.zeros((bsz, N_HEADS, LANES), F32)
                else:
                    cn0, m0 = _pack_state(cache[0][:, j, direction], cache[1][:, j, direction],
                                          cache[2][:, j, direction])
                outs.append(_mlstm_scan(q, kt, v, r, cn0, m0, reverse=bool(direction)))
            states.append([_unpack_state(cn, m) for (_, cn, m) in outs])
            x = _mlstm_out(outs[0][0], outs[1][0], o, x, m_i, p["ml_head_g"][j], p["ml_w_out"][j], per_batch_mod)
            x = unflat(_ffn(flat(x), m_i, p["norm_g"][i, 1], p["ffn_w1"][j], p["ffn_w3"][j], p["ffn_w2"][j],
                            per_batch_mod))
        else:
            x = _fourier_mixer(x, m_i, p["norm_g"][i, 0], p["fn_w"][j], p["fn_b"][j], per_batch_mod)
            x = unflat(_moe(flat(x), m_i, p["norm_g"][i, 1], p["moe_router"][j], p["moe_w1"][j], p["moe_w3"][j],
                            p["moe_w2"][j], per_batch_mod))
    y = unflat(_final_norm(flat(x), p["final_g"]))
    return y, states


def kernel(x_prompt, x_sample, state_C, state_n, state_m, c, c_ctx, w_mod, b_mod, norm_g, final_g,
           ml_w_in, ml_b_gate, ml_head_g, ml_w_out, fn_w, fn_b, ffn_w1, ffn_w3, ffn_w2,
           moe_router, moe_w1, moe_w3, moe_w2):
    p = dict(norm_g=norm_g, final_g=final_g, ml_w_in=ml_w_in, ml_b_gate=ml_b_gate, ml_head_g=ml_head_g,
             ml_w_out=ml_w_out, fn_w=fn_w, fn_b=fn_b, ffn_w1=ffn_w1, ffn_w3=ffn_w3, ffn_w2=ffn_w2,
             moe_router=moe_router, moe_w1=moe_w1, moe_w3=moe_w3, moe_w2=moe_w2)
    depth, d = w_mod.shape[0], w_mod.shape[1]
    n_dec = c.shape[0]
    rows = ((n_dec + 1 + 7) // 8) * 8
    cond = jnp.zeros((rows, d), F32).at[:n_dec].set(c).at[n_dec].set(c_ctx)
    mod = _mod_table(cond, w_mod, b_mod).reshape(depth, rows, 6, d)
    y_prompt, st = _trunk(x_prompt, mod[:, n_dec:n_dec + 1], None, False, False, p)
    y_sample, _ = _trunk(x_sample, mod[:, :n_dec], (state_C, state_n, state_m), True, True, p)
    new_c = jnp.stack([jnp.stack([s[0][0], s[1][0]], axis=1) for s in st], axis=1)
    new_n = jnp.stack([jnp.stack([s[0][1], s[1][1]], axis=1) for s in st], axis=1)
    new_m = jnp.stack([jnp.stack([s[0][2], s[1][2]], axis=1) for s in st], axis=1)
    return (y_prompt, y_sample, new_c.astype(x_prompt.dtype), new_n.astype(x_prompt.dtype),
            new_m.astype(x_prompt.dtype))
```

```python
import functools

import numpy as np
import jax
import jax.numpy as jnp
from jax import lax
from jax.experimental import pallas as pl
from jax.experimental.pallas import tpu as pltpu

F32 = jnp.float32
BF16 = jnp.bfloat16
HIGHEST = lax.Precision.HIGHEST

EPS = 1e-6
N_HEADS = 8
HEAD_DK = 64
HEAD_DV = 128
QK_W = N_HEADS * HEAD_DK
V_W = N_HEADS * HEAD_DV
GRID_W = 64
ROPE_BASE = 10000.0
N_GROUPS = 4
DFT_SPLIT = 64
N_GATES = 4 * N_HEADS
LANES = 128
SCAN_CHUNK = 128
VMEM_LIMIT = 56 * 1024 * 1024
NEG_INF = float("-inf")


def _params(*sem):
    return pltpu.CompilerParams(dimension_semantics=sem, vmem_limit_bytes=VMEM_LIMIT)


def _tile(n, pref):
    t = min(n, pref)
    assert n % t == 0, (n, pref)
    return t


def _norm_mod(x, gain, scale, shift):
    ms = jnp.mean(x * x, axis=-1, keepdims=True)
    return x * lax.rsqrt(ms + EPS) * gain * (1.0 + scale) + shift


def _dot(a, b):
    return jnp.dot(a, b, preferred_element_type=F32)


def _dot_nt(a, b):
    return lax.dot_general(a, b, (((1,), (1,)), ((), ())), preferred_element_type=F32)


def _log_sigmoid(x):
    return jnp.minimum(x, 0.0) - jnp.log1p(jnp.exp(-jnp.abs(x)))


def _mod_kernel(c_ref, w_ref, b_ref, o_ref):
    c = c_ref[...]
    s = c * jax.nn.sigmoid(c)
    o_ref[0] = jnp.dot(s, w_ref[0], preferred_element_type=F32, precision=HIGHEST) + b_ref[0]


def _mod_table(cond, w_mod, b_mod):
    depth, d, n = w_mod.shape
    rows = cond.shape[0]
    tn = _tile(n, 1536)
    return pl.pallas_call(
        _mod_kernel,
        grid=(depth, n // tn),
        in_specs=[pl.BlockSpec((rows, d), lambda l, j: (0, 0)),
                  pl.BlockSpec((1, d, tn), lambda l, j: (l, 0, j)),
                  pl.BlockSpec((1, 1, tn), lambda l, j: (l, 0, j))],
        out_specs=pl.BlockSpec((1, rows, tn), lambda l, j: (l, 0, j)),
        out_shape=jax.ShapeDtypeStruct((depth, rows, n), F32),
        compiler_params=_params("parallel", "parallel"),
        name="adaln_table",
    )(cond, w_mod, b_mod.reshape(depth, 1, n))


def _rope_tables(t_len):
    pos = np.arange(t_len)
    row = (pos // GRID_W).astype(np.float32)
    col = (pos % GRID_W).astype(np.float32)
    nf = HEAD_DK // 4
    inv = (np.float32(ROPE_BASE) ** (-np.arange(nf, dtype=np.float32) / nf)).astype(np.float32)
    d = np.arange(HEAD_DK)
    p = np.where(d[None, :] < HEAD_DK // 2, row[:, None], col[:, None]).astype(np.float32)
    ang = p * inv[d % nf][None, :]
    sign = np.where((d % (2 * nf)) < nf, -1.0, 1.0).astype(np.float32)
    return np.cos(ang).astype(np.float32), (np.sin(ang) * sign[None, :]).astype(np.float32)


def _inproj_kernel(*refs, use_rope):
    if use_rope:
        (x_ref, mod_ref, ng_ref, wq_ref, wkt_ref, wv_ref, wo_ref, wgt_ref, bg_ref, trif_ref, trib_ref,
         cq_ref, sq_ref, ck_ref, sk_ref, q_ref, kt_ref, v_ref, o_ref, r_ref) = refs
    else:
        (x_ref, mod_ref, ng_ref, wq_ref, wkt_ref, wv_ref, wo_ref, wgt_ref, bg_ref, trif_ref, trib_ref,
         q_ref, kt_ref, v_ref, o_ref, r_ref) = refs
    hn = _norm_mod(x_ref[0], ng_ref[...], mod_ref[0, 1:2, :], mod_ref[0, 0:1, :])
    hb = hn.astype(BF16)
    q = _dot(hb, wq_ref[...]) * (HEAD_DK ** -0.5)
    kt = _dot_nt(wkt_ref[...], hb)
    if use_rope:
        nf = HEAD_DK // 4
        lane = lax.broadcasted_iota(jnp.int32, (q.shape[0], LANES), 1)
        first_q = (lane % (2 * nf)) < nf
        sub = lax.broadcasted_iota(jnp.int32, (LANES, kt.shape[1]), 0)
        first_k = (sub % (2 * nf)) < nf
        cq, sq, ck, sk = cq_ref[...], sq_ref[...], ck_ref[...], sk_ref[...]
        for s in range(QK_W // LANES):
            qs = q[:, s * LANES:(s + 1) * LANES]
            sw = jnp.where(first_q, pltpu.roll(qs, LANES - nf, 1), pltpu.roll(qs, nf, 1))
            q_ref[0, :, s * LANES:(s + 1) * LANES] = (qs * cq + sw * sq).astype(BF16)
            ks = kt[s * LANES:(s + 1) * LANES, :]
            sw = jnp.where(first_k, pltpu.roll(ks, LANES - nf, 0), pltpu.roll(ks, nf, 0))
            kt_ref[0, s * LANES:(s + 1) * LANES, :] = (ks * ck + sw * sk).astype(BF16)
    else:
        q_ref[0] = q.astype(BF16)
        kt_ref[0] = kt.astype(BF16)
    v_ref[0] = _dot(hb, wv_ref[...]).astype(BF16)
    o_ref[0] = _dot(hb, wo_ref[...])
    gt = _dot_nt(wgt_ref[...], hb) + bg_ref[...]
    h = N_HEADS
    i_f, f_f = gt[0:h], _log_sigmoid(gt[h:2 * h])
    i_b, f_b = gt[2 * h:3 * h], _log_sigmoid(gt[3 * h:4 * h])
    b_f = jnp.dot(f_f, trif_ref[...], preferred_element_type=F32, precision=HIGHEST)
    b_b = jnp.dot(f_b, trib_ref[...], preferred_element_type=F32, precision=HIGHEST)
    r_ref[0, 0:h, :] = i_f - b_f
    r_ref[0, h:2 * h, :] = b_f
    r_ref[0, 2 * h:3 * h, :] = i_b - b_b
    r_ref[0, 3 * h:4 * h, :] = b_b


def _mlstm_inproj(x, mod, ng, w_in, b_gate, use_rope, per_batch_mod):
    bsz, t_len, d = x.shape
    tm = _tile(t_len, 512)
    wq = w_in[:, :QK_W].astype(BF16)
    wkt = w_in[:, QK_W:2 * QK_W].T.astype(BF16)
    wv = w_in[:, 2 * QK_W:2 * QK_W + V_W].astype(BF16)
    wo = w_in[:, 2 * QK_W + V_W:2 * QK_W + 2 * V_W].astype(BF16)
    wgt = w_in[:, 2 * QK_W + 2 * V_W:].T.astype(BF16)
    bg = b_gate.reshape(N_GATES, 1).astype(F32)
    pos = np.arange(tm)
    same = (pos[:, None] // SCAN_CHUNK) == (pos[None, :] // SCAN_CHUNK)
    trif = jnp.asarray((same & (pos[:, None] <= pos[None, :])).astype(np.float32))
    trib = jnp.asarray((same & (pos[:, None] >= pos[None, :])).astype(np.float32))
    bidx = (lambda b: b) if per_batch_mod else (lambda b: 0)
    const = lambda shp: pl.BlockSpec(shp, lambda b, i: (0,) * len(shp))
    in_specs = [pl.BlockSpec((1, tm, d), lambda b, i: (b, i, 0)),
                pl.BlockSpec((1, 6, d), lambda b, i: (bidx(b), 0, 0)),
                const((1, d)), const((d, QK_W)), const((QK_W, d)), const((d, V_W)), const((d, V_W)),
                const((N_GATES, d)), const((N_GATES, 1)), const((tm, tm)), const((tm, tm))]
    args = [x, mod, ng.reshape(1, d), wq, wkt, wv, wo, wgt, bg, trif, trib]
    if use_rope:
        cos, sin = _rope_tables(t_len)
        rep = LANES // HEAD_DK
        args += [jnp.asarray(np.tile(cos, (1, rep))), jnp.asarray(np.tile(sin, (1, rep))),
                 jnp.asarray(np.tile(cos.T, (rep, 1))), jnp.asarray(np.tile(sin.T, (rep, 1)))]
        in_specs += [pl.BlockSpec((tm, LANES), lambda b, i: (i, 0)), pl.BlockSpec((tm, LANES), lambda b, i: (i, 0)),
                     pl.BlockSpec((LANES, tm), lambda b, i: (0, i)), pl.BlockSpec((LANES, tm), lambda b, i: (0, i))]
    out_shape = (jax.ShapeDtypeStruct((bsz, t_len, QK_W), BF16),
                 jax.ShapeDtypeStruct((bsz, QK_W, t_len), BF16),
                 jax.ShapeDtypeStruct((bsz, t_len, V_W), BF16),
                 jax.ShapeDtypeStruct((bsz, t_len, V_W), F32),
                 jax.ShapeDtypeStruct((bsz, N_GATES, t_len), F32))
    out_specs = (pl.BlockSpec((1, tm, QK_W), lambda b, i: (b, i, 0)),
                 pl.BlockSpec((1, QK_W, tm), lambda b, i: (b, 0, i)),
                 pl.BlockSpec((1, tm, V_W), lambda b, i: (b, i, 0)),
                 pl.BlockSpec((1, tm, V_W), lambda b, i: (b, i, 0)),
                 pl.BlockSpec((1, N_GATES, tm), lambda b, i: (b, 0, i)))
    return pl.pallas_call(
        functools.partial(_inproj_kernel, use_rope=use_rope),
        grid=(bsz, t_len // tm), in_specs=in_specs, out_specs=out_specs, out_shape=out_shape,
        compiler_params=_params("parallel", "parallel"),
        name="mlstm_inproj",
    )(*args)


def _scan_kernel(q_ref, kt_ref, v_ref, r_ref, cn0_ref, m0_ref, h_ref, cn_out_ref, m_out_ref,
                 cn_scr, m_scr, *, reverse, n_chunks):
    L = SCAN_CHUNK
    j = pl.program_id(1)

    @pl.when(j == 0)
    def _():
        cn_scr[...] = cn0_ref[0]
        m_scr[...] = m0_ref[0]

    row_i = lax.broadcasted_iota(jnp.int32, (L, L), 0)
    col_i = lax.broadcasted_iota(jnp.int32, (L, L), 1)
    visible = (col_i >= row_i) if reverse else (col_i <= row_i)
    diag = col_i == row_i
    upper_lanes = lax.broadcasted_iota(jnp.int32, (L, LANES), 1) >= HEAD_DK
    ones = jnp.ones((L, LANES), BF16)
    end_lane = lax.broadcasted_iota(jnp.int32, (1, L), 1) == (0 if reverse else L - 1)
    order = range(n_chunks - 1, -1, -1) if reverse else range(n_chunks)
    for c in order:
        r0 = c * L
        for h in range(N_HEADS):
            p, e = h // 2, h % 2
            q_pair = q_ref[0, r0:r0 + L, p * LANES:(p + 1) * LANES]
            q_m = jnp.where(upper_lanes if e else jnp.logical_not(upper_lanes), q_pair, jnp.zeros_like(q_pair))
            kt_pair = kt_ref[0, p * LANES:(p + 1) * LANES, r0:r0 + L]
            kt_h = kt_ref[0, h * HEAD_DK:(h + 1) * HEAD_DK, r0:r0 + L]
            a_row = r_ref[0, h:h + 1, r0:r0 + L]
            b_row = r_ref[0, N_HEADS + h:N_HEADS + h + 1, r0:r0 + L]
            m = m_scr[h:h + 1, 0:1]
            am = jnp.where(visible, a_row, NEG_INF)
            u = jnp.maximum(m, jnp.max(am, axis=1, keepdims=True))
            b_col = jnp.sum(jnp.where(diag, b_row, 0.0), axis=1, keepdims=True)
            d_mat = jnp.exp(am - u)
            prev_scale = jnp.exp(m - u)
            guard = jnp.exp(-(b_col + u))
            s = (_dot(q_m, kt_pair) * d_mat).astype(BF16)
            v_aug = jnp.concatenate([v_ref[0, r0:r0 + L, h * HEAD_DV:(h + 1) * HEAD_DV], ones], axis=1)
            intra = _dot(s, v_aug)
            cn = cn_scr[p]
            inter = _dot(q_m, cn.astype(BF16))
            num = prev_scale * inter[:, :HEAD_DV] + intra[:, :HEAD_DV]
            den = prev_scale * inter[:, HEAD_DV:] + intra[:, HEAD_DV:]
            h_ref[0, r0:r0 + L, h * HEAD_DV:(h + 1) * HEAD_DV] = num / jnp.maximum(jnp.abs(den), guard)
            u_end = jnp.maximum(m, jnp.max(a_row, axis=1, keepdims=True))
            w_row = jnp.exp(a_row - u_end)
            decay = jnp.exp(m - u_end)
            kw = (kt_h.astype(F32) * w_row).astype(BF16)
            rows = slice(e * HEAD_DK, (e + 1) * HEAD_DK)
            cn_scr[p, rows, :] = decay * cn[rows, :] + _dot(kw, v_aug)
            b_end = jnp.sum(jnp.where(end_lane, b_row, 0.0), axis=1, keepdims=True)
            m_scr[h:h + 1, :] = jnp.broadcast_to(b_end + u_end, (1, LANES))

    @pl.when(j == pl.num_programs(1) - 1)
    def _():
        cn_out_ref[0] = cn_scr[...]
        m_out_ref[0] = m_scr[...]


def _mlstm_scan(q, kt, v, r, cn0, m0, reverse):
    bsz, t_len, _ = q.shape
    tb = _tile(t_len, 4 * SCAN_CHUNK)
    nblk = t_len // tb
    blk = (lambda j: nblk - 1 - j) if reverse else (lambda j: j)
    d = 1 if reverse else 0
    half = N_HEADS // 2
    return pl.pallas_call(
        functools.partial(_scan_kernel, reverse=reverse, n_chunks=tb // SCAN_CHUNK),
        grid=(bsz, nblk),
        in_specs=[pl.BlockSpec((1, tb, QK_W), lambda b, j: (b, blk(j), 0)),
                  pl.BlockSpec((1, QK_W, tb), lambda b, j: (b, 0, blk(j))),
                  pl.BlockSpec((1, tb, V_W), lambda b, j: (b, blk(j), 0)),
                  pl.BlockSpec((1, 2 * N_HEADS, tb), lambda b, j: (b, d, blk(j))),
                  pl.BlockSpec((1, half, 2 * HEAD_DK, 2 * HEAD_DV), lambda b, j: (b, 0, 0, 0)),
                  pl.BlockSpec((1, N_HEADS, LANES), lambda b, j: (b, 0, 0))],
        out_specs=(pl.BlockSpec((1, tb, V_W), lambda b, j: (b, blk(j), 0)),
                   pl.BlockSpec((1, half, 2 * HEAD_DK, 2 * HEAD_DV), lambda b, j: (b, 0, 0, 0)),
                   pl.BlockSpec((1, N_HEADS, LANES), lambda b, j: (b, 0, 0))),
        out_shape=(jax.ShapeDtypeStruct((bsz, t_len, V_W), F32),
                   jax.ShapeDtypeStruct((bsz, half, 2 * HEAD_DK, 2 * HEAD_DV), F32),
                   jax.ShapeDtypeStruct((bsz, N_HEADS, LANES), F32)),
        scratch_shapes=[pltpu.VMEM((half, 2 * HEAD_DK, 2 * HEAD_DV), F32),
                        pltpu.VMEM((N_HEADS, LANES), F32)],
        compiler_params=_params("parallel", "arbitrary"),
        name="mlstm_scan_bwd" if reverse else "mlstm_scan_fwd",
    )(q, kt, v, r, cn0, m0)


def _pack_state(c, n, m):
    bsz = c.shape[0]
    cn = jnp.concatenate([c, jnp.broadcast_to(n[..., None], n.shape + (HEAD_DV,))], axis=-1)
    cn = cn.reshape(bsz, N_HEADS // 2, 2 * HEAD_DK, 2 * HEAD_DV)
    return cn.astype(F32), jnp.broadcast_to(m[..., None], (bsz, N_HEADS, LANES)).astype(F32)


def _unpack_state(cn, m):
    bsz = cn.shape[0]
    cn = cn.reshape(bsz, N_HEADS, HEAD_DK, 2 * HEAD_DV)
    return cn[..., :HEAD_DV], cn[..., HEAD_DV], m[..., 0]


def _mlstm_out_kernel(hf_ref, hb_ref, o_ref, x_ref, mod_ref, hg_ref, w_ref, out_ref):
    hs = hf_ref[0] + hb_ref[0]
    parts = []
    for h in range(N_HEADS):
        z = hs[:, h * HEAD_DV:(h + 1) * HEAD_DV]
        parts.append(z * lax.rsqrt(jnp.mean(z * z, axis=-1, keepdims=True) + EPS))
    hn = jnp.concatenate(parts, axis=1) * hg_ref[...] * jax.nn.sigmoid(o_ref[0])
    y = _dot(hn.astype(BF16), w_ref[...])
    out_ref[0] = x_ref[0] + mod_ref[0, 2:3, :] * y


def _mlstm_out(hf, hb, o, x, mod, head_g, w_out, per_batch_mod):
    bsz, t_len, d = x.shape
    tm = _tile(t_len, 512)
    bidx = (lambda b: b) if per_batch_mod else (lambda b: 0)
    tok = lambda w: pl.BlockSpec((1, tm, w), lambda b, i: (b, i, 0))
    return pl.pallas_call(
        _mlstm_out_kernel,
        grid=(bsz, t_len // tm),
        in_specs=[tok(V_W), tok(V_W), tok(V_W), tok(d),
                  pl.BlockSpec((1, 6, d), lambda b, i: (bidx(b), 0, 0)),
                  pl.BlockSpec((1, V_W), lambda b, i: (0, 0)),
                  pl.BlockSpec((V_W, d), lambda b, i: (0, 0))],
        out_specs=tok(d),
        out_shape=jax.ShapeDtypeStruct((bsz, t_len, d), F32),
        compiler_params=_params("parallel", "parallel"),
        name="mlstm_out",
    )(hf, hb, o, x, mod, head_g.reshape(1, V_W).astype(F32), w_out.astype(BF16))


def _ffn_kernel(x_ref, mod_ref, ng_ref, w1_ref, w3_ref, w2_ref, out_ref, hn_scr, acc_scr):
    f = pl.program_id(2)

    @pl.when(f == 0)
    def _():
        hn = _norm_mod(x_ref[0], ng_ref[...], mod_ref[0, 4:5, :], mod_ref[0, 3:4, :])
        hn_scr[...] = hn.astype(BF16)
        acc_scr[...] = jnp.zeros_like(acc_scr)

    hb = hn_scr[...]
    h1 = _dot(hb, w1_ref[...])
    h3 = _dot(hb, w3_ref[...])
    act = (h1 * jax.nn.sigmoid(h1) * h3).astype(BF16)
    acc_scr[...] += _dot(act, w2_ref[...])

    @pl.when(f == pl.num_programs(2) - 1)
    def _():
        out_ref[0] = x_ref[0] + mod_ref[0, 5:6, :] * acc_scr[...]


def _ffn(x, mod, ng, w1, w3, w2, per_batch_mod):
    bsz, t_len, d = x.shape
    d_ff = w1.shape[1]
    tm = _tile(t_len, 1024)
    tf = _tile(d_ff, 896)
    bidx = (lambda b: b) if per_batch_mod else (lambda b: 0)
    return pl.pallas_call(
        _ffn_kernel,
        grid=(bsz, t_len // tm, d_ff // tf),
        in_specs=[pl.BlockSpec((1, tm, d), lambda b, i, f: (b, i, 0)),
                  pl.BlockSpec((1, 6, d), lambda b, i, f: (bidx(b), 0, 0)),
                  pl.BlockSpec((1, d), lambda b, i, f: (0, 0)),
                  pl.BlockSpec((d, tf), lambda b, i, f: (0, f)),
                  pl.BlockSpec((d, tf), lambda b, i, f: (0, f)),
                  pl.BlockSpec((tf, d), lambda b, i, f: (f, 0))],
        out_specs=pl.BlockSpec((1, tm, d), lambda b, i, f: (b, i, 0)),
        out_shape=jax.ShapeDtypeStruct((bsz, t_len, d), F32),
        scratch_shapes=[pltpu.VMEM((tm, d), BF16), pltpu.VMEM((tm, d), F32)],
        compiler_params=_params("parallel", "parallel", "arbitrary"),
        name="dense_swiglu",
    )(x, mod, ng.reshape(1, d), w1.astype(BF16), w3.astype(BF16), w2.astype(BF16))


ROUTE_TILE = 512
GROUP_TILE = 512
INFO_E1, INFO_E2, INFO_P1, INFO_P2, INFO_R1, INFO_R2 = range(6)


def _lane_pick(rec, lane, k):
    return jnp.sum(jnp.where(lane == k, rec, 0.0), axis=-1, keepdims=True)


def _router_kernel(x_ref, mod_ref, ng_ref, wr_ref, tri_ref, hn_ref, info_ref, cnt_ref, carry, *, n_exp):
    @pl.when((pl.program_id(0) == 0) & (pl.program_id(1) == 0))
    def _():
        carry[...] = jnp.zeros_like(carry)

    hn = _norm_mod(x_ref[0], ng_ref[...], mod_ref[0, 4:5, :], mod_ref[0, 3:4, :])
    hn_ref[0] = hn
    logits = jnp.dot(hn, wr_ref[...], preferred_element_type=F32, precision=HIGHEST)
    lane = lax.broadcasted_iota(jnp.int32, logits.shape, 1).astype(F32)
    logits = jnp.where(lane < n_exp, logits, NEG_INF)
    v1 = jnp.max(logits, axis=-1, keepdims=True)
    i1 = jnp.min(jnp.where(logits == v1, lane, float(LANES)), axis=-1, keepdims=True)
    rest = jnp.where(lane == i1, NEG_INF, logits)
    v2 = jnp.max(rest, axis=-1, keepdims=True)
    i2 = jnp.min(jnp.where(rest == v2, lane, float(LANES)), axis=-1, keepdims=True)
    e2 = jnp.exp(v2 - v1)
    den = 1.0 + e2
    chosen = jnp.where((lane == i1) | (lane == i2), 1.0, 0.0)
    before = _dot(tri_ref[...], chosen.astype(BF16)) + carry[...]
    r1 = jnp.sum(jnp.where(lane == i1, before, 0.0), axis=-1, keepdims=True)
    r2 = jnp.sum(jnp.where(lane == i2, before, 0.0), axis=-1, keepdims=True)
    total = carry[...] + jnp.sum(chosen, axis=0, keepdims=True)
    carry[...] = total
    cnt_ref[...] = total
    rec = jnp.zeros_like(logits)
    for k, val in ((INFO_E1, i1), (INFO_E2, i2), (INFO_P1, 1.0 / den), (INFO_P2, e2 / den),
                   (INFO_R1, r1), (INFO_R2, r2)):
        rec = jnp.where(lane == k, val, rec)
    info_ref[0] = rec


def _router(x, mod, ng, w_router, per_batch_mod):
    bsz, t_len, d = x.shape
    n_exp = w_router.shape[1]
    tm = _tile(t_len, ROUTE_TILE)
    wr = jnp.zeros((d, LANES), F32).at[:, :n_exp].set(w_router)
    pos = np.arange(tm)
    tri = jnp.asarray((pos[None, :] < pos[:, None]).astype(np.float32)).astype(BF16)
    bidx = (lambda b: b) if per_batch_mod else (lambda b: 0)
    return pl.pallas_call(
        functools.partial(_router_kernel, n_exp=n_exp),
        grid=(bsz, t_len // tm),
        in_specs=[pl.BlockSpec((1, tm, d), lambda b, i: (b, i, 0)),
                  pl.BlockSpec((1, 6, d), lambda b, i: (bidx(b), 0, 0)),
                  pl.BlockSpec((1, d), lambda b, i: (0, 0)),
                  pl.BlockSpec((d, LANES), lambda b, i: (0, 0)),
                  pl.BlockSpec((tm, tm), lambda b, i: (0, 0))],
        out_specs=(pl.BlockSpec((1, tm, d), lambda b, i: (b, i, 0)),
                   pl.BlockSpec((1, tm, LANES), lambda b, i: (b, i, 0)),
                   pl.BlockSpec((1, LANES), lambda b, i: (0, 0))),
        out_shape=(jax.ShapeDtypeStruct((bsz, t_len, d), F32),
                   jax.ShapeDtypeStruct((bsz, t_len, LANES), F32),
                   jax.ShapeDtypeStruct((1, LANES), F32)),
        scratch_shapes=[pltpu.VMEM((1, LANES), F32)],
        compiler_params=_params("arbitrary", "arbitrary"),
        name="moe_router",
    )(x, mod, ng.reshape(1, d), wr, tri)


def _dispatch_kernel(zmask_ref, pos_ref, hn_ref, xs_ref, zeros, sem, *, tb, n_row_tiles):
    def row_copy(r, slot):
        return pltpu.make_async_copy(hn_ref.at[pl.ds(r, 1)], xs_ref.at[pl.ds(slot, 1)], sem)

    def tile_fill(r):
        row0 = pl.multiple_of(r * GROUP_TILE, GROUP_TILE)
        return pltpu.make_async_copy(zeros, xs_ref.at[pl.ds(row0, GROUP_TILE)], sem)

    @pl.when(pl.program_id(0) == 0)
    def _():
        zeros[...] = jnp.zeros_like(zeros)

        def fill(r, c):
            @pl.when(zmask_ref[r] != 0)
            def _():
                tile_fill(r).start()
            return c

        def fill_done(r, c):
            @pl.when(zmask_ref[r] != 0)
            def _():
                tile_fill(r).wait()
            return c

        lax.fori_loop(0, n_row_tiles, fill, 0)
        lax.fori_loop(0, n_row_tiles, fill_done, 0)

    def issue(r, c):
        row_copy(r, pos_ref[0, 0, r]).start()
        row_copy(r, pos_ref[0, 1, r]).start()
        return c

    def drain(r, c):
        row_copy(0, 0).wait()
        row_copy(0, 0).wait()
        return c

    lax.fori_loop(0, tb, issue, 0, unroll=8)
    lax.fori_loop(0, tb, drain, 0, unroll=8)


def _dispatch(zmask, pos, hn, s_max):
    n, d = hn.shape
    n_tiles, _, tb = pos.shape
    return pl.pallas_call(
        functools.partial(_dispatch_kernel, tb=tb, n_row_tiles=zmask.shape[0]),
        grid_spec=pltpu.PrefetchScalarGridSpec(
            num_scalar_prefetch=1,
            grid=(n_tiles,),
            in_specs=[pl.BlockSpec((1, 2, tb), lambda i, zm: (i, 0, 0), memory_space=pltpu.SMEM),
                      pl.BlockSpec((tb, d), lambda i, zm: (i, 0))],
            out_specs=pl.BlockSpec(memory_space=pl.ANY),
            scratch_shapes=[pltpu.VMEM((GROUP_TILE, d), F32), pltpu.SemaphoreType.DMA(())]),
        out_shape=jax.ShapeDtypeStruct((s_max, d), F32),
        compiler_params=_params("arbitrary"),
        name="moe_dispatch",
    )(zmask, pos, hn)


def _group_ffn_kernel(te_ref, tx_ref, tv_ref, x_ref, w1_ref, w3_ref, w2_ref, o_ref, xb_scr, acc_scr):
    r, f = pl.program_id(0), pl.program_id(1)
    last = pl.num_programs(1) - 1

    @pl.when(tv_ref[r] != 0)
    def _():
        @pl.when(f == 0)
        def _():
            xb_scr[...] = x_ref[...].astype(BF16)

        xb = xb_scr[...]
        h1 = _dot(xb, w1_ref[0])
        h3 = _dot(xb, w3_ref[0])
        act = (h1 * jax.nn.sigmoid(h1) * h3).astype(BF16)
        y = _dot(act, w2_ref[0])

        @pl.when(f == 0)
        def _():
            acc_scr[...] = y

        @pl.when(f != 0)
        def _():
            acc_scr[...] += y

        @pl.when(f == last)
        def _():
            o_ref[...] = acc_scr[...]

    @pl.when((tv_ref[r] == 0) & (f == last))
    def _():
        o_ref[...] = jnp.zeros_like(o_ref)


def _group_ffn(te, tx, tv, xs, w1, w3, w2):
    s_max, d = xs.shape
    d_ff = w1.shape[2]
    tf = _tile(d_ff, 1792)
    nf = d_ff // tf
    fidx = lambda r, f, tv: f * tv[r] + (nf - 1) * (1 - tv[r])
    return pl.pallas_call(
        _group_ffn_kernel,
        grid_spec=pltpu.PrefetchScalarGridSpec(
            num_scalar_prefetch=3,
            grid=(s_max // GROUP_TILE, nf),
            in_specs=[pl.BlockSpec((GROUP_TILE, d), lambda r, f, te, tx, tv: (tx[r], 0)),
                      pl.BlockSpec((1, d, tf), lambda r, f, te, tx, tv: (te[r], 0, fidx(r, f, tv))),
                      pl.BlockSpec((1, d, tf), lambda r, f, te, tx, tv: (te[r], 0, fidx(r, f, tv))),
                      pl.BlockSpec((1, tf, d), lambda r, f, te, tx, tv: (te[r], fidx(r, f, tv), 0))],
            out_specs=pl.BlockSpec((GROUP_TILE, d), lambda r, f, te, tx, tv: (r, 0)),
            scratch_shapes=[pltpu.VMEM((GROUP_TILE, d), BF16), pltpu.VMEM((GROUP_TILE, d), F32)]),
        out_shape=jax.ShapeDtypeStruct((s_max, d), F32),
        compiler_params=_params("parallel", "arbitrary"),
        name="moe_group_swiglu",
    )(te, tx, tv, xs, w1, w3, w2)


def _combine_kernel(pos_ref, x_ref, mod_ref, info_ref, ys_ref, out_ref, buf, sem, *, tb):
    def row_copy(slot, k, r):
        return pltpu.make_async_copy(ys_ref.at[pl.ds(slot, 1)], buf.at[k, pl.ds(r, 1)], sem)

    def issue(r, c):
        row_copy(pos_ref[0, 0, r], 0, r).start()
        row_copy(pos_ref[0, 1, r], 1, r).start()
        return c

    def drain(r, c):
        row_copy(0, 0, 0).wait()
        row_copy(0, 1, 0).wait()
        return c

    lax.fori_loop(0, tb, issue, 0, unroll=8)
    lax.fori_loop(0, tb, drain, 0, unroll=8)
    rec = info_ref[...]
    lane = lax.broadcasted_iota(jnp.int32, rec.shape, 1)
    y = _lane_pick(rec, lane, INFO_P1) * buf[0] + _lane_pick(rec, lane, INFO_P2) * buf[1]
    out_ref[...] = x_ref[...] + mod_ref[0, 5:6, :] * y


def _combine(pos, x, mod, info, ys, t_len, per_batch_mod):
    n, d = x.shape
    n_tiles, _, tb = pos.shape
    bidx = (lambda i: (i * tb) // t_len) if per_batch_mod else (lambda i: 0)
    return pl.pallas_call(
        functools.partial(_combine_kernel, tb=tb),
        grid=(n_tiles,),
        in_specs=[pl.BlockSpec((1, 2, tb), lambda i: (i, 0, 0), memory_space=pltpu.SMEM),
                  pl.BlockSpec((tb, d), lambda i: (i, 0)),
                  pl.BlockSpec((1, 6, d), lambda i: (bidx(i), 0, 0)),
                  pl.BlockSpec((tb, LANES), lambda i: (i, 0)),
                  pl.BlockSpec(memory_space=pl.ANY)],
        out_specs=pl.BlockSpec((tb, d), lambda i: (i, 0)),
        out_shape=jax.ShapeDtypeStruct((n, d), F32),
        scratch_shapes=[pltpu.VMEM((2, tb, d), F32), pltpu.SemaphoreType.DMA(())],
        compiler_params=_params("arbitrary"),
        name="moe_combine",
    )(pos, x, mod, info, ys)


def _moe(x, mod, ng, w_router, w1, w3, w2, per_batch_mod):
    bsz, t_len, d = x.shape
    n = bsz * t_len
    n_exp = w_router.shape[1]
    hn, info, cnt = _router(x, mod, ng, w_router, per_batch_mod)
    info = info.reshape(n, LANES)
    e1, e2 = info[:, INFO_E1].astype(jnp.int32), info[:, INFO_E2].astype(jnp.int32)
    r1, r2 = info[:, INFO_R1].astype(jnp.int32), info[:, INFO_R2].astype(jnp.int32)
    counts = cnt[0, :n_exp].astype(jnp.int32)
    padded = ((counts + GROUP_TILE - 1) // GROUP_TILE) * GROUP_TILE
    ends = jnp.cumsum(padded)
    starts = ends - padded
    tb = _tile(n, ROUTE_TILE)
    pos = jnp.stack([(starts[e1] + r1).reshape(n // tb, tb), (starts[e2] + r2).reshape(n // tb, tb)], axis=1)
    s_max = 2 * n + n_exp * GROUP_TILE
    tile_row = jnp.arange(s_max // GROUP_TILE, dtype=jnp.int32) * GROUP_TILE
    tv = (tile_row < ends[-1]).astype(jnp.int32)
    te = jnp.minimum(jnp.searchsorted(ends, tile_row, side="right"), n_exp - 1).astype(jnp.int32)
    tx = (jnp.minimum(tile_row, ends[-1] - GROUP_TILE) // GROUP_TILE).astype(jnp.int32)
    region_end = ((tile_row + GROUP_TILE)[:, None] == ends[None, :]) & (padded > 0)[None, :]
    zmask = jnp.maximum(1 - tv, jnp.any(region_end, axis=1).astype(jnp.int32))
    xs = _dispatch(zmask, pos, hn.reshape(n, d), s_max)
    ys = _group_ffn(te, tx, tv, xs, w1.astype(BF16), w3.astype(BF16), w2.astype(BF16))
    out = _combine(pos, x.reshape(n, d), mod, info, ys, t_len, per_batch_mod)
    return out.reshape(bsz, t_len, d)


def _dft_channel_kernel(x_ref, mod_ref, ng_ref, cs_ref, y_ref, *, gw):
    hn = _norm_mod(x_ref[0], ng_ref[...], mod_ref[0, 1:2, :], mod_ref[0, 0:1, :]).astype(BF16)
    for g in range(hn.shape[1] // gw):
        y = _dot(hn[:, g * gw:(g + 1) * gw], cs_ref[...])
        y_ref[0, 0, :, g * gw:(g + 1) * gw] = y[:, :gw].astype(BF16)
        y_ref[0, 1, :, g * gw:(g + 1) * gw] = y[:, gw:].astype(BF16)


def _dft_time_kernel(w_ref, y_ref, x_ref, mod_ref, fw_ref, fb_ref, out_ref, acc_scr):
    k = pl.program_id(2)

    @pl.when(k == 0)
    def _():
        acc_scr[...] = jnp.zeros_like(acc_scr)

    acc_scr[...] += _dot(w_ref[...], y_ref[0])

    @pl.when(k == pl.num_programs(2) - 1)
    def _():
        y = _dot(acc_scr[...].astype(BF16), fw_ref[...]) + fb_ref[...]
        out_ref[0] = x_ref[0] + mod_ref[0, 2:3, :] * y


def _dft_matrix(n, scale):
    idx = (np.arange(n)[:, None] * np.arange(n)[None, :]) % n
    ang = 2.0 * np.pi * idx.astype(np.float64) / n
    return np.cos(ang) * scale, np.sin(ang) * scale


def _fourier_mixer(x, mod, ng, fn_w, fn_b, per_batch_mod):
    bsz, t_len, d = x.shape
    gw = d // N_GROUPS
    cg, sg = _dft_matrix(gw, gw ** -0.5)
    cs = jnp.asarray(np.concatenate([cg, sg], axis=1).astype(np.float32)).astype(BF16)
    tm = _tile(t_len, 512)
    bidx = (lambda b: b) if per_batch_mod else (lambda b: 0)
    y = pl.pallas_call(
        functools.partial(_dft_channel_kernel, gw=gw),
        grid=(bsz, t_len // tm),
        in_specs=[pl.BlockSpec((1, tm, d), lambda b, i: (b, i, 0)),
                  pl.BlockSpec((1, 6, d), lambda b, i: (bidx(b), 0, 0)),
                  pl.BlockSpec((1, d), lambda b, i: (0, 0)),
                  pl.BlockSpec((gw, 2 * gw), lambda b, i: (0, 0))],
        out_specs=pl.BlockSpec((1, 2, tm, d), lambda b, i: (b, 0, i, 0)),
        out_shape=jax.ShapeDtypeStruct((bsz, 2, t_len, d), BF16),
        compiler_params=_params("parallel", "parallel"),
        name="dft_channel",
    )(x, mod, ng.reshape(1, d), cs)
    y = y.reshape(bsz, 2 * t_len, d)
    assert t_len % DFT_SPLIT == 0
    kk = np.arange(t_len)[:, None]
    ang_hi = 2.0 * np.pi * ((kk * DFT_SPLIT * np.arange(t_len // DFT_SPLIT)[None, :]) % t_len) / t_len
    ang_lo = 2.0 * np.pi * ((kk * np.arange(DFT_SPLIT)[None, :]) % t_len) / t_len
    scale = t_len ** -0.5
    c_hi, s_hi = (jnp.asarray((f(ang_hi) * scale).astype(np.float32))[:, :, None] for f in (np.cos, np.sin))
    c_lo, s_lo = (jnp.asarray(f(ang_lo).astype(np.float32))[:, None, :] for f in (np.cos, np.sin))
    ct = (c_hi * c_lo - s_hi * s_lo).reshape(t_len, t_len)
    st = (s_hi * c_lo + c_hi * s_lo).reshape(t_len, t_len)
    wt = jnp.concatenate([ct, -st], axis=1).astype(BF16)
    tmk = _tile(t_len, 1024)
    tk = _tile(2 * t_len, 2048)
    return pl.pallas_call(
        _dft_time_kernel,
        grid=(bsz, t_len // tmk, 2 * t_len // tk),
        in_specs=[pl.BlockSpec((tmk, tk), lambda b, i, k: (i, k)),
                  pl.BlockSpec((1, tk, d), lambda b, i, k: (b, k, 0)),
                  pl.BlockSpec((1, tmk, d), lambda b, i, k: (b, i, 0)),
                  pl.BlockSpec((1, 6, d), lambda b, i, k: (bidx(b), 0, 0)),
                  pl.BlockSpec((d, d), lambda b, i, k: (0, 0)),
                  pl.BlockSpec((1, d), lambda b, i, k: (0, 0))],
        out_specs=pl.BlockSpec((1, tmk, d), lambda b, i, k: (b, i, 0)),
        out_shape=jax.ShapeDtypeStruct((bsz, t_len, d), F32),
        scratch_shapes=[pltpu.VMEM((tmk, d), F32)],
        compiler_params=_params("parallel", "parallel", "arbitrary"),
        name="dft_time",
    )(wt, y, x, mod, fn_w.astype(BF16), fn_b.reshape(1, d).astype(F32))


def _final_norm_kernel(x_ref, g_ref, o_ref):
    x = x_ref[0]
    o_ref[0] = x * lax.rsqrt(jnp.mean(x * x, axis=-1, keepdims=True) + EPS) * g_ref[...]


def _final_norm(x, g):
    bsz, t_len, d = x.shape
    tm = _tile(t_len, 1024)
    return pl.pallas_call(
        _final_norm_kernel,
        grid=(bsz, t_len // tm),
        in_specs=[pl.BlockSpec((1, tm, d), lambda b, i: (b, i, 0)), pl.BlockSpec((1, d), lambda b, i: (0, 0))],
        out_specs=pl.BlockSpec((1, tm, d), lambda b, i: (b, i, 0)),
        out_shape=jax.ShapeDtypeStruct((bsz, t_len, d), F32),
        compiler_params=_params("parallel", "parallel"),
        name="final_norm",
    )(x, g.reshape(1, d))


def _trunk(x, mod, cache, use_rope, per_batch_mod, p):
    bsz, t_len, d = x.shape
    depth = mod.shape[0]
    flat = (lambda a: a) if per_batch_mod else (lambda a: a.reshape(1, bsz * t_len, a.shape[-1]))
    unflat = (lambda a: a) if per_batch_mod else (lambda a: a.reshape(bsz, t_len, a.shape[-1]))
    states = []
    for i in range(depth):
        j = i // 2
        m_i = mod[i]
        if i % 2 == 0:
            q, kt, v, o, r = _mlstm_inproj(x, m_i, p["norm_g"][i, 0], p["ml_w_in"][j], p["ml_b_gate"][j],
                                           use_rope, per_batch_mod)
            outs = []
            for direction in range(2):
                if cache is None:
                    cn0 = jnp.zeros((bsz, N_HEADS // 2, 2 * HEAD_DK, 2 * HEAD_DV), F32)
                    m0 = jnp.zeros((bsz, N_HEADS, LANES), F32)
                else:
                    cn0, m0 = _pack_state(cache[0][:, j, direction], cache[1][:, j, direction],
                                          cache[2][:, j, direction])
                outs.append(_mlstm_scan(q, kt, v, r, cn0, m0, reverse=bool(direction)))
            states.append([_unpack_state(cn, m) for (_, cn, m) in outs])
            x = _mlstm_out(outs[0][0], outs[1][0], o, x, m_i, p["ml_head_g"][j], p["ml_w_out"][j], per_batch_mod)
            x = unflat(_ffn(flat(x), m_i, p["norm_g"][i, 1], p["ffn_w1"][j], p["ffn_w3"][j], p["ffn_w2"][j],
                            per_batch_mod))
        else:
            x = _fourier_mixer(x, m_i, p["norm_g"][i, 0], p["fn_w"][j], p["fn_b"][j], per_batch_mod)
            x = unflat(_moe(flat(x), m_i, p["norm_g"][i, 1], p["moe_router"][j], p["moe_w1"][j], p["moe_w3"][j],
                            p["moe_w2"][j], per_batch_mod))
    y = unflat(_final_norm(flat(x), p["final_g"]))
    return y, states


def kernel(x_prompt, x_sample, state_C, state_n, state_m, c, c_ctx, w_mod, b_mod, norm_g, final_g,
           ml_w_in, ml_b_gate, ml_head_g, ml_w_out, fn_w, fn_b, ffn_w1, ffn_w3, ffn_w2,
           moe_router, moe_w1, moe_w3, moe_w2):
    p = dict(norm_g=norm_g, final_g=final_g, ml_w_in=ml_w_in, ml_b_gate=ml_b_gate, ml_head_g=ml_head_g,
             ml_w_out=ml_w_out, fn_w=fn_w, fn_b=fn_b, ffn_w1=ffn_w1, ffn_w3=ffn_w3, ffn_w2=ffn_w2,
             moe_router=moe_router, moe_w1=moe_w1, moe_w3=moe_w3, moe_w2=moe_w2)
    depth, d = w_mod.shape[0], w_mod.shape[1]
    n_dec = c.shape[0]
    rows = ((n_dec + 1 + 7) // 8) * 8
    cond = jnp.zeros((rows, d), F32).at[:n_dec].set(c).at[n_dec].set(c_ctx)
    mod = _mod_table(cond, w_mod, b_mod).reshape(depth, rows, 6, d)
    y_prompt, st = _trunk(x_prompt, mod[:, n_dec:n_dec + 1], None, False, False, p)
    y_sample, _ = _trunk(x_sample, mod[:, :n_dec], (state_C, state_n, state_m), True, True, p)
    new_c = jnp.stack([jnp.stack([s[0][0], s[1][0]], axis=1) for s in st], axis=1)
    new_n = jnp.stack([jnp.stack([s[0][1], s[1][1]], axis=1) for s in st], axis=1)
    new_m = jnp.stack([jnp.stack([s[0][2], s[1][2]], axis=1) for s in st], axis=1)
    return (y_prompt, y_sample, new_c.astype(x_prompt.dtype), new_n.astype(x_prompt.dtype),
            new_m.astype(x_prompt.dtype))
```

```python
import functools

import numpy as np
import jax
import jax.numpy as jnp
from jax import lax
from jax.experimental import pallas as pl
from jax.experimental.pallas import tpu as pltpu

F32 = jnp.float32
BF16 = jnp.bfloat16
HIGHEST = lax.Precision.HIGHEST

EPS = 1e-6
N_HEADS = 8
HEAD_DK = 64
HEAD_DV = 128
QK_W = N_HEADS * HEAD_DK
V_W = N_HEADS * HEAD_DV
GRID_W = 64
ROPE_BASE = 10000.0
N_GROUPS = 4
DFT_SPLIT = 64
N_GATES = 4 * N_HEADS
LANES = 128
SCAN_CHUNK = 128
VMEM_LIMIT = 56 * 1024 * 1024
NEG_INF = float("-inf")


def _params(*sem):
    return pltpu.CompilerParams(dimension_semantics=sem, vmem_limit_bytes=VMEM_LIMIT)


def _tile(n, pref):
    t = min(n, pref)
    assert n % t == 0, (n, pref)
    return t


def _norm_mod(x, gain, scale, shift):
    ms = jnp.mean(x * x, axis=-1, keepdims=True)
    return x * lax.rsqrt(ms + EPS) * gain * (1.0 + scale) + shift


def _dot(a, b):
    return jnp.dot(a, b, preferred_element_type=F32)


def _dot_nt(a, b):
    return lax.dot_general(a, b, (((1,), (1,)), ((), ())), preferred_element_type=F32)


def _log_sigmoid(x):
    return jnp.minimum(x, 0.0) - jnp.log1p(jnp.exp(-jnp.abs(x)))


def _mod_kernel(c_ref, w_ref, b_ref, o_ref):
    c = c_ref[...]
    s = c * jax.nn.sigmoid(c)
    o_ref[0] = jnp.dot(s, w_ref[0], preferred_element_type=F32, precision=HIGHEST) + b_ref[0]


def _mod_table(cond, w_mod, b_mod):
    depth, d, n = w_mod.shape
    rows = cond.shape[0]
    tn = _tile(n, 1536)
    return pl.pallas_call(
        _mod_kernel,
        grid=(depth, n // tn),
        in_specs=[pl.BlockSpec((rows, d), lambda l, j: (0, 0)),
                  pl.BlockSpec((1, d, tn), lambda l, j: (l, 0, j)),
                  pl.BlockSpec((1, 1, tn), lambda l, j: (l, 0, j))],
        out_specs=pl.BlockSpec((1, rows, tn), lambda l, j: (l, 0, j)),
        out_shape=jax.ShapeDtypeStruct((depth, rows, n), F32),
        compiler_params=_params("parallel", "parallel"),
        name="adaln_table",
    )(cond, w_mod, b_mod.reshape(depth, 1, n))


def _rope_tables(t_len):
    pos = np.arange(t_len)
    row = (pos // GRID_W).astype(np.float32)
    col = (pos % GRID_W).astype(np.float32)
    nf = HEAD_DK // 4
    inv = (np.float32(ROPE_BASE) ** (-np.arange(nf, dtype=np.float32) / nf)).astype(np.float32)
    d = np.arange(HEAD_DK)
    p = np.where(d[None, :] < HEAD_DK // 2, row[:, None], col[:, None]).astype(np.float32)
    ang = p * inv[d % nf][None, :]
    sign = np.where((d % (2 * nf)) < nf, -1.0, 1.0).astype(np.float32)
    return np.cos(ang).astype(np.float32), (np.sin(ang) * sign[None, :]).astype(np.float32)


def _inproj_kernel(*refs, use_rope):
    if use_rope:
        (x_ref, mod_ref, ng_ref, wq_ref, wkt_ref, wv_ref, wo_ref, wgt_ref, bg_ref, trif_ref, trib_ref,
         cq_ref, sq_ref, ck_ref, sk_ref, q_ref, kt_ref, v_ref, o_ref, r_ref) = refs
    else:
        (x_ref, mod_ref, ng_ref, wq_ref, wkt_ref, wv_ref, wo_ref, wgt_ref, bg_ref, trif_ref, trib_ref,
         q_ref, kt_ref, v_ref, o_ref, r_ref) = refs
    hn = _norm_mod(x_ref[0], ng_ref[...], mod_ref[0, 1:2, :], mod_ref[0, 0:1, :])
    hb = hn.astype(BF16)
    q = _dot(hb, wq_ref[...]) * (HEAD_DK ** -0.5)
    kt = _dot_nt(wkt_ref[...], hb)
    if use_rope:
        nf = HEAD_DK // 4
        lane = lax.broadcasted_iota(jnp.int32, (q.shape[0], LANES), 1)
        first_q = (lane % (2 * nf)) < nf
        sub = lax.broadcasted_iota(jnp.int32, (LANES, kt.shape[1]), 0)
        first_k = (sub % (2 * nf)) < nf
        cq, sq, ck, sk = cq_ref[...], sq_ref[...], ck_ref[...], sk_ref[...]
        for s in range(QK_W // LANES):
            qs = q[:, s * LANES:(s + 1) * LANES]
            sw = jnp.where(first_q, pltpu.roll(qs, LANES - nf, 1), pltpu.roll(qs, nf, 1))
            q_ref[0, :, s * LANES:(s + 1) * LANES] = (qs * cq + sw * sq).astype(BF16)
            ks = kt[s * LANES:(s + 1) * LANES, :]
            sw = jnp.where(first_k, pltpu.roll(ks, LANES - nf, 0), pltpu.roll(ks, nf, 0))
            kt_ref[0, s * LANES:(s + 1) * LANES, :] = (ks * ck + sw * sk).astype(BF16)
    else:
        q_ref[0] = q.astype(BF16)
        kt_ref[0] = kt.astype(BF16)
    v_ref[0] = _dot(hb, wv_ref[...]).astype(BF16)
    o_ref[0] = _dot(hb, wo_ref[...])
    gt = _dot_nt(wgt_ref[...], hb) + bg_ref[...]
    h = N_HEADS
    i_f, f_f = gt[0:h], _log_sigmoid(gt[h:2 * h])
    i_b, f_b = gt[2 * h:3 * h], _log_sigmoid(gt[3 * h:4 * h])
    b_f = jnp.dot(f_f, trif_ref[...], preferred_element_type=F32, precision=HIGHEST)
    b_b = jnp.dot(f_b, trib_ref[...], preferred_element_type=F32, precision=HIGHEST)
    r_ref[0, 0:h, :] = i_f - b_f
    r_ref[0, h:2 * h, :] = b_f
    r_ref[0, 2 * h:3 * h, :] = i_b - b_b
    r_ref[0, 3 * h:4 * h, :] = b_b


def _mlstm_inproj(x, mod, ng, w_in, b_gate, use_rope, per_batch_mod):
    bsz, t_len, d = x.shape
    tm = _tile(t_len, 512)
    wq = w_in[:, :QK_W].astype(BF16)
    wkt = w_in[:, QK_W:2 * QK_W].T.astype(BF16)
    wv = w_in[:, 2 * QK_W:2 * QK_W + V_W].astype(BF16)
    wo = w_in[:, 2 * QK_W + V_W:2 * QK_W + 2 * V_W].astype(BF16)
    wgt = w_in[:, 2 * QK_W + 2 * V_W:].T.astype(BF16)
    bg = b_gate.reshape(N_GATES, 1).astype(F32)
    pos = np.arange(tm)
    same = (pos[:, None] // SCAN_CHUNK) == (pos[None, :] // SCAN_CHUNK)
    trif = jnp.asarray((same & (pos[:, None] <= pos[None, :])).astype(np.float32))
    trib = jnp.asarray((same & (pos[:, None] >= pos[None, :])).astype(np.float32))
    bidx = (lambda b: b) if per_batch_mod else (lambda b: 0)
    const = lambda shp: pl.BlockSpec(shp, lambda b, i: (0,) * len(shp))
    in_specs = [pl.BlockSpec((1, tm, d), lambda b, i: (b, i, 0)),
                pl.BlockSpec((1, 6, d), lambda b, i: (bidx(b), 0, 0)),
                const((1, d)), const((d, QK_W)), const((QK_W, d)), const((d, V_W)), const((d, V_W)),
                const((N_GATES, d)), const((N_GATES, 1)), const((tm, tm)), const((tm, tm))]
    args = [x, mod, ng.reshape(1, d), wq, wkt, wv, wo, wgt, bg, trif, trib]
    if use_rope:
        cos, sin = _rope_tables(t_len)
        rep = LANES // HEAD_DK
        args += [jnp.asarray(np.tile(cos, (1, rep))), jnp.asarray(np.tile(sin, (1, rep))),
                 jnp.asarray(np.tile(cos.T, (rep, 1))), jnp.asarray(np.tile(sin.T, (rep, 1)))]
        in_specs += [pl.BlockSpec((tm, LANES), lambda b, i: (i, 0)), pl.BlockSpec((tm, LANES), lambda b, i: (i, 0)),
                     pl.BlockSpec((LANES, tm), lambda b, i: (0, i)), pl.BlockSpec((LANES, tm), lambda b, i: (0, i))]
    out_shape = (jax.ShapeDtypeStruct((bsz, t_len, QK_W), BF16),
                 jax.ShapeDtypeStruct((bsz, QK_W, t_len), BF16),
                 jax.ShapeDtypeStruct((bsz, t_len, V_W), BF16),
                 jax.ShapeDtypeStruct((bsz, t_len, V_W), F32),
                 jax.ShapeDtypeStruct((bsz, N_GATES, t_len), F32))
    out_specs = (pl.BlockSpec((1, tm, QK_W), lambda b, i: (b, i, 0)),
                 pl.BlockSpec((1, QK_W, tm), lambda b, i: (b, 0, i)),
                 pl.BlockSpec((1, tm, V_W), lambda b, i: (b, i, 0)),
                 pl.BlockSpec((1, tm, V_W), lambda b, i: (b, i, 0)),
                 pl.BlockSpec((1, N_GATES, tm), lambda b, i: (b, 0, i)))
    return pl.pallas_call(
        functools.partial(_inproj_kernel, use_rope=use_rope),
        grid=(bsz, t_len // tm), in_specs=in_specs, out_specs=out_specs, out_shape=out_shape,
        compiler_params=_params("parallel", "parallel"),
        name="mlstm_inproj",
    )(*args)


def _scan_kernel(q_ref, kt_ref, v_ref, r_ref, cn0_ref, m0_ref, h_ref, cn_out_ref, m_out_ref,
                 cn_scr, m_scr, *, reverse, n_chunks):
    L = SCAN_CHUNK
    j = pl.program_id(1)

    @pl.when(j == 0)
    def _():
        cn_scr[...] = cn0_ref[0]
        m_scr[...] = m0_ref[0]

    row_i = lax.broadcasted_iota(jnp.int32, (L, L), 0)
    col_i = lax.broadcasted_iota(jnp.int32, (L, L), 1)
    visible = (col_i >= row_i) if reverse else (col_i <= row_i)
    diag = col_i == row_i
    upper_lanes = lax.broadcasted_iota(jnp.int32, (L, LANES), 1) >= HEAD_DK
    ones = jnp.ones((L, LANES), BF16)
    no_rows = jnp.zeros((HEAD_DK, 2 * HEAD_DV), BF16)
    end_lane = lax.broadcasted_iota(jnp.int32, (1, L), 1) == (0 if reverse else L - 1)
    order = range(n_chunks - 1, -1, -1) if reverse else range(n_chunks)
    head_rows = [slice((h % 2) * HEAD_DK, (h % 2 + 1) * HEAD_DK) for h in range(N_HEADS)]
    states = [(cn_scr[h // 2, head_rows[h], :], m_scr[h:h + 1, 0:1]) for h in range(N_HEADS)]
    new_states = []
    for h in range(N_HEADS):
        p, e = h // 2, h % 2
        cn, m = states[h]
        for c in order:
            r0 = c * L
            q_pair = q_ref[0, r0:r0 + L, p * LANES:(p + 1) * LANES]
            q_m = jnp.where(upper_lanes if e else jnp.logical_not(upper_lanes), q_pair, jnp.zeros_like(q_pair))
            kt_pair = kt_ref[0, p * LANES:(p + 1) * LANES, r0:r0 + L]
            kt_h = kt_ref[0, h * HEAD_DK:(h + 1) * HEAD_DK, r0:r0 + L]
            a_row = r_ref[0, h:h + 1, r0:r0 + L]
            b_row = r_ref[0, N_HEADS + h:N_HEADS + h + 1, r0:r0 + L]
            v_aug = jnp.concatenate([v_ref[0, r0:r0 + L, h * HEAD_DV:(h + 1) * HEAD_DV], ones], axis=1)
            am = jnp.where(visible, a_row, NEG_INF)
            row_max = jnp.max(am, axis=1, keepdims=True)
            a_max = jnp.max(a_row, axis=1, keepdims=True)
            s = (_dot(q_m, kt_pair) * jnp.exp(am - row_max)).astype(BF16)
            intra = _dot(s, v_aug)
            kw = (kt_h.astype(F32) * jnp.exp(a_row - a_max)).astype(BF16)
            upd = _dot(kw, v_aug)
            b_col = jnp.sum(jnp.where(diag, b_row, 0.0), axis=1, keepdims=True)
            b_end = jnp.sum(jnp.where(end_lane, b_row, 0.0), axis=1, keepdims=True)
            u = jnp.maximum(m, row_max)
            prev_scale = jnp.exp(m - u)
            loc_scale = jnp.exp(row_max - u)
            guard = jnp.exp(-(b_col + u))
            cn_b = cn.astype(BF16)
            cn_pair = jnp.concatenate([no_rows, cn_b] if e else [cn_b, no_rows], axis=0)
            inter = _dot(q_m, cn_pair)
            num = prev_scale * inter[:, :HEAD_DV] + loc_scale * intra[:, :HEAD_DV]
            den = prev_scale * inter[:, HEAD_DV:] + loc_scale * intra[:, HEAD_DV:]
            h_ref[0, r0:r0 + L, h * HEAD_DV:(h + 1) * HEAD_DV] = num / jnp.maximum(jnp.abs(den), guard)
            u_end = jnp.maximum(m, a_max)
            cn = jnp.exp(m - u_end) * cn + jnp.exp(a_max - u_end) * upd
            m = b_end + u_end
        new_states.append((cn, m))
    for h in range(N_HEADS):
        cn_scr[h // 2, head_rows[h], :] = new_states[h][0]
        m_scr[h:h + 1, :] = jnp.broadcast_to(new_states[h][1], (1, LANES))

    @pl.when(j == pl.num_programs(1) - 1)
    def _():
        cn_out_ref[0] = cn_scr[...]
        m_out_ref[0] = m_scr[...]


def _mlstm_scan(q, kt, v, r, cn0, m0, reverse):
    bsz, t_len, _ = q.shape
    tb = _tile(t_len, 4 * SCAN_CHUNK)
    nblk = t_len // tb
    blk = (lambda j: nblk - 1 - j) if reverse else (lambda j: j)
    d = 1 if reverse else 0
    half = N_HEADS // 2
    return pl.pallas_call(
        functools.partial(_scan_kernel, reverse=reverse, n_chunks=tb // SCAN_CHUNK),
        grid=(bsz, nblk),
        in_specs=[pl.BlockSpec((1, tb, QK_W), lambda b, j: (b, blk(j), 0)),
                  pl.BlockSpec((1, QK_W, tb), lambda b, j: (b, 0, blk(j))),
                  pl.BlockSpec((1, tb, V_W), lambda b, j: (b, blk(j), 0)),
                  pl.BlockSpec((1, 2 * N_HEADS, tb), lambda b, j: (b, d, blk(j))),
                  pl.BlockSpec((1, half, 2 * HEAD_DK, 2 * HEAD_DV), lambda b, j: (b, 0, 0, 0)),
                  pl.BlockSpec((1, N_HEADS, LANES), lambda b, j: (b, 0, 0))],
        out_specs=(pl.BlockSpec((1, tb, V_W), lambda b, j: (b, blk(j), 0)),
                   pl.BlockSpec((1, half, 2 * HEAD_DK, 2 * HEAD_DV), lambda b, j: (b, 0, 0, 0)),
                   pl.BlockSpec((1, N_HEADS, LANES), lambda b, j: (b, 0, 0))),
        out_shape=(jax.ShapeDtypeStruct((bsz, t_len, V_W), F32),
                   jax.ShapeDtypeStruct((bsz, half, 2 * HEAD_DK, 2 * HEAD_DV), F32),
                   jax.ShapeDtypeStruct((bsz, N_HEADS, LANES), F32)),
        scratch_shapes=[pltpu.VMEM((half, 2 * HEAD_DK, 2 * HEAD_DV), F32),
                        pltpu.VMEM((N_HEADS, LANES), F32)],
        compiler_params=_params("parallel", "arbitrary"),
        name="mlstm_scan_bwd" if reverse else "mlstm_scan_fwd",
    )(q, kt, v, r, cn0, m0)


def _pack_state(c, n, m):
    bsz = c.shape[0]
    cn = jnp.concatenate([c, jnp.broadcast_to(n[..., None], n.shape + (HEAD_DV,))], axis=-1)
    cn = cn.reshape(bsz, N_HEADS // 2, 2 * HEAD_DK, 2 * HEAD_DV)
    return cn.astype(F32), jnp.broadcast_to(m[..., None], (bsz, N_HEADS, LANES)).astype(F32)


def _unpack_state(cn, m):
    bsz = cn.shape[0]
    cn = cn.reshape(bsz, N_HEADS, HEAD_DK, 2 * HEAD_DV)
    return cn[..., :HEAD_DV], cn[..., HEAD_DV], m[..., 0]


def _mlstm_out_kernel(hf_ref, hb_ref, o_ref, x_ref, mod_ref, hg_ref, w_ref, out_ref):
    hs = hf_ref[0] + hb_ref[0]
    parts = []
    for h in range(N_HEADS):
        z = hs[:, h * HEAD_DV:(h + 1) * HEAD_DV]
        parts.append(z * lax.rsqrt(jnp.mean(z * z, axis=-1, keepdims=True) + EPS))
    hn = jnp.concatenate(parts, axis=1) * hg_ref[...] * jax.nn.sigmoid(o_ref[0])
    y = _dot(hn.astype(BF16), w_ref[...])
    out_ref[0] = x_ref[0] + mod_ref[0, 2:3, :] * y


def _mlstm_out(hf, hb, o, x, mod, head_g, w_out, per_batch_mod):
    bsz, t_len, d = x.shape
    tm = _tile(t_len, 512)
    bidx = (lambda b: b) if per_batch_mod else (lambda b: 0)
    tok = lambda w: pl.BlockSpec((1, tm, w), lambda b, i: (b, i, 0))
    return pl.pallas_call(
        _mlstm_out_kernel,
        grid=(bsz, t_len // tm),
        in_specs=[tok(V_W), tok(V_W), tok(V_W), tok(d),
                  pl.BlockSpec((1, 6, d), lambda b, i: (bidx(b), 0, 0)),
                  pl.BlockSpec((1, V_W), lambda b, i: (0, 0)),
                  pl.BlockSpec((V_W, d), lambda b, i: (0, 0))],
        out_specs=tok(d),
        out_shape=jax.ShapeDtypeStruct((bsz, t_len, d), F32),
        compiler_params=_params("parallel", "parallel"),
        name="mlstm_out",
    )(hf, hb, o, x, mod, head_g.reshape(1, V_W).astype(F32), w_out.astype(BF16))


def _ffn_kernel(x_ref, mod_ref, ng_ref, w1_ref, w3_ref, w2_ref, out_ref, hn_scr, acc_scr):
    f = pl.program_id(2)

    @pl.when(f == 0)
    def _():
        hn = _norm_mod(x_ref[0], ng_ref[...], mod_ref[0, 4:5, :], mod_ref[0, 3:4, :])
        hn_scr[...] = hn.astype(BF16)
        acc_scr[...] = jnp.zeros_like(acc_scr)

    hb = hn_scr[...]
    h1 = _dot(hb, w1_ref[...])
    h3 = _dot(hb, w3_ref[...])
    act = (h1 * jax.nn.sigmoid(h1) * h3).astype(BF16)
    acc_scr[...] += _dot(act, w2_ref[...])

    @pl.when(f == pl.num_programs(2) - 1)
    def _():
        out_ref[0] = x_ref[0] + mod_ref[0, 5:6, :] * acc_scr[...]


def _ffn(x, mod, ng, w1, w3, w2, per_batch_mod):
    bsz, t_len, d = x.shape
    d_ff = w1.shape[1]
    tm = _tile(t_len, 1024)
    tf = _tile(d_ff, 512)
    bidx = (lambda b: b) if per_batch_mod else (lambda b: 0)
    return pl.pallas_call(
        _ffn_kernel,
        grid=(bsz, t_len // tm, d_ff // tf),
        in_specs=[pl.BlockSpec((1, tm, d), lambda b, i, f: (b, i, 0)),
                  pl.BlockSpec((1, 6, d), lambda b, i, f: (bidx(b), 0, 0)),
                  pl.BlockSpec((1, d), lambda b, i, f: (0, 0)),
                  pl.BlockSpec((d, tf), lambda b, i, f: (0, f)),
                  pl.BlockSpec((d, tf), lambda b, i, f: (0, f)),
                  pl.BlockSpec((tf, d), lambda b, i, f: (f, 0))],
        out_specs=pl.BlockSpec((1, tm, d), lambda b, i, f: (b, i, 0)),
        out_shape=jax.ShapeDtypeStruct((bsz, t_len, d), F32),
        scratch_shapes=[pltpu.VMEM((tm, d), BF16), pltpu.VMEM((tm, d), F32)],
        compiler_params=_params("parallel", "parallel", "arbitrary"),
        name="dense_swiglu",
    )(x, mod, ng.reshape(1, d), w1.astype(BF16), w3.astype(BF16), w2.astype(BF16))


ROUTE_TILE = 512
GROUP_TILE = 512
INFO_E1, INFO_E2, INFO_P1, INFO_P2, INFO_R1, INFO_R2 = range(6)
INFO_ROWS = 8


def _lane_pick(rec, lane, k):
    return jnp.sum(jnp.where(lane == k, rec, 0.0), axis=-1, keepdims=True)


def _router_kernel(x_ref, mod_ref, ng_ref, wr_ref, tri_ref, hn_ref, info_ref, infot_ref, cnt_ref, carry, *, n_exp):
    @pl.when((pl.program_id(0) == 0) & (pl.program_id(1) == 0))
    def _():
        carry[...] = jnp.zeros_like(carry)

    hn = _norm_mod(x_ref[0], ng_ref[...], mod_ref[0, 4:5, :], mod_ref[0, 3:4, :])
    hn_ref[0] = hn
    logits = jnp.dot(hn, wr_ref[...], preferred_element_type=F32, precision=HIGHEST)
    lane = lax.broadcasted_iota(jnp.int32, logits.shape, 1).astype(F32)
    logits = jnp.where(lane < n_exp, logits, NEG_INF)
    v1 = jnp.max(logits, axis=-1, keepdims=True)
    i1 = jnp.min(jnp.where(logits == v1, lane, float(LANES)), axis=-1, keepdims=True)
    rest = jnp.where(lane == i1, NEG_INF, logits)
    v2 = jnp.max(rest, axis=-1, keepdims=True)
    i2 = jnp.min(jnp.where(rest == v2, lane, float(LANES)), axis=-1, keepdims=True)
    e2 = jnp.exp(v2 - v1)
    den = 1.0 + e2
    chosen = jnp.where((lane == i1) | (lane == i2), 1.0, 0.0)
    before = _dot(tri_ref[...], chosen.astype(BF16)) + carry[...]
    r1 = jnp.sum(jnp.where(lane == i1, before, 0.0), axis=-1, keepdims=True)
    r2 = jnp.sum(jnp.where(lane == i2, before, 0.0), axis=-1, keepdims=True)
    total = carry[...] + jnp.sum(chosen, axis=0, keepdims=True)
    carry[...] = total
    cnt_ref[...] = total
    rec = jnp.zeros_like(logits)
    for k, val in ((INFO_E1, i1), (INFO_E2, i2), (INFO_P1, 1.0 / den), (INFO_P2, e2 / den),
                   (INFO_R1, r1), (INFO_R2, r2)):
        rec = jnp.where(lane == k, val, rec)
    info_ref[0] = rec
    infot_ref[...] = rec.T[0:INFO_ROWS, :]


def _router(x, mod, ng, w_router, per_batch_mod):
    bsz, t_len, d = x.shape
    n_exp = w_router.shape[1]
    tm = _tile(t_len, ROUTE_TILE)
    wr = jnp.zeros((d, LANES), F32).at[:, :n_exp].set(w_router)
    pos = np.arange(tm)
    tri = jnp.asarray((pos[None, :] < pos[:, None]).astype(np.float32)).astype(BF16)
    bidx = (lambda b: b) if per_batch_mod else (lambda b: 0)
    return pl.pallas_call(
        functools.partial(_router_kernel, n_exp=n_exp),
        grid=(bsz, t_len // tm),
        in_specs=[pl.BlockSpec((1, tm, d), lambda b, i: (b, i, 0)),
                  pl.BlockSpec((1, 6, d), lambda b, i: (bidx(b), 0, 0)),
                  pl.BlockSpec((1, d), lambda b, i: (0, 0)),
                  pl.BlockSpec((d, LANES), lambda b, i: (0, 0)),
                  pl.BlockSpec((tm, tm), lambda b, i: (0, 0))],
        out_specs=(pl.BlockSpec((1, tm, d), lambda b, i: (b, i, 0)),
                   pl.BlockSpec((1, tm, LANES), lambda b, i: (b, i, 0)),
                   pl.BlockSpec((INFO_ROWS, tm), lambda b, i: (0, b * (t_len // tm) + i)),
                   pl.BlockSpec((1, LANES), lambda b, i: (0, 0))),
        out_shape=(jax.ShapeDtypeStruct((bsz, t_len, d), F32),
                   jax.ShapeDtypeStruct((bsz, t_len, LANES), F32),
                   jax.ShapeDtypeStruct((INFO_ROWS, bsz * t_len), F32),
                   jax.ShapeDtypeStruct((1, LANES), F32)),
        scratch_shapes=[pltpu.VMEM((1, LANES), F32)],
        compiler_params=_params("arbitrary", "arbitrary"),
        name="moe_router",
    )(x, mod, ng.reshape(1, d), wr, tri)


def _dispatch_kernel(zmask_ref, pos_ref, hn_ref, xs_ref, zeros, sem, *, tb, n_row_tiles):
    def row_copy(r, slot):
        return pltpu.make_async_copy(hn_ref.at[pl.ds(r, 1)], xs_ref.at[pl.ds(slot, 1)], sem)

    def tile_fill(r):
        row0 = pl.multiple_of(r * GROUP_TILE, GROUP_TILE)
        return pltpu.make_async_copy(zeros, xs_ref.at[pl.ds(row0, GROUP_TILE)], sem)

    @pl.when(pl.program_id(0) == 0)
    def _():
        zeros[...] = jnp.zeros_like(zeros)

        def fill(r, c):
            @pl.when(zmask_ref[r] != 0)
            def _():
                tile_fill(r).start()
            return c

        def fill_done(r, c):
            @pl.when(zmask_ref[r] != 0)
            def _():
                tile_fill(r).wait()
            return c

        lax.fori_loop(0, n_row_tiles, fill, 0)
        lax.fori_loop(0, n_row_tiles, fill_done, 0)

    def issue(r, c):
        row_copy(r, pos_ref[0, 0, r]).start()
        row_copy(r, pos_ref[0, 1, r]).start()
        return c

    def drain(r, c):
        row_copy(0, 0).wait()
        row_copy(0, 0).wait()
        return c

    lax.fori_loop(0, tb, issue, 0, unroll=8)
    lax.fori_loop(0, tb, drain, 0, unroll=8)


def _dispatch(zmask, pos, hn, s_max):
    n, d = hn.shape
    n_tiles, _, tb = pos.shape
    return pl.pallas_call(
        functools.partial(_dispatch_kernel, tb=tb, n_row_tiles=zmask.shape[0]),
        grid_spec=pltpu.PrefetchScalarGridSpec(
            num_scalar_prefetch=1,
            grid=(n_tiles,),
            in_specs=[pl.BlockSpec((1, 2, tb), lambda i, zm: (i, 0, 0), memory_space=pltpu.SMEM),
                      pl.BlockSpec((tb, d), lambda i, zm: (i, 0))],
            out_specs=pl.BlockSpec(memory_space=pl.ANY),
            scratch_shapes=[pltpu.VMEM((GROUP_TILE, d), F32), pltpu.SemaphoreType.DMA(())]),
        out_shape=jax.ShapeDtypeStruct((s_max, d), F32),
        compiler_params=_params("arbitrary"),
        name="moe_dispatch",
    )(zmask, pos, hn)


def _group_ffn_kernel(te_ref, tx_ref, tv_ref, x_ref, w1_ref, w3_ref, w2_ref, o_ref, xb_scr, acc_scr):
    r, f = pl.program_id(0), pl.program_id(1)
    last = pl.num_programs(1) - 1

    @pl.when(tv_ref[r] != 0)
    def _():
        @pl.when(f == 0)
        def _():
            xb_scr[...] = x_ref[...].astype(BF16)

        xb = xb_scr[...]
        h1 = _dot(xb, w1_ref[0])
        h3 = _dot(xb, w3_ref[0])
        act = (h1 * jax.nn.sigmoid(h1) * h3).astype(BF16)
        y = _dot(act, w2_ref[0])

        @pl.when(f == 0)
        def _():
            acc_scr[...] = y

        @pl.when(f != 0)
        def _():
            acc_scr[...] += y

        @pl.when(f == last)
        def _():
            o_ref[...] = acc_scr[...]

    @pl.when((tv_ref[r] == 0) & (f == last))
    def _():
        o_ref[...] = jnp.zeros_like(o_ref)


def _group_ffn(te, tx, tv, xs, w1, w3, w2):
    s_max, d = xs.shape
    d_ff = w1.shape[2]
    tf = _tile(d_ff, 1792)
    nf = d_ff // tf
    fidx = lambda r, f, tv: f * tv[r] + (nf - 1) * (1 - tv[r])
    return pl.pallas_call(
        _group_ffn_kernel,
        grid_spec=pltpu.PrefetchScalarGridSpec(
            num_scalar_prefetch=3,
            grid=(s_max // GROUP_TILE, nf),
            in_specs=[pl.BlockSpec((GROUP_TILE, d), lambda r, f, te, tx, tv: (tx[r], 0)),
                      pl.BlockSpec((1, d, tf), lambda r, f, te, tx, tv: (te[r], 0, fidx(r, f, tv))),
                      pl.BlockSpec((1, d, tf), lambda r, f, te, tx, tv: (te[r], 0, fidx(r, f, tv))),
                      pl.BlockSpec((1, tf, d), lambda r, f, te, tx, tv: (te[r], fidx(r, f, tv), 0))],
            out_specs=pl.BlockSpec((GROUP_TILE, d), lambda r, f, te, tx, tv: (r, 0)),
            scratch_shapes=[pltpu.VMEM((GROUP_TILE, d), BF16), pltpu.VMEM((GROUP_TILE, d), F32)]),
        out_shape=jax.ShapeDtypeStruct((s_max, d), F32),
        compiler_params=_params("parallel", "arbitrary"),
        name="moe_group_swiglu",
    )(te, tx, tv, xs, w1, w3, w2)


def _combine_kernel(pos_ref, x_ref, mod_ref, info_ref, ys_ref, out_ref, buf, sem, *, tb):
    def row_copy(slot, k, r):
        return pltpu.make_async_copy(ys_ref.at[pl.ds(slot, 1)], buf.at[k, pl.ds(r, 1)], sem)

    def issue(r, c):
        row_copy(pos_ref[0, 0, r], 0, r).start()
        row_copy(pos_ref[0, 1, r], 1, r).start()
        return c

    def drain(r, c):
        row_copy(0, 0, 0).wait()
        row_copy(0, 1, 0).wait()
        return c

    lax.fori_loop(0, tb, issue, 0, unroll=8)
    lax.fori_loop(0, tb, drain, 0, unroll=8)
    rec = info_ref[...]
    lane = lax.broadcasted_iota(jnp.int32, rec.shape, 1)
    y = _lane_pick(rec, lane, INFO_P1) * buf[0] + _lane_pick(rec, lane, INFO_P2) * buf[1]
    out_ref[...] = x_ref[...] + mod_ref[0, 5:6, :] * y


def _combine(pos, x, mod, info, ys, t_len, per_batch_mod):
    n, d = x.shape
    n_tiles, _, tb = pos.shape
    bidx = (lambda i: (i * tb) // t_len) if per_batch_mod else (lambda i: 0)
    return pl.pallas_call(
        functools.partial(_combine_kernel, tb=tb),
        grid=(n_tiles,),
        in_specs=[pl.BlockSpec((1, 2, tb), lambda i: (i, 0, 0), memory_space=pltpu.SMEM),
                  pl.BlockSpec((tb, d), lambda i: (i, 0)),
                  pl.BlockSpec((1, 6, d), lambda i: (bidx(i), 0, 0)),
                  pl.BlockSpec((tb, LANES), lambda i: (i, 0)),
                  pl.BlockSpec(memory_space=pl.ANY)],
        out_specs=pl.BlockSpec((tb, d), lambda i: (i, 0)),
        out_shape=jax.ShapeDtypeStruct((n, d), F32),
        scratch_shapes=[pltpu.VMEM((2, tb, d), F32), pltpu.SemaphoreType.DMA(())],
        compiler_params=_params("arbitrary"),
        name="moe_combine",
    )(pos, x, mod, info, ys)


def _moe(x, mod, ng, w_router, w1, w3, w2, per_batch_mod):
    bsz, t_len, d = x.shape
    n = bsz * t_len
    n_exp = w_router.shape[1]
    hn, info, info_t, cnt = _router(x, mod, ng, w_router, per_batch_mod)
    info = info.reshape(n, LANES)
    e1, e2 = info_t[INFO_E1].astype(jnp.int32), info_t[INFO_E2].astype(jnp.int32)
    r1, r2 = info_t[INFO_R1].astype(jnp.int32), info_t[INFO_R2].astype(jnp.int32)
    counts = cnt[0, :n_exp].astype(jnp.int32)
    padded = ((counts + GROUP_TILE - 1) // GROUP_TILE) * GROUP_TILE
    ends = jnp.cumsum(padded)
    starts = ends - padded
    tb = _tile(n, ROUTE_TILE)
    pos = jnp.stack([(starts[e1] + r1).reshape(n // tb, tb), (starts[e2] + r2).reshape(n // tb, tb)], axis=1)
    s_max = 2 * n + n_exp * GROUP_TILE
    tile_row = jnp.arange(s_max // GROUP_TILE, dtype=jnp.int32) * GROUP_TILE
    tv = (tile_row < ends[-1]).astype(jnp.int32)
    te = jnp.minimum(jnp.searchsorted(ends, tile_row, side="right"), n_exp - 1).astype(jnp.int32)
    tx = (jnp.minimum(tile_row, ends[-1] - GROUP_TILE) // GROUP_TILE).astype(jnp.int32)
    region_end = ((tile_row + GROUP_TILE)[:, None] == ends[None, :]) & (padded > 0)[None, :]
    zmask = jnp.maximum(1 - tv, jnp.any(region_end, axis=1).astype(jnp.int32))
    xs = _dispatch(zmask, pos, hn.reshape(n, d), s_max)
    ys = _group_ffn(te, tx, tv, xs, w1.astype(BF16), w3.astype(BF16), w2.astype(BF16))
    out = _combine(pos, x.reshape(n, d), mod, info, ys, t_len, per_batch_mod)
    return out.reshape(bsz, t_len, d)


def _dft_channel_kernel(x_ref, mod_ref, ng_ref, cs_ref, y_ref, *, gw):
    hn = _norm_mod(x_ref[0], ng_ref[...], mod_ref[0, 1:2, :], mod_ref[0, 0:1, :]).astype(BF16)
    for g in range(hn.shape[1] // gw):
        y = _dot(hn[:, g * gw:(g + 1) * gw], cs_ref[...])
        y_ref[0, 0, :, g * gw:(g + 1) * gw] = y[:, :gw].astype(BF16)
        y_ref[0, 1, :, g * gw:(g + 1) * gw] = y[:, gw:].astype(BF16)


REV_TILE = 128


def _reverse_shift(src_tiles, wrap_row, m1):
    n_t = len(src_tiles)
    first = lax.broadcasted_iota(jnp.int32, (REV_TILE, 1), 0) == 0
    out = []
    for a in range(n_t):
        body = _dot(m1, src_tiles[n_t - 1 - a])
        head = wrap_row if a == 0 else src_tiles[n_t - a][0:1, :]
        out.append(jnp.where(first, head.astype(F32), body))
    return out


def _dft_fold_kernel(ya_ref, yb_ref, yn_ref, m1_ref, f_ref):
    n_t = ya_ref.shape[2] // REV_TILE
    keep = jnp.where(pl.program_id(1) == 0, 0.0, 1.0)
    for plane, sign in ((0, 1.0), (1, -1.0)):
        tiles = [yb_ref[0, plane, a * REV_TILE:(a + 1) * REV_TILE, :] for a in range(n_t)]
        wrap = yn_ref[0, plane, 0:1, :].astype(F32) * keep
        rev = _reverse_shift(tiles, wrap, m1_ref[...])
        for a in range(n_t):
            rows = slice(a * REV_TILE, (a + 1) * REV_TILE)
            f_ref[0, plane, rows, :] = (ya_ref[0, plane, rows, :].astype(F32) + sign * rev[a]).astype(BF16)


def _dft_time_kernel(wc_ref, ws_ref, wcx_ref, wsx_ref, f_ref, yh_ref, xlo_ref, xhi_ref, mod_ref, fw_ref, fb_ref,
                     m1_ref, lo_ref, hi_ref):
    rk = wc_ref.shape[0]
    ec, od = f_ref[0, 0], f_ref[0, 1]
    y_half = yh_ref[0, 0, 0:1, :].astype(F32) * ((2 * f_ref.shape[2]) ** -0.5)
    parity = lax.broadcasted_iota(jnp.int32, (rk, 1), 0) % 2
    p = _dot(wc_ref[...], ec) + jnp.where(parity == 0, 1.0, -1.0) * y_half
    q = _dot(ws_ref[...], od)
    gate = mod_ref[0, 2:3, :]

    def project(z, x):
        return x + gate * (_dot(z, fw_ref[...]) + fb_ref[...])

    lo_ref[0] = project((p - q).astype(BF16), xlo_ref[0])
    px = _dot(wcx_ref[...], ec)[0:1, :] + y_half
    qx = _dot(wsx_ref[...], od)[0:1, :]
    src = (p + q).astype(BF16)
    tiles = [src[a * REV_TILE:(a + 1) * REV_TILE, :] for a in range(rk // REV_TILE)]
    rev = _reverse_shift(tiles, (px + qx).astype(BF16), m1_ref[...])
    hi_ref[0] = project(jnp.concatenate(rev, axis=0).astype(BF16), xhi_ref[0])


def _dft_matrix(n, scale):
    idx = (np.arange(n)[:, None] * np.arange(n)[None, :]) % n
    ang = 2.0 * np.pi * idx.astype(np.float64) / n
    return np.cos(ang) * scale, np.sin(ang) * scale


def _fourier_mixer(x, mod, ng, fn_w, fn_b, per_batch_mod):
    bsz, t_len, d = x.shape
    gw = d // N_GROUPS
    cg, sg = _dft_matrix(gw, gw ** -0.5)
    cs = jnp.asarray(np.concatenate([cg, sg], axis=1).astype(np.float32)).astype(BF16)
    tm = _tile(t_len, 512)
    bidx = (lambda b: b) if per_batch_mod else (lambda b: 0)
    y = pl.pallas_call(
        functools.partial(_dft_channel_kernel, gw=gw),
        grid=(bsz, t_len // tm),
        in_specs=[pl.BlockSpec((1, tm, d), lambda b, i: (b, i, 0)),
                  pl.BlockSpec((1, 6, d), lambda b, i: (bidx(b), 0, 0)),
                  pl.BlockSpec((1, d), lambda b, i: (0, 0)),
                  pl.BlockSpec((gw, 2 * gw), lambda b, i: (0, 0))],
        out_specs=pl.BlockSpec((1, 2, tm, d), lambda b, i: (b, 0, i, 0)),
        out_shape=jax.ShapeDtypeStruct((bsz, 2, t_len, d), BF16),
        compiler_params=_params("parallel", "parallel"),
        name="dft_channel",
    )(x, mod, ng.reshape(1, d), cs)
    half = t_len // 2
    assert half % REV_TILE == 0 and half % DFT_SPLIT == 0
    pos = np.arange(REV_TILE)
    m1 = jnp.asarray((pos[None, :] == REV_TILE - pos[:, None]).astype(np.float32)).astype(BF16)
    rf = _tile(half, 512)
    nb_f = t_len // rf
    sub = 16
    folded = pl.pallas_call(
        _dft_fold_kernel,
        grid=(bsz, half // rf),
        in_specs=[pl.BlockSpec((1, 2, rf, d), lambda b, i: (b, 0, i, 0)),
                  pl.BlockSpec((1, 2, rf, d), lambda b, i: (b, 0, nb_f - 1 - i, 0)),
                  pl.BlockSpec((1, 2, sub, d), lambda b, i: (b, 0, ((nb_f - i) % nb_f) * (rf // sub), 0)),
                  pl.BlockSpec((REV_TILE, REV_TILE), lambda b, i: (0, 0))],
        out_specs=pl.BlockSpec((1, 2, rf, d), lambda b, i: (b, 0, i, 0)),
        out_shape=jax.ShapeDtypeStruct((bsz, 2, half, d), BF16),
        compiler_params=_params("parallel", "parallel"),
        name="dft_fold",
    )(y, y, y, m1)
    kk = np.arange(half + sub)[:, None]
    ang_hi = 2.0 * np.pi * ((kk * DFT_SPLIT * np.arange(half // DFT_SPLIT)[None, :]) % t_len) / t_len
    ang_lo = 2.0 * np.pi * ((kk * np.arange(DFT_SPLIT)[None, :]) % t_len) / t_len
    scale = t_len ** -0.5
    c_hi, s_hi = (jnp.asarray((f(ang_hi) * scale).astype(np.float32))[:, :, None] for f in (np.cos, np.sin))
    c_lo, s_lo = (jnp.asarray(f(ang_lo).astype(np.float32))[:, None, :] for f in (np.cos, np.sin))
    wc = (c_hi * c_lo - s_hi * s_lo).reshape(half + sub, half).astype(BF16)
    ws = (s_hi * c_lo + c_hi * s_lo).reshape(half + sub, half).astype(BF16)
    rk = _tile(half, 512)
    nk = half // rk
    lo, hi = pl.pallas_call(
        _dft_time_kernel,
        grid=(bsz, nk),
        in_specs=[pl.BlockSpec((rk, half), lambda b, i: (i, 0)),
                  pl.BlockSpec((rk, half), lambda b, i: (i, 0)),
                  pl.BlockSpec((sub, half), lambda b, i: ((i + 1) * (rk // sub), 0)),
                  pl.BlockSpec((sub, half), lambda b, i: ((i + 1) * (rk // sub), 0)),
                  pl.BlockSpec((1, 2, half, d), lambda b, i: (b, 0, 0, 0)),
                  pl.BlockSpec((1, 1, sub, d), lambda b, i: (b, 0, half // sub, 0)),
                  pl.BlockSpec((1, rk, d), lambda b, i: (b, i, 0)),
                  pl.BlockSpec((1, rk, d), lambda b, i: (b, 2 * nk - 1 - i, 0)),
                  pl.BlockSpec((1, 6, d), lambda b, i: (bidx(b), 0, 0)),
                  pl.BlockSpec((d, d), lambda b, i: (0, 0)),
                  pl.BlockSpec((1, d), lambda b, i: (0, 0)),
                  pl.BlockSpec((REV_TILE, REV_TILE), lambda b, i: (0, 0))],
        out_specs=(pl.BlockSpec((1, rk, d), lambda b, i: (b, i, 0)),
                   pl.BlockSpec((1, rk, d), lambda b, i: (b, nk - 1 - i, 0))),
        out_shape=(jax.ShapeDtypeStruct((bsz, half, d), F32), jax.ShapeDtypeStruct((bsz, half, d), F32)),
        compiler_params=_params("parallel", "parallel"),
        name="dft_time",
    )(wc, ws, wc, ws, folded, y, x, x, mod, fn_w.astype(BF16), fn_b.reshape(1, d).astype(F32), m1)
    return jnp.concatenate([lo, hi], axis=1)


def _final_norm_kernel(x_ref, g_ref, o_ref):
    x = x_ref[0]
    o_ref[0] = x * lax.rsqrt(jnp.mean(x * x, axis=-1, keepdims=True) + EPS) * g_ref[...]


def _final_norm(x, g):
    bsz, t_len, d = x.shape
    tm = _tile(t_len, 1024)
    return pl.pallas_call(
        _final_norm_kernel,
        grid=(bsz, t_len // tm),
        in_specs=[pl.BlockSpec((1, tm, d), lambda b, i: (b, i, 0)), pl.BlockSpec((1, d), lambda b, i: (0, 0))],
        out_specs=pl.BlockSpec((1, tm, d), lambda b, i: (b, i, 0)),
        out_shape=jax.ShapeDtypeStruct((bsz, t_len, d), F32),
        compiler_params=_params("parallel", "parallel"),
        name="final_norm",
    )(x, g.reshape(1, d))


def _trunk(x, mod, cache, use_rope, per_batch_mod, p):
    bsz, t_len, d = x.shape
    depth = mod.shape[0]
    flat = (lambda a: a) if per_batch_mod else (lambda a: a.reshape(1, bsz * t_len, a.shape[-1]))
    unflat = (lambda a: a) if per_batch_mod else (lambda a: a.reshape(bsz, t_len, a.shape[-1]))
    states = []
    for i in range(depth):
        j = i // 2
        m_i = mod[i]
        if i % 2 == 0:
            q, kt, v, o, r = _mlstm_inproj(x, m_i, p["norm_g"][i, 0], p["ml_w_in"][j], p["ml_b_gate"][j],
                                           use_rope, per_batch_mod)
            outs = []
            for direction in range(2):
                if cache is None:
                    cn0 = jnp.zeros((bsz, N_HEADS // 2, 2 * HEAD_DK, 2 * HEAD_DV), F32)
                    m0 = jnp.zeros((bsz, N_HEADS, LANES), F32)
                else:
                    cn0, m0 = _pack_state(cache[0][:, j, direction], cache[1][:, j, direction],
                                          cache[2][:, j, direction])
                outs.append(_mlstm_scan(q, kt, v, r, cn0, m0, reverse=bool(direction)))
            states.append([_unpack_state(cn, m) for (_, cn, m) in outs])
            x = _mlstm_out(outs[0][0], outs[1][0], o, x, m_i, p["ml_head_g"][j], p["ml_w_out"][j], per_batch_mod)
            x = unflat(_ffn(flat(x), m_i, p["norm_g"][i, 1], p["ffn_w1"][j], p["ffn_w3"][j], p["ffn_w2"][j],
                            per_batch_mod))
        else:
            x = _fourier_mixer(x, m_i, p["norm_g"][i, 0], p["fn_w"][j], p["fn_b"][j], per_batch_mod)
            x = unflat(_moe(flat(x), m_i, p["norm_g"][i, 1], p["moe_router"][j], p["moe_w1"][j], p["moe_w3"][j],
                            p["moe_w2"][j], per_batch_mod))
    y = unflat(_final_norm(flat(x), p["final_g"]))
    return y, states


def kernel(x_prompt, x_sample, state_C, state_n, state_m, c, c_ctx, w_mod, b_mod, norm_g, final_g,
           ml_w_in, ml_b_gate, ml_head_g, ml_w_out, fn_w, fn_b, ffn_w1, ffn_w3, ffn_w2,
           moe_router, moe_w1, moe_w3, moe_w2):
    p = dict(norm_g=norm_g, final_g=final_g, ml_w_in=ml_w_in, ml_b_gate=ml_b_gate, ml_head_g=ml_head_g,
             ml_w_out=ml_w_out, fn_w=fn_w, fn_b=fn_b, ffn_w1=ffn_w1, ffn_w3=ffn_w3, ffn_w2=ffn_w2,
             moe_router=moe_router, moe_w1=moe_w1, moe_w3=moe_w3, moe_w2=moe_w2)
    depth, d = w_mod.shape[0], w_mod.shape[1]
    n_dec = c.shape[0]
    rows = ((n_dec + 1 + 7) // 8) * 8
    cond = jnp.zeros((rows, d), F32).at[:n_dec].set(c).at[n_dec].set(c_ctx)
    mod = _mod_table(cond, w_mod, b_mod).reshape(depth, rows, 6, d)
    y_prompt, st = _trunk(x_prompt, mod[:, n_dec:n_dec + 1], None, False, False, p)
    y_sample, _ = _trunk(x_sample, mod[:, :n_dec], (state_C, state_n, state_m), True, True, p)
    new_c = jnp.stack([jnp.stack([s[0][0], s[1][0]], axis=1) for s in st], axis=1)
    new_n = jnp.stack([jnp.stack([s[0][1], s[1][1]], axis=1) for s in st], axis=1)
    new_m = jnp.stack([jnp.stack([s[0][2], s[1][2]], axis=1) for s in st], axis=1)
    return (y_prompt, y_sample, new_c.astype(x_prompt.dtype), new_n.astype(x_prompt.dtype),
            new_m.astype(x_prompt.dtype))
```

```python
import functools

import numpy as np
import jax
import jax.numpy as jnp
from jax import lax
from jax.experimental import pallas as pl
from jax.experimental.pallas import tpu as pltpu

F32 = jnp.float32
BF16 = jnp.bfloat16
HIGHEST = lax.Precision.HIGHEST

EPS = 1e-6
N_HEADS = 8
HEAD_DK = 64
HEAD_DV = 128
QK_W = N_HEADS * HEAD_DK
V_W = N_HEADS * HEAD_DV
GRID_W = 64
ROPE_BASE = 10000.0
N_GROUPS = 4
DFT_SPLIT = 64
N_GATES = 4 * N_HEADS
LANES = 128
SCAN_CHUNK = 128
VMEM_LIMIT = 56 * 1024 * 1024
NEG_INF = float("-inf")


def _params(*sem):
    return pltpu.CompilerParams(dimension_semantics=sem, vmem_limit_bytes=VMEM_LIMIT)


def _tile(n, pref):
    t = min(n, pref)
    assert n % t == 0, (n, pref)
    return t


def _norm_mod(x, gain, scale, shift):
    ms = jnp.mean(x * x, axis=-1, keepdims=True)
    return x * lax.rsqrt(ms + EPS) * gain * (1.0 + scale) + shift


def _dot(a, b):
    return jnp.dot(a, b, preferred_element_type=F32)


def _dot_nt(a, b):
    return lax.dot_general(a, b, (((1,), (1,)), ((), ())), preferred_element_type=F32)


def _log_sigmoid(x):
    return jnp.minimum(x, 0.0) - jnp.log1p(jnp.exp(-jnp.abs(x)))


def _mod_kernel(c_ref, w_ref, b_ref, o_ref):
    c = c_ref[...]
    s = c * jax.nn.sigmoid(c)
    o_ref[0] = jnp.dot(s, w_ref[0], preferred_element_type=F32, precision=HIGHEST) + b_ref[0]


def _mod_table(cond, w_mod, b_mod):
    depth, d, n = w_mod.shape
    rows = cond.shape[0]
    tn = _tile(n, 1536)
    return pl.pallas_call(
        _mod_kernel,
        grid=(depth, n // tn),
        in_specs=[pl.BlockSpec((rows, d), lambda l, j: (0, 0)),
                  pl.BlockSpec((1, d, tn), lambda l, j: (l, 0, j)),
                  pl.BlockSpec((1, 1, tn), lambda l, j: (l, 0, j))],
        out_specs=pl.BlockSpec((1, rows, tn), lambda l, j: (l, 0, j)),
        out_shape=jax.ShapeDtypeStruct((depth, rows, n), F32),
        compiler_params=_params("parallel", "parallel"),
        name="adaln_table",
    )(cond, w_mod, b_mod.reshape(depth, 1, n))


def _rope_tables(t_len):
    pos = np.arange(t_len)
    row = (pos // GRID_W).astype(np.float32)
    col = (pos % GRID_W).astype(np.float32)
    nf = HEAD_DK // 4
    inv = (np.float32(ROPE_BASE) ** (-np.arange(nf, dtype=np.float32) / nf)).astype(np.float32)
    d = np.arange(HEAD_DK)
    p = np.where(d[None, :] < HEAD_DK // 2, row[:, None], col[:, None]).astype(np.float32)
    ang = p * inv[d % nf][None, :]
    sign = np.where((d % (2 * nf)) < nf, -1.0, 1.0).astype(np.float32)
    return np.cos(ang).astype(np.float32), (np.sin(ang) * sign[None, :]).astype(np.float32)


def _inproj_kernel(*refs, use_rope):
    if use_rope:
        (x_ref, mod_ref, ng_ref, wq_ref, wkt_ref, wv_ref, wo_ref, wgt_ref, bg_ref, trif_ref, trib_ref,
         cq_ref, sq_ref, ck_ref, sk_ref, q_ref, kt_ref, v_ref, o_ref, r_ref) = refs
    else:
        (x_ref, mod_ref, ng_ref, wq_ref, wkt_ref, wv_ref, wo_ref, wgt_ref, bg_ref, trif_ref, trib_ref,
         q_ref, kt_ref, v_ref, o_ref, r_ref) = refs
    hn = _norm_mod(x_ref[0], ng_ref[...], mod_ref[0, 1:2, :], mod_ref[0, 0:1, :])
    hb = hn.astype(BF16)
    q = _dot(hb, wq_ref[...]) * (HEAD_DK ** -0.5)
    kt = _dot_nt(wkt_ref[...], hb)
    if use_rope:
        nf = HEAD_DK // 4
        lane = lax.broadcasted_iota(jnp.int32, (q.shape[0], LANES), 1)
        first_q = (lane % (2 * nf)) < nf
        sub = lax.broadcasted_iota(jnp.int32, (LANES, kt.shape[1]), 0)
        first_k = (sub % (2 * nf)) < nf
        cq, sq, ck, sk = cq_ref[...], sq_ref[...], ck_ref[...], sk_ref[...]
        for s in range(QK_W // LANES):
            qs = q[:, s * LANES:(s + 1) * LANES]
            sw = jnp.where(first_q, pltpu.roll(qs, LANES - nf, 1), pltpu.roll(qs, nf, 1))
            q_ref[0, :, s * LANES:(s + 1) * LANES] = (qs * cq + sw * sq).astype(BF16)
            ks = kt[s * LANES:(s + 1) * LANES, :]
            sw = jnp.where(first_k, pltpu.roll(ks, LANES - nf, 0), pltpu.roll(ks, nf, 0))
            kt_ref[0, s * LANES:(s + 1) * LANES, :] = (ks * ck + sw * sk).astype(BF16)
    else:
        q_ref[0] = q.astype(BF16)
        kt_ref[0] = kt.astype(BF16)
    v_ref[0] = _dot(hb, wv_ref[...]).astype(BF16)
    o_ref[0] = _dot(hb, wo_ref[...])
    gt = _dot_nt(wgt_ref[...], hb) + bg_ref[...]
    h = N_HEADS
    i_f, f_f = gt[0:h], _log_sigmoid(gt[h:2 * h])
    i_b, f_b = gt[2 * h:3 * h], _log_sigmoid(gt[3 * h:4 * h])
    b_f = jnp.dot(f_f, trif_ref[...], preferred_element_type=F32, precision=HIGHEST)
    b_b = jnp.dot(f_b, trib_ref[...], preferred_element_type=F32, precision=HIGHEST)
    r_ref[0, 0:h, :] = i_f - b_f
    r_ref[0, h:2 * h, :] = b_f
    r_ref[0, 2 * h:3 * h, :] = i_b - b_b
    r_ref[0, 3 * h:4 * h, :] = b_b


def _mlstm_inproj(x, mod, ng, w_in, b_gate, use_rope, per_batch_mod):
    bsz, t_len, d = x.shape
    tm = _tile(t_len, 512)
    wq = w_in[:, :QK_W].astype(BF16)
    wkt = w_in[:, QK_W:2 * QK_W].T.astype(BF16)
    wv = w_in[:, 2 * QK_W:2 * QK_W + V_W].astype(BF16)
    wo = w_in[:, 2 * QK_W + V_W:2 * QK_W + 2 * V_W].astype(BF16)
    wgt = w_in[:, 2 * QK_W + 2 * V_W:].T.astype(BF16)
    bg = b_gate.reshape(N_GATES, 1).astype(F32)
    pos = np.arange(tm)
    same = (pos[:, None] // SCAN_CHUNK) == (pos[None, :] // SCAN_CHUNK)
    trif = jnp.asarray((same & (pos[:, None] <= pos[None, :])).astype(np.float32))
    trib = jnp.asarray((same & (pos[:, None] >= pos[None, :])).astype(np.float32))
    bidx = (lambda b: b) if per_batch_mod else (lambda b: 0)
    const = lambda shp: pl.BlockSpec(shp, lambda b, i: (0,) * len(shp))
    in_specs = [pl.BlockSpec((1, tm, d), lambda b, i: (b, i, 0)),
                pl.BlockSpec((1, 6, d), lambda b, i: (bidx(b), 0, 0)),
                const((1, d)), const((d, QK_W)), const((QK_W, d)), const((d, V_W)), const((d, V_W)),
                const((N_GATES, d)), const((N_GATES, 1)), const((tm, tm)), const((tm, tm))]
    args = [x, mod, ng.reshape(1, d), wq, wkt, wv, wo, wgt, bg, trif, trib]
    if use_rope:
        cos, sin = _rope_tables(t_len)
        rep = LANES // HEAD_DK
        args += [jnp.asarray(np.tile(cos, (1, rep))), jnp.asarray(np.tile(sin, (1, rep))),
                 jnp.asarray(np.tile(cos.T, (rep, 1))), jnp.asarray(np.tile(sin.T, (rep, 1)))]
        in_specs += [pl.BlockSpec((tm, LANES), lambda b, i: (i, 0)), pl.BlockSpec((tm, LANES), lambda b, i: (i, 0)),
                     pl.BlockSpec((LANES, tm), lambda b, i: (0, i)), pl.BlockSpec((LANES, tm), lambda b, i: (0, i))]
    out_shape = (jax.ShapeDtypeStruct((bsz, t_len, QK_W), BF16),
                 jax.ShapeDtypeStruct((bsz, QK_W, t_len), BF16),
                 jax.ShapeDtypeStruct((bsz, t_len, V_W), BF16),
                 jax.ShapeDtypeStruct((bsz, t_len, V_W), F32),
                 jax.ShapeDtypeStruct((bsz, N_GATES, t_len), F32))
    out_specs = (pl.BlockSpec((1, tm, QK_W), lambda b, i: (b, i, 0)),
                 pl.BlockSpec((1, QK_W, tm), lambda b, i: (b, 0, i)),
                 pl.BlockSpec((1, tm, V_W), lambda b, i: (b, i, 0)),
                 pl.BlockSpec((1, tm, V_W), lambda b, i: (b, i, 0)),
                 pl.BlockSpec((1, N_GATES, tm), lambda b, i: (b, 0, i)))
    return pl.pallas_call(
        functools.partial(_inproj_kernel, use_rope=use_rope),
        grid=(bsz, t_len // tm), in_specs=in_specs, out_specs=out_specs, out_shape=out_shape,
        compiler_params=_params("parallel", "parallel"),
        name="mlstm_inproj",
    )(*args)


def _scan_kernel(q_ref, kt_ref, v_ref, r_ref, cn0_ref, m0_ref, h_ref, cn_out_ref, m_out_ref,
                 cn_scr, m_scr, *, reverse, n_chunks):
    L = SCAN_CHUNK
    j = pl.program_id(1)

    @pl.when(j == 0)
    def _():
        cn_scr[...] = cn0_ref[0]
        m_scr[...] = m0_ref[0]

    row_i = lax.broadcasted_iota(jnp.int32, (L, L), 0)
    col_i = lax.broadcasted_iota(jnp.int32, (L, L), 1)
    visible = (col_i >= row_i) if reverse else (col_i <= row_i)
    diag = col_i == row_i
    upper_lanes = lax.broadcasted_iota(jnp.int32, (L, LANES), 1) >= HEAD_DK
    ones = jnp.ones((L, LANES), BF16)
    no_rows = jnp.zeros((HEAD_DK, 2 * HEAD_DV), BF16)
    end_lane = lax.broadcasted_iota(jnp.int32, (1, L), 1) == (0 if reverse else L - 1)
    order = range(n_chunks - 1, -1, -1) if reverse else range(n_chunks)
    head_rows = [slice((h % 2) * HEAD_DK, (h % 2 + 1) * HEAD_DK) for h in range(N_HEADS)]
    states = [(cn_scr[h // 2, head_rows[h], :], m_scr[h:h + 1, 0:1]) for h in range(N_HEADS)]
    new_states = []
    for h in range(N_HEADS):
        p, e = h // 2, h % 2
        cn, m = states[h]
        for c in order:
            r0 = c * L
            q_pair = q_ref[0, r0:r0 + L, p * LANES:(p + 1) * LANES]
            q_m = jnp.where(upper_lanes if e else jnp.logical_not(upper_lanes), q_pair, jnp.zeros_like(q_pair))
            kt_pair = kt_ref[0, p * LANES:(p + 1) * LANES, r0:r0 + L]
            kt_h = kt_ref[0, h * HEAD_DK:(h + 1) * HEAD_DK, r0:r0 + L]
            a_row = r_ref[0, h:h + 1, r0:r0 + L]
            b_row = r_ref[0, N_HEADS + h:N_HEADS + h + 1, r0:r0 + L]
            v_aug = jnp.concatenate([v_ref[0, r0:r0 + L, h * HEAD_DV:(h + 1) * HEAD_DV], ones], axis=1)
            am = jnp.where(visible, a_row, NEG_INF)
            row_max = jnp.max(am, axis=1, keepdims=True)
            a_max = jnp.max(a_row, axis=1, keepdims=True)
            s = (_dot(q_m, kt_pair) * jnp.exp(am - row_max)).astype(BF16)
            intra = _dot(s, v_aug)
            kw = (kt_h.astype(F32) * jnp.exp(a_row - a_max)).astype(BF16)
            upd = _dot(kw, v_aug)
            b_col = jnp.sum(jnp.where(diag, b_row, 0.0), axis=1, keepdims=True)
            b_end = jnp.sum(jnp.where(end_lane, b_row, 0.0), axis=1, keepdims=True)
            u = jnp.maximum(m, row_max)
            prev_scale = jnp.exp(m - u)
            loc_scale = jnp.exp(row_max - u)
            guard = jnp.exp(-(b_col + u))
            cn_b = cn.astype(BF16)
            cn_pair = jnp.concatenate([no_rows, cn_b] if e else [cn_b, no_rows], axis=0)
            inter = _dot(q_m, cn_pair)
            num = prev_scale * inter[:, :HEAD_DV] + loc_scale * intra[:, :HEAD_DV]
            den = prev_scale * inter[:, HEAD_DV:] + loc_scale * intra[:, HEAD_DV:]
            h_ref[0, r0:r0 + L, h * HEAD_DV:(h + 1) * HEAD_DV] = num / jnp.maximum(jnp.abs(den), guard)
            u_end = jnp.maximum(m, a_max)
            cn = jnp.exp(m - u_end) * cn + jnp.exp(a_max - u_end) * upd
            m = b_end + u_end
        new_states.append((cn, m))
    for h in range(N_HEADS):
        cn_scr[h // 2, head_rows[h], :] = new_states[h][0]
        m_scr[h:h + 1, :] = jnp.broadcast_to(new_states[h][1], (1, LANES))

    @pl.when(j == pl.num_programs(1) - 1)
    def _():
        cn_out_ref[0] = cn_scr[...]
        m_out_ref[0] = m_scr[...]


def _mlstm_scan(q, kt, v, r, cn0, m0, reverse):
    bsz, t_len, _ = q.shape
    tb = _tile(t_len, 4 * SCAN_CHUNK)
    nblk = t_len // tb
    blk = (lambda j: nblk - 1 - j) if reverse else (lambda j: j)
    d = 1 if reverse else 0
    half = N_HEADS // 2
    return pl.pallas_call(
        functools.partial(_scan_kernel, reverse=reverse, n_chunks=tb // SCAN_CHUNK),
        grid=(bsz, nblk),
        in_specs=[pl.BlockSpec((1, tb, QK_W), lambda b, j: (b, blk(j), 0)),
                  pl.BlockSpec((1, QK_W, tb), lambda b, j: (b, 0, blk(j))),
                  pl.BlockSpec((1, tb, V_W), lambda b, j: (b, blk(j), 0)),
                  pl.BlockSpec((1, 2 * N_HEADS, tb), lambda b, j: (b, d, blk(j))),
                  pl.BlockSpec((1, half, 2 * HEAD_DK, 2 * HEAD_DV), lambda b, j: (b, 0, 0, 0)),
                  pl.BlockSpec((1, N_HEADS, LANES), lambda b, j: (b, 0, 0))],
        out_specs=(pl.BlockSpec((1, tb, V_W), lambda b, j: (b, blk(j), 0)),
                   pl.BlockSpec((1, half, 2 * HEAD_DK, 2 * HEAD_DV), lambda b, j: (b, 0, 0, 0)),
                   pl.BlockSpec((1, N_HEADS, LANES), lambda b, j: (b, 0, 0))),
        out_shape=(jax.ShapeDtypeStruct((bsz, t_len, V_W), F32),
                   jax.ShapeDtypeStruct((bsz, half, 2 * HEAD_DK, 2 * HEAD_DV), F32),
                   jax.ShapeDtypeStruct((bsz, N_HEADS, LANES), F32)),
        scratch_shapes=[pltpu.VMEM((half, 2 * HEAD_DK, 2 * HEAD_DV), F32),
                        pltpu.VMEM((N_HEADS, LANES), F32)],
        compiler_params=_params("parallel", "arbitrary"),
        name="mlstm_scan_bwd" if reverse else "mlstm_scan_fwd",
    )(q, kt, v, r, cn0, m0)


def _pack_state(c, n, m):
    bsz = c.shape[0]
    cn = jnp.concatenate([c, jnp.broadcast_to(n[..., None], n.shape + (HEAD_DV,))], axis=-1)
    cn = cn.reshape(bsz, N_HEADS // 2, 2 * HEAD_DK, 2 * HEAD_DV)
    return cn.astype(F32), jnp.broadcast_to(m[..., None], (bsz, N_HEADS, LANES)).astype(F32)


def _unpack_state(cn, m):
    bsz = cn.shape[0]
    cn = cn.reshape(bsz, N_HEADS, HEAD_DK, 2 * HEAD_DV)
    return cn[..., :HEAD_DV], cn[..., HEAD_DV], m[..., 0]


def _mlstm_out_kernel(hf_ref, hb_ref, o_ref, x_ref, mod_ref, hg_ref, w_ref, out_ref):
    hs = hf_ref[0] + hb_ref[0]
    parts = []
    for h in range(N_HEADS):
        z = hs[:, h * HEAD_DV:(h + 1) * HEAD_DV]
        parts.append(z * lax.rsqrt(jnp.mean(z * z, axis=-1, keepdims=True) + EPS))
    hn = jnp.concatenate(parts, axis=1) * hg_ref[...] * jax.nn.sigmoid(o_ref[0])
    y = _dot(hn.astype(BF16), w_ref[...])
    out_ref[0] = x_ref[0] + mod_ref[0, 2:3, :] * y


def _mlstm_out(hf, hb, o, x, mod, head_g, w_out, per_batch_mod):
    bsz, t_len, d = x.shape
    tm = _tile(t_len, 512)
    bidx = (lambda b: b) if per_batch_mod else (lambda b: 0)
    tok = lambda w: pl.BlockSpec((1, tm, w), lambda b, i: (b, i, 0))
    return pl.pallas_call(
        _mlstm_out_kernel,
        grid=(bsz, t_len // tm),
        in_specs=[tok(V_W), tok(V_W), tok(V_W), tok(d),
                  pl.BlockSpec((1, 6, d), lambda b, i: (bidx(b), 0, 0)),
                  pl.BlockSpec((1, V_W), lambda b, i: (0, 0)),
                  pl.BlockSpec((V_W, d), lambda b, i: (0, 0))],
        out_specs=tok(d),
        out_shape=jax.ShapeDtypeStruct((bsz, t_len, d), F32),
        compiler_params=_params("parallel", "parallel"),
        name="mlstm_out",
    )(hf, hb, o, x, mod, head_g.reshape(1, V_W).astype(F32), w_out.astype(BF16))


def _ffn_kernel(x_ref, mod_ref, ng_ref, w1_ref, w3_ref, w2_ref, out_ref, hn_scr, acc_scr):
    f = pl.program_id(2)

    @pl.when(f == 0)
    def _():
        hn = _norm_mod(x_ref[0], ng_ref[...], mod_ref[0, 4:5, :], mod_ref[0, 3:4, :])
        hn_scr[...] = hn.astype(BF16)
        acc_scr[...] = jnp.zeros_like(acc_scr)

    hb = hn_scr[...]
    h1 = _dot(hb, w1_ref[...])
    h3 = _dot(hb, w3_ref[...])
    act = (h1 * jax.nn.sigmoid(h1) * h3).astype(BF16)
    acc_scr[...] += _dot(act, w2_ref[...])

    @pl.when(f == pl.num_programs(2) - 1)
    def _():
        out_ref[0] = x_ref[0] + mod_ref[0, 5:6, :] * acc_scr[...]


def _ffn(x, mod, ng, w1, w3, w2, per_batch_mod):
    bsz, t_len, d = x.shape
    d_ff = w1.shape[1]
    tm = _tile(t_len, 1024)
    tf = _tile(d_ff, 512)
    bidx = (lambda b: b) if per_batch_mod else (lambda b: 0)
    return pl.pallas_call(
        _ffn_kernel,
        grid=(bsz, t_len // tm, d_ff // tf),
        in_specs=[pl.BlockSpec((1, tm, d), lambda b, i, f: (b, i, 0)),
                  pl.BlockSpec((1, 6, d), lambda b, i, f: (bidx(b), 0, 0)),
                  pl.BlockSpec((1, d), lambda b, i, f: (0, 0)),
                  pl.BlockSpec((d, tf), lambda b, i, f: (0, f)),
                  pl.BlockSpec((d, tf), lambda b, i, f: (0, f)),
                  pl.BlockSpec((tf, d), lambda b, i, f: (f, 0))],
        out_specs=pl.BlockSpec((1, tm, d), lambda b, i, f: (b, i, 0)),
        out_shape=jax.ShapeDtypeStruct((bsz, t_len, d), F32),
        scratch_shapes=[pltpu.VMEM((tm, d), BF16), pltpu.VMEM((tm, d), F32)],
        compiler_params=_params("parallel", "parallel", "arbitrary"),
        name="dense_swiglu",
    )(x, mod, ng.reshape(1, d), w1.astype(BF16), w3.astype(BF16), w2.astype(BF16))


ROUTE_TILE = 512
GROUP_TILE = 512
INFO_E1, INFO_E2, INFO_P1, INFO_P2, INFO_R1, INFO_R2 = range(6)
INFO_ROWS = 8


def _lane_pick(rec, lane, k):
    return jnp.sum(jnp.where(lane == k, rec, 0.0), axis=-1, keepdims=True)


def _router_kernel(x_ref, mod_ref, ng_ref, wr_ref, tri_ref, hn_ref, info_ref, infot_ref, cnt_ref, carry, *, n_exp):
    @pl.when((pl.program_id(0) == 0) & (pl.program_id(1) == 0))
    def _():
        carry[...] = jnp.zeros_like(carry)

    hn = _norm_mod(x_ref[0], ng_ref[...], mod_ref[0, 4:5, :], mod_ref[0, 3:4, :])
    hn_ref[0] = hn
    hn_hi = hn.astype(BF16)
    hn_lo = (hn - hn_hi.astype(F32)).astype(BF16)
    hh = _dot(hn_hi, wr_ref[...])
    logits = hh[:, :LANES] + hh[:, LANES:] + _dot(hn_lo, wr_ref[:, :LANES])
    lane = lax.broadcasted_iota(jnp.int32, logits.shape, 1).astype(F32)
    logits = jnp.where(lane < n_exp, logits, NEG_INF)
    v1 = jnp.max(logits, axis=-1, keepdims=True)
    i1 = jnp.min(jnp.where(logits == v1, lane, float(LANES)), axis=-1, keepdims=True)
    rest = jnp.where(lane == i1, NEG_INF, logits)
    v2 = jnp.max(rest, axis=-1, keepdims=True)
    i2 = jnp.min(jnp.where(rest == v2, lane, float(LANES)), axis=-1, keepdims=True)
    e2 = jnp.exp(v2 - v1)
    den = 1.0 + e2
    chosen = jnp.where((lane == i1) | (lane == i2), 1.0, 0.0)
    before = _dot(tri_ref[...], chosen.astype(BF16)) + carry[...]
    r1 = jnp.sum(jnp.where(lane == i1, before, 0.0), axis=-1, keepdims=True)
    r2 = jnp.sum(jnp.where(lane == i2, before, 0.0), axis=-1, keepdims=True)
    total = carry[...] + jnp.sum(chosen, axis=0, keepdims=True)
    carry[...] = total
    cnt_ref[...] = total
    rec = jnp.zeros_like(logits)
    for k, val in ((INFO_E1, i1), (INFO_E2, i2), (INFO_P1, 1.0 / den), (INFO_P2, e2 / den),
                   (INFO_R1, r1), (INFO_R2, r2)):
        rec = jnp.where(lane == k, val, rec)
    info_ref[0] = rec
    infot_ref[...] = rec.T[0:INFO_ROWS, :]


def _router(x, mod, ng, w_router, per_batch_mod):
    bsz, t_len, d = x.shape
    n_exp = w_router.shape[1]
    tm = _tile(t_len, ROUTE_TILE)
    wr = jnp.zeros((d, LANES), F32).at[:, :n_exp].set(w_router)
    wr_hi = wr.astype(BF16)
    wr = jnp.concatenate([wr_hi, (wr - wr_hi.astype(F32)).astype(BF16)], axis=1)
    pos = np.arange(tm)
    tri = jnp.asarray((pos[None, :] < pos[:, None]).astype(np.float32)).astype(BF16)
    bidx = (lambda b: b) if per_batch_mod else (lambda b: 0)
    return pl.pallas_call(
        functools.partial(_router_kernel, n_exp=n_exp),
        grid=(bsz, t_len // tm),
        in_specs=[pl.BlockSpec((1, tm, d), lambda b, i: (b, i, 0)),
                  pl.BlockSpec((1, 6, d), lambda b, i: (bidx(b), 0, 0)),
                  pl.BlockSpec((1, d), lambda b, i: (0, 0)),
                  pl.BlockSpec((d, 2 * LANES), lambda b, i: (0, 0)),
                  pl.BlockSpec((tm, tm), lambda b, i: (0, 0))],
        out_specs=(pl.BlockSpec((1, tm, d), lambda b, i: (b, i, 0)),
                   pl.BlockSpec((1, tm, LANES), lambda b, i: (b, i, 0)),
                   pl.BlockSpec((INFO_ROWS, tm), lambda b, i: (0, b * (t_len // tm) + i)),
                   pl.BlockSpec((1, LANES), lambda b, i: (0, 0))),
        out_shape=(jax.ShapeDtypeStruct((bsz, t_len, d), F32),
                   jax.ShapeDtypeStruct((bsz, t_len, LANES), F32),
                   jax.ShapeDtypeStruct((INFO_ROWS, bsz * t_len), F32),
                   jax.ShapeDtypeStruct((1, LANES), F32)),
        scratch_shapes=[pltpu.VMEM((1, LANES), F32)],
        compiler_params=_params("arbitrary", "arbitrary"),
        name="moe_router",
    )(x, mod, ng.reshape(1, d), wr, tri)


def _dispatch_kernel(zmask_ref, pos_ref, hn_ref, xs_ref, zeros, sem, *, tb, n_row_tiles):
    def row_copy(r, slot):
        return pltpu.make_async_copy(hn_ref.at[pl.ds(r, 1)], xs_ref.at[pl.ds(slot, 1)], sem)

    def tile_fill(r):
        row0 = pl.multiple_of(r * GROUP_TILE, GROUP_TILE)
        return pltpu.make_async_copy(zeros, xs_ref.at[pl.ds(row0, GROUP_TILE)], sem)

    @pl.when(pl.program_id(0) == 0)
    def _():
        zeros[...] = jnp.zeros_like(zeros)

        def fill(r, c):
            @pl.when(zmask_ref[r] != 0)
            def _():
                tile_fill(r).start()
            return c

        def fill_done(r, c):
            @pl.when(zmask_ref[r] != 0)
            def _():
                tile_fill(r).wait()
            return c

        lax.fori_loop(0, n_row_tiles, fill, 0)
        lax.fori_loop(0, n_row_tiles, fill_done, 0)

    def issue(r, c):
        row_copy(r, pos_ref[0, 0, r]).start()
        row_copy(r, pos_ref[0, 1, r]).start()
        return c

    def drain(r, c):
        row_copy(0, 0).wait()
        row_copy(0, 0).wait()
        return c

    lax.fori_loop(0, tb, issue, 0, unroll=8)
    lax.fori_loop(0, tb, drain, 0, unroll=8)


def _dispatch(zmask, pos, hn, s_max):
    n, d = hn.shape
    n_tiles, _, tb = pos.shape
    return pl.pallas_call(
        functools.partial(_dispatch_kernel, tb=tb, n_row_tiles=zmask.shape[0]),
        grid_spec=pltpu.PrefetchScalarGridSpec(
            num_scalar_prefetch=1,
            grid=(n_tiles,),
            in_specs=[pl.BlockSpec((1, 2, tb), lambda i, zm: (i, 0, 0), memory_space=pltpu.SMEM),
                      pl.BlockSpec((tb, d), lambda i, zm: (i, 0))],
            out_specs=pl.BlockSpec(memory_space=pl.ANY),
            scratch_shapes=[pltpu.VMEM((GROUP_TILE, d), F32), pltpu.SemaphoreType.DMA(())]),
        out_shape=jax.ShapeDtypeStruct((s_max, d), F32),
        compiler_params=_params("arbitrary"),
        name="moe_dispatch",
    )(zmask, pos, hn)


def _group_ffn_kernel(te_ref, tx_ref, tv_ref, x_ref, w1_ref, w3_ref, w2_ref, o_ref, xb_scr, acc_scr):
    r, f = pl.program_id(0), pl.program_id(1)
    last = pl.num_programs(1) - 1

    @pl.when(tv_ref[r] != 0)
    def _():
        @pl.when(f == 0)
        def _():
            xb_scr[...] = x_ref[...].astype(BF16)

        xb = xb_scr[...]
        h1 = _dot(xb, w1_ref[0, 0])
        h3 = _dot(xb, w3_ref[0, 0])
        act = (h1 * jax.nn.sigmoid(h1) * h3).astype(BF16)
        y = _dot(act, w2_ref[0, 0])

        @pl.when(f == 0)
        def _():
            acc_scr[...] = y

        @pl.when(f != 0)
        def _():
            acc_scr[...] += y

        @pl.when(f == last)
        def _():
            o_ref[...] = acc_scr[...]

    @pl.when((tv_ref[r] == 0) & (f == last))
    def _():
        o_ref[...] = jnp.zeros_like(o_ref)


def _group_ffn(te, tx, tv, xs, w1, w3, w2, layer):
    s_max, d = xs.shape
    d_ff = w1.shape[3]
    tf = _tile(d_ff, 1792)
    nf = d_ff // tf
    fidx = lambda r, f, tv: f * tv[r] + (nf - 1) * (1 - tv[r])
    return pl.pallas_call(
        _group_ffn_kernel,
        grid_spec=pltpu.PrefetchScalarGridSpec(
            num_scalar_prefetch=3,
            grid=(s_max // GROUP_TILE, nf),
            in_specs=[pl.BlockSpec((GROUP_TILE, d), lambda r, f, te, tx, tv: (tx[r], 0)),
                      pl.BlockSpec((1, 1, d, tf), lambda r, f, te, tx, tv: (layer, te[r], 0, fidx(r, f, tv))),
                      pl.BlockSpec((1, 1, d, tf), lambda r, f, te, tx, tv: (layer, te[r], 0, fidx(r, f, tv))),
                      pl.BlockSpec((1, 1, tf, d), lambda r, f, te, tx, tv: (layer, te[r], fidx(r, f, tv), 0))],
            out_specs=pl.BlockSpec((GROUP_TILE, d), lambda r, f, te, tx, tv: (r, 0)),
            scratch_shapes=[pltpu.VMEM((GROUP_TILE, d), BF16), pltpu.VMEM((GROUP_TILE, d), F32)]),
        out_shape=jax.ShapeDtypeStruct((s_max, d), F32),
        compiler_params=_params("parallel", "arbitrary"),
        name="moe_group_swiglu",
    )(te, tx, tv, xs, w1, w3, w2)


def _combine_kernel(pos_ref, x_ref, mod_ref, info_ref, fg_ref, ys_ref, out_ref, buf, sem, *, tb, final_norm):
    def row_copy(slot, k, r):
        return pltpu.make_async_copy(ys_ref.at[pl.ds(slot, 1)], buf.at[k, pl.ds(r, 1)], sem)

    def issue(r, c):
        row_copy(pos_ref[0, 0, r], 0, r).start(priority=0)
        row_copy(pos_ref[0, 1, r], 1, r).start(priority=1)
        return c

    def drain(r, c):
        row_copy(0, 0, 0).wait()
        row_copy(0, 1, 0).wait()
        return c

    lax.fori_loop(0, tb, issue, 0, unroll=8)
    lax.fori_loop(0, tb, drain, 0, unroll=8)
    rec = info_ref[...]
    lane = lax.broadcasted_iota(jnp.int32, rec.shape, 1)
    y = _lane_pick(rec, lane, INFO_P1) * buf[0] + _lane_pick(rec, lane, INFO_P2) * buf[1]
    out = x_ref[...] + mod_ref[0, 5:6, :] * y
    if final_norm:
        out = out * lax.rsqrt(jnp.mean(out * out, axis=-1, keepdims=True) + EPS) * fg_ref[...]
    out_ref[...] = out


def _combine(pos, x, mod, info, ys, t_len, per_batch_mod, final_g):
    n, d = x.shape
    n_tiles, _, tb = pos.shape
    bidx = (lambda i: (i * tb) // t_len) if per_batch_mod else (lambda i: 0)
    final_norm = final_g is not None
    gain = (final_g if final_norm else jnp.ones((d,), F32)).reshape(1, d).astype(F32)
    return pl.pallas_call(
        functools.partial(_combine_kernel, tb=tb, final_norm=final_norm),
        grid=(n_tiles,),
        in_specs=[pl.BlockSpec((1, 2, tb), lambda i: (i, 0, 0), memory_space=pltpu.SMEM),
                  pl.BlockSpec((tb, d), lambda i: (i, 0)),
                  pl.BlockSpec((1, 6, d), lambda i: (bidx(i), 0, 0)),
                  pl.BlockSpec((tb, LANES), lambda i: (i, 0)),
                  pl.BlockSpec((1, d), lambda i: (0, 0)),
                  pl.BlockSpec(memory_space=pl.ANY)],
        out_specs=pl.BlockSpec((tb, d), lambda i: (i, 0)),
        out_shape=jax.ShapeDtypeStruct((n, d), F32),
        scratch_shapes=[pltpu.VMEM((2, tb, d), F32), pltpu.SemaphoreType.DMA(())],
        compiler_params=_params("arbitrary"),
        name="moe_combine",
    )(pos, x, mod, info, gain, ys)


def _moe(x, mod, ng, w_router, w1, w3, w2, layer, per_batch_mod, final_g=None):
    bsz, t_len, d = x.shape
    n = bsz * t_len
    n_exp = w_router.shape[1]
    hn, info, info_t, cnt = _router(x, mod, ng, w_router, per_batch_mod)
    info = info.reshape(n, LANES)
    e1, e2 = info_t[INFO_E1].astype(jnp.int32), info_t[INFO_E2].astype(jnp.int32)
    r1, r2 = info_t[INFO_R1].astype(jnp.int32), info_t[INFO_R2].astype(jnp.int32)
    counts = cnt[0, :n_exp].astype(jnp.int32)
    padded = ((counts + GROUP_TILE - 1) // GROUP_TILE) * GROUP_TILE
    ends = jnp.cumsum(padded)
    starts = ends - padded
    tb = _tile(n, ROUTE_TILE)
    pos = jnp.stack([(starts[e1] + r1).reshape(n // tb, tb), (starts[e2] + r2).reshape(n // tb, tb)], axis=1)
    s_max = 2 * n + n_exp * GROUP_TILE
    tile_row = jnp.arange(s_max // GROUP_TILE, dtype=jnp.int32) * GROUP_TILE
    tv = (tile_row < ends[-1]).astype(jnp.int32)
    te = jnp.minimum(jnp.searchsorted(ends, tile_row, side="right"), n_exp - 1).astype(jnp.int32)
    tx = (jnp.minimum(tile_row, ends[-1] - GROUP_TILE) // GROUP_TILE).astype(jnp.int32)
    region_end = ((tile_row + GROUP_TILE)[:, None] == ends[None, :]) & (padded > 0)[None, :]
    zmask = jnp.maximum(1 - tv, jnp.any(region_end, axis=1).astype(jnp.int32))
    xs = _dispatch(zmask, pos, hn.reshape(n, d), s_max)
    ys = _group_ffn(te, tx, tv, xs, w1.astype(BF16), w3.astype(BF16), w2.astype(BF16), layer)
    out = _combine(pos, x.reshape(n, d), mod, info, ys, t_len, per_batch_mod, final_g)
    return out.reshape(bsz, t_len, d)


def _dft_channel_kernel(x_ref, mod_ref, ng_ref, cs_ref, y_ref, *, gw):
    hn = _norm_mod(x_ref[0], ng_ref[...], mod_ref[0, 1:2, :], mod_ref[0, 0:1, :]).astype(BF16)
    for g in range(hn.shape[1] // gw):
        y = _dot(hn[:, g * gw:(g + 1) * gw], cs_ref[...])
        y_ref[0, 0, :, g * gw:(g + 1) * gw] = y[:, :gw].astype(BF16)
        y_ref[0, 1, :, g * gw:(g + 1) * gw] = y[:, gw:].astype(BF16)


REV_TILE = 128


def _reverse_shift(src_tiles, wrap_row, m1):
    n_t = len(src_tiles)
    first = lax.broadcasted_iota(jnp.int32, (REV_TILE, 1), 0) == 0
    out = []
    for a in range(n_t):
        body = _dot(m1, src_tiles[n_t - 1 - a])
        head = wrap_row if a == 0 else src_tiles[n_t - a][0:1, :]
        out.append(jnp.where(first, head.astype(F32), body))
    return out


def _dft_fold_kernel(ya_ref, yb_ref, yn_ref, m1_ref, f_ref):
    n_t = ya_ref.shape[2] // REV_TILE
    keep = jnp.where(pl.program_id(1) == 0, 0.0, 1.0)
    for plane, sign in ((0, 1.0), (1, -1.0)):
        tiles = [yb_ref[0, plane, a * REV_TILE:(a + 1) * REV_TILE, :] for a in range(n_t)]
        wrap = yn_ref[0, plane, 0:1, :].astype(F32) * keep
        rev = _reverse_shift(tiles, wrap, m1_ref[...])
        for a in range(n_t):
            rows = slice(a * REV_TILE, (a + 1) * REV_TILE)
            f_ref[0, plane, rows, :] = (ya_ref[0, plane, rows, :].astype(F32) + sign * rev[a]).astype(BF16)


def _dft_time_kernel(wc_ref, ws_ref, wcx_ref, wsx_ref, f_ref, yh_ref, xlo_ref, xhi_ref, mod_ref, fw_ref, fb_ref,
                     m1_ref, lo_ref, hi_ref):
    rk = wc_ref.shape[0]
    ec, od = f_ref[0, 0], f_ref[0, 1]
    y_half = yh_ref[0, 0, 0:1, :].astype(F32) * ((2 * f_ref.shape[2]) ** -0.5)
    parity = lax.broadcasted_iota(jnp.int32, (rk, 1), 0) % 2
    p = _dot(wc_ref[...], ec) + jnp.where(parity == 0, 1.0, -1.0) * y_half
    q = _dot(ws_ref[...], od)
    gate = mod_ref[0, 2:3, :]

    def project(z, x):
        return x + gate * (_dot(z, fw_ref[...]) + fb_ref[...])

    lo_ref[0] = project((p - q).astype(BF16), xlo_ref[0])
    px = _dot(wcx_ref[...], ec)[0:1, :] + y_half
    qx = _dot(wsx_ref[...], od)[0:1, :]
    src = (p + q).astype(BF16)
    tiles = [src[a * REV_TILE:(a + 1) * REV_TILE, :] for a in range(rk // REV_TILE)]
    rev = _reverse_shift(tiles, (px + qx).astype(BF16), m1_ref[...])
    hi_ref[0] = project(jnp.concatenate(rev, axis=0).astype(BF16), xhi_ref[0])


def _dft_matrix(n, scale):
    idx = (np.arange(n)[:, None] * np.arange(n)[None, :]) % n
    ang = 2.0 * np.pi * idx.astype(np.float64) / n
    return np.cos(ang) * scale, np.sin(ang) * scale


def _fourier_mixer(x, mod, ng, fn_w, fn_b, per_batch_mod):
    bsz, t_len, d = x.shape
    gw = d // N_GROUPS
    cg, sg = _dft_matrix(gw, gw ** -0.5)
    cs = jnp.asarray(np.concatenate([cg, sg], axis=1).astype(np.float32)).astype(BF16)
    tm = _tile(t_len, 512)
    bidx = (lambda b: b) if per_batch_mod else (lambda b: 0)
    y = pl.pallas_call(
        functools.partial(_dft_channel_kernel, gw=gw),
        grid=(bsz, t_len // tm),
        in_specs=[pl.BlockSpec((1, tm, d), lambda b, i: (b, i, 0)),
                  pl.BlockSpec((1, 6, d), lambda b, i: (bidx(b), 0, 0)),
                  pl.BlockSpec((1, d), lambda b, i: (0, 0)),
                  pl.BlockSpec((gw, 2 * gw), lambda b, i: (0, 0))],
        out_specs=pl.BlockSpec((1, 2, tm, d), lambda b, i: (b, 0, i, 0)),
        out_shape=jax.ShapeDtypeStruct((bsz, 2, t_len, d), BF16),
        compiler_params=_params("parallel", "parallel"),
        name="dft_channel",
    )(x, mod, ng.reshape(1, d), cs)
    half = t_len // 2
    assert half % REV_TILE == 0 and half % DFT_SPLIT == 0
    pos = np.arange(REV_TILE)
    m1 = jnp.asarray((pos[None, :] == REV_TILE - pos[:, None]).astype(np.float32)).astype(BF16)
    rf = _tile(half, 512)
    nb_f = t_len // rf
    sub = 16
    folded = pl.pallas_call(
        _dft_fold_kernel,
        grid=(bsz, half // rf),
        in_specs=[pl.BlockSpec((1, 2, rf, d), lambda b, i: (b, 0, i, 0)),
                  pl.BlockSpec((1, 2, rf, d), lambda b, i: (b, 0, nb_f - 1 - i, 0)),
                  pl.BlockSpec((1, 2, sub, d), lambda b, i: (b, 0, ((nb_f - i) % nb_f) * (rf // sub), 0)),
                  pl.BlockSpec((REV_TILE, REV_TILE), lambda b, i: (0, 0))],
        out_specs=pl.BlockSpec((1, 2, rf, d), lambda b, i: (b, 0, i, 0)),
        out_shape=jax.ShapeDtypeStruct((bsz, 2, half, d), BF16),
        compiler_params=_params("parallel", "parallel"),
        name="dft_fold",
    )(y, y, y, m1)
    kk = np.arange(half + sub)[:, None]
    ang_hi = 2.0 * np.pi * ((kk * DFT_SPLIT * np.arange(half // DFT_SPLIT)[None, :]) % t_len) / t_len
    ang_lo = 2.0 * np.pi * ((kk * np.arange(DFT_SPLIT)[None, :]) % t_len) / t_len
    scale = t_len ** -0.5
    c_hi, s_hi = (jnp.asarray((f(ang_hi) * scale).astype(np.float32))[:, :, None] for f in (np.cos, np.sin))
    c_lo, s_lo = (jnp.asarray(f(ang_lo).astype(np.float32))[:, None, :] for f in (np.cos, np.sin))
    wc = (c_hi * c_lo - s_hi * s_lo).reshape(half + sub, half).astype(BF16)
    ws = (s_hi * c_lo + c_hi * s_lo).reshape(half + sub, half).astype(BF16)
    rk = _tile(half, 512)
    nk = half // rk
    lo, hi = pl.pallas_call(
        _dft_time_kernel,
        grid=(bsz, nk),
        in_specs=[pl.BlockSpec((rk, half), lambda b, i: (i, 0)),
                  pl.BlockSpec((rk, half), lambda b, i: (i, 0)),
                  pl.BlockSpec((sub, half), lambda b, i: ((i + 1) * (rk // sub), 0)),
                  pl.BlockSpec((sub, half), lambda b, i: ((i + 1) * (rk // sub), 0)),
                  pl.BlockSpec((1, 2, half, d), lambda b, i: (b, 0, 0, 0)),
                  pl.BlockSpec((1, 1, sub, d), lambda b, i: (b, 0, half // sub, 0)),
                  pl.BlockSpec((1, rk, d), lambda b, i: (b, i, 0)),
                  pl.BlockSpec((1, rk, d), lambda b, i: (b, 2 * nk - 1 - i, 0)),
                  pl.BlockSpec((1, 6, d), lambda b, i: (bidx(b), 0, 0)),
                  pl.BlockSpec((d, d), lambda b, i: (0, 0)),
                  pl.BlockSpec((1, d), lambda b, i: (0, 0)),
                  pl.BlockSpec((REV_TILE, REV_TILE), lambda b, i: (0, 0))],
        out_specs=(pl.BlockSpec((1, rk, d), lambda b, i: (b, i, 0)),
                   pl.BlockSpec((1, rk, d), lambda b, i: (b, nk - 1 - i, 0))),
        out_shape=(jax.ShapeDtypeStruct((bsz, half, d), F32), jax.ShapeDtypeStruct((bsz, half, d), F32)),
        compiler_params=_params("parallel", "parallel"),
        name="dft_time",
    )(wc, ws, wc, ws, folded, y, x, x, mod, fn_w.astype(BF16), fn_b.reshape(1, d).astype(F32), m1)
    return jnp.concatenate([lo, hi], axis=1)


def _final_norm_kernel(x_ref, g_ref, o_ref):
    x = x_ref[0]
    o_ref[0] = x * lax.rsqrt(jnp.mean(x * x, axis=-1, keepdims=True) + EPS) * g_ref[...]


def _final_norm(x, g):
    bsz, t_len, d = x.shape
    tm = _tile(t_len, 1024)
    return pl.pallas_call(
        _final_norm_kernel,
        grid=(bsz, t_len // tm),
        in_specs=[pl.BlockSpec((1, tm, d), lambda b, i: (b, i, 0)), pl.BlockSpec((1, d), lambda b, i: (0, 0))],
        out_specs=pl.BlockSpec((1, tm, d), lambda b, i: (b, i, 0)),
        out_shape=jax.ShapeDtypeStruct((bsz, t_len, d), F32),
        compiler_params=_params("parallel", "parallel"),
        name="final_norm",
    )(x, g.reshape(1, d))


def _trunk(x, mod, cache, use_rope, per_batch_mod, p):
    bsz, t_len, d = x.shape
    depth = mod.shape[0]
    flat = (lambda a: a) if per_batch_mod else (lambda a: a.reshape(1, bsz * t_len, a.shape[-1]))
    unflat = (lambda a: a) if per_batch_mod else (lambda a: a.reshape(bsz, t_len, a.shape[-1]))
    states = []
    for i in range(depth):
        j = i // 2
        m_i = mod[i]
        if i % 2 == 0:
            q, kt, v, o, r = _mlstm_inproj(x, m_i, p["norm_g"][i, 0], p["ml_w_in"][j], p["ml_b_gate"][j],
                                           use_rope, per_batch_mod)
            outs = []
            for direction in range(2):
                if cache is None:
                    cn0 = jnp.zeros((bsz, N_HEADS // 2, 2 * HEAD_DK, 2 * HEAD_DV), F32)
                    m0 = jnp.zeros((bsz, N_HEADS, LANES), F32)
                else:
                    cn0, m0 = _pack_state(cache[0][:, j, direction], cache[1][:, j, direction],
                                          cache[2][:, j, direction])
                outs.append(_mlstm_scan(q, kt, v, r, cn0, m0, reverse=bool(direction)))
            states.append([_unpack_state(cn, m) for (_, cn, m) in outs])
            x = _mlstm_out(outs[0][0], outs[1][0], o, x, m_i, p["ml_head_g"][j], p["ml_w_out"][j], per_batch_mod)
            x = unflat(_ffn(flat(x), m_i, p["norm_g"][i, 1], p["ffn_w1"][j], p["ffn_w3"][j], p["ffn_w2"][j],
                            per_batch_mod))
        else:
            x = _fourier_mixer(x, m_i, p["norm_g"][i, 0], p["fn_w"][j], p["fn_b"][j], per_batch_mod)
            closing = p["final_g"] if i == depth - 1 else None
            x = unflat(_moe(flat(x), m_i, p["norm_g"][i, 1], p["moe_router"][j], p["moe_w1"], p["moe_w3"],
                            p["moe_w2"], j, per_batch_mod, closing))
    if depth % 2 == 1:
        x = unflat(_final_norm(flat(x), p["final_g"]))
    return x, states


def kernel(x_prompt, x_sample, state_C, state_n, state_m, c, c_ctx, w_mod, b_mod, norm_g, final_g,
           ml_w_in, ml_b_gate, ml_head_g, ml_w_out, fn_w, fn_b, ffn_w1, ffn_w3, ffn_w2,
           moe_router, moe_w1, moe_w3, moe_w2):
    p = dict(norm_g=norm_g, final_g=final_g, ml_w_in=ml_w_in, ml_b_gate=ml_b_gate, ml_head_g=ml_head_g,
             ml_w_out=ml_w_out, fn_w=fn_w, fn_b=fn_b, ffn_w1=ffn_w1, ffn_w3=ffn_w3, ffn_w2=ffn_w2,
             moe_router=moe_router, moe_w1=moe_w1, moe_w3=moe_w3, moe_w2=moe_w2)
    depth, d = w_mod.shape[0], w_mod.shape[1]
    n_dec = c.shape[0]
    rows = ((n_dec + 1 + 7) // 8) * 8
    cond = jnp.zeros((rows, d), F32).at[:n_dec].set(c).at[n_dec].set(c_ctx)
    mod = _mod_table(cond, w_mod, b_mod).reshape(depth, rows, 6, d)
    y_prompt, st = _trunk(x_prompt, mod[:, n_dec:n_dec + 1], None, False, False, p)
    y_sample, _ = _trunk(x_sample, mod[:, :n_dec], (state_C, state_n, state_m), True, True, p)
    new_c = jnp.stack([jnp.stack([s[0][0], s[1][0]], axis=1) for s in st], axis=1)
    new_n = jnp.stack([jnp.stack([s[0][1], s[1][1]], axis=1) for s in st], axis=1)
    new_m = jnp.stack([jnp.stack([s[0][2], s[1][2]], axis=1) for s in st], axis=1)
    return (y_prompt, y_sample, new_c.astype(x_prompt.dtype), new_n.astype(x_prompt.dtype),
            new_m.astype(x_prompt.dtype))
```

```python
import functools

import numpy as np
import jax
import jax.numpy as jnp
from jax import lax
from jax.experimental import pallas as pl
from jax.experimental.pallas import tpu as pltpu

F32 = jnp.float32
BF16 = jnp.bfloat16
HIGHEST = lax.Precision.HIGHEST

EPS = 1e-6
N_HEADS = 8
HEAD_DK = 64
HEAD_DV = 128
QK_W = N_HEADS * HEAD_DK
V_W = N_HEADS * HEAD_DV
GRID_W = 64
ROPE_BASE = 10000.0
N_GROUPS = 4
DFT_SPLIT = 64
N_GATES = 4 * N_HEADS
LANES = 128
SCAN_CHUNK = 128
SCAN_ROWS = 3 * N_HEADS
VMEM_LIMIT = 56 * 1024 * 1024
NEG_INF = float("-inf")


def _params(*sem):
    return pltpu.CompilerParams(dimension_semantics=sem, vmem_limit_bytes=VMEM_LIMIT)


def _tile(n, pref):
    t = min(n, pref)
    assert n % t == 0, (n, pref)
    return t


def _norm_mod(x, gain, scale, shift):
    ms = jnp.mean(x * x, axis=-1, keepdims=True)
    return x * lax.rsqrt(ms + EPS) * gain * (1.0 + scale) + shift


def _dot(a, b):
    return jnp.dot(a, b, preferred_element_type=F32)


def _dot_nt(a, b):
    return lax.dot_general(a, b, (((1,), (1,)), ((), ())), preferred_element_type=F32)


def _log_sigmoid(x):
    return jnp.minimum(x, 0.0) - jnp.log1p(jnp.exp(-jnp.abs(x)))


def _mod_kernel(c_ref, w_ref, b_ref, o_ref):
    c = c_ref[...]
    s = c * jax.nn.sigmoid(c)
    o_ref[0] = jnp.dot(s, w_ref[0], preferred_element_type=F32, precision=HIGHEST) + b_ref[0]


def _mod_table(cond, w_mod, b_mod):
    depth, d, n = w_mod.shape
    rows = cond.shape[0]
    tn = _tile(n, 1536)
    return pl.pallas_call(
        _mod_kernel,
        grid=(depth, n // tn),
        in_specs=[pl.BlockSpec((rows, d), lambda l, j: (0, 0)),
                  pl.BlockSpec((1, d, tn), lambda l, j: (l, 0, j)),
                  pl.BlockSpec((1, 1, tn), lambda l, j: (l, 0, j))],
        out_specs=pl.BlockSpec((1, rows, tn), lambda l, j: (l, 0, j)),
        out_shape=jax.ShapeDtypeStruct((depth, rows, n), F32),
        compiler_params=_params("parallel", "parallel"),
        name="adaln_table",
    )(cond, w_mod, b_mod.reshape(depth, 1, n))


def _rope_tables(t_len):
    pos = np.arange(t_len)
    row = (pos // GRID_W).astype(np.float32)
    col = (pos % GRID_W).astype(np.float32)
    nf = HEAD_DK // 4
    inv = (np.float32(ROPE_BASE) ** (-np.arange(nf, dtype=np.float32) / nf)).astype(np.float32)
    d = np.arange(HEAD_DK)
    p = np.where(d[None, :] < HEAD_DK // 2, row[:, None], col[:, None]).astype(np.float32)
    ang = p * inv[d % nf][None, :]
    sign = np.where((d % (2 * nf)) < nf, -1.0, 1.0).astype(np.float32)
    return np.cos(ang).astype(np.float32), (np.sin(ang) * sign[None, :]).astype(np.float32)


def _inproj_kernel(*refs, use_rope):
    if use_rope:
        (x_ref, mod_ref, ng_ref, wq_ref, wkt_ref, wv_ref, wo_ref, wgt_ref, bg_ref, trif_ref, trib_ref,
         cq_ref, sq_ref, ck_ref, sk_ref, q_ref, kt_ref, v_ref, o_ref, r_ref) = refs
    else:
        (x_ref, mod_ref, ng_ref, wq_ref, wkt_ref, wv_ref, wo_ref, wgt_ref, bg_ref, trif_ref, trib_ref,
         q_ref, kt_ref, v_ref, o_ref, r_ref) = refs
    hn = _norm_mod(x_ref[0], ng_ref[...], mod_ref[0, 1:2, :], mod_ref[0, 0:1, :])
    hb = hn.astype(BF16)
    q = _dot(hb, wq_ref[...]) * (HEAD_DK ** -0.5)
    kt = _dot_nt(wkt_ref[...], hb)
    if use_rope:
        nf = HEAD_DK // 4
        lane = lax.broadcasted_iota(jnp.int32, (q.shape[0], LANES), 1)
        first_q = (lane % (2 * nf)) < nf
        sub = lax.broadcasted_iota(jnp.int32, (LANES, kt.shape[1]), 0)
        first_k = (sub % (2 * nf)) < nf
        cq, sq, ck, sk = cq_ref[...], sq_ref[...], ck_ref[...], sk_ref[...]
        for s in range(QK_W // LANES):
            qs = q[:, s * LANES:(s + 1) * LANES]
            sw = jnp.where(first_q, pltpu.roll(qs, LANES - nf, 1), pltpu.roll(qs, nf, 1))
            q_ref[0, :, s * LANES:(s + 1) * LANES] = (qs * cq + sw * sq).astype(BF16)
            ks = kt[s * LANES:(s + 1) * LANES, :]
            sw = jnp.where(first_k, pltpu.roll(ks, LANES - nf, 0), pltpu.roll(ks, nf, 0))
            kt_ref[0, s * LANES:(s + 1) * LANES, :] = (ks * ck + sw * sk).astype(BF16)
    else:
        q_ref[0] = q.astype(BF16)
        kt_ref[0] = kt.astype(BF16)
    v_ref[0] = _dot(hb, wv_ref[...]).astype(BF16)
    o_ref[0] = _dot(hb, wo_ref[...]).astype(BF16)
    gt = _dot_nt(wgt_ref[...], hb) + bg_ref[...]
    h = N_HEADS
    i_f, f_f = gt[0:h], _log_sigmoid(gt[h:2 * h])
    i_b, f_b = gt[2 * h:3 * h], _log_sigmoid(gt[3 * h:4 * h])
    b_f = jnp.dot(f_f, trif_ref[...], preferred_element_type=F32, precision=HIGHEST)
    b_b = jnp.dot(f_b, trib_ref[...], preferred_element_type=F32, precision=HIGHEST)
    a_f, a_b = i_f - b_f, i_b - b_b
    width = a_f.shape[1]
    in_chunk = lax.broadcasted_iota(jnp.int32, a_f.shape, 1) % SCAN_CHUNK
    c_f, c_b = a_f, a_b
    step = 1
    while step < SCAN_CHUNK:
        c_f = jnp.where(in_chunk >= step, jnp.maximum(c_f, pltpu.roll(c_f, step, 1)), c_f)
        c_b = jnp.where(in_chunk < SCAN_CHUNK - step, jnp.maximum(c_b, pltpu.roll(c_b, width - step, 1)), c_b)
        step *= 2
    for k, rows in enumerate((a_f, b_f, c_f, a_b, b_b, c_b)):
        r_ref[0, k * h:(k + 1) * h, :] = rows


def _mlstm_inproj(x, mod, ng, w_in, b_gate, use_rope, per_batch_mod):
    bsz, t_len, d = x.shape
    tm = _tile(t_len, 512)
    wq = w_in[:, :QK_W].astype(BF16)
    wkt = w_in[:, QK_W:2 * QK_W].T.astype(BF16)
    wv = w_in[:, 2 * QK_W:2 * QK_W + V_W].astype(BF16)
    wo = w_in[:, 2 * QK_W + V_W:2 * QK_W + 2 * V_W].astype(BF16)
    wgt = w_in[:, 2 * QK_W + 2 * V_W:].T.astype(BF16)
    bg = b_gate.reshape(N_GATES, 1).astype(F32)
    pos = np.arange(tm)
    same = (pos[:, None] // SCAN_CHUNK) == (pos[None, :] // SCAN_CHUNK)
    trif = jnp.asarray((same & (pos[:, None] <= pos[None, :])).astype(np.float32))
    trib = jnp.asarray((same & (pos[:, None] >= pos[None, :])).astype(np.float32))
    bidx = (lambda b: b) if per_batch_mod else (lambda b: 0)
    const = lambda shp: pl.BlockSpec(shp, lambda b, i: (0,) * len(shp))
    in_specs = [pl.BlockSpec((1, tm, d), lambda b, i: (b, i, 0)),
                pl.BlockSpec((1, 6, d), lambda b, i: (bidx(b), 0, 0)),
                const((1, d)), const((d, QK_W)), const((QK_W, d)), const((d, V_W)), const((d, V_W)),
                const((N_GATES, d)), const((N_GATES, 1)), const((tm, tm)), const((tm, tm))]
    args = [x, mod, ng.reshape(1, d), wq, wkt, wv, wo, wgt, bg, trif, trib]
    if use_rope:
        cos, sin = _rope_tables(t_len)
        rep = LANES // HEAD_DK
        args += [jnp.asarray(np.tile(cos, (1, rep))), jnp.asarray(np.tile(sin, (1, rep))),
                 jnp.asarray(np.tile(cos.T, (rep, 1))), jnp.asarray(np.tile(sin.T, (rep, 1)))]
        in_specs += [pl.BlockSpec((tm, LANES), lambda b, i: (i, 0)), pl.BlockSpec((tm, LANES), lambda b, i: (i, 0)),
                     pl.BlockSpec((LANES, tm), lambda b, i: (0, i)), pl.BlockSpec((LANES, tm), lambda b, i: (0, i))]
    out_shape = (jax.ShapeDtypeStruct((bsz, t_len, QK_W), BF16),
                 jax.ShapeDtypeStruct((bsz, QK_W, t_len), BF16),
                 jax.ShapeDtypeStruct((bsz, t_len, V_W), BF16),
                 jax.ShapeDtypeStruct((bsz, t_len, V_W), BF16),
                 jax.ShapeDtypeStruct((bsz, 2 * SCAN_ROWS, t_len), F32))
    out_specs = (pl.BlockSpec((1, tm, QK_W), lambda b, i: (b, i, 0)),
                 pl.BlockSpec((1, QK_W, tm), lambda b, i: (b, 0, i)),
                 pl.BlockSpec((1, tm, V_W), lambda b, i: (b, i, 0)),
                 pl.BlockSpec((1, tm, V_W), lambda b, i: (b, i, 0)),
                 pl.BlockSpec((1, 2 * SCAN_ROWS, tm), lambda b, i: (b, 0, i)))
    return pl.pallas_call(
        functools.partial(_inproj_kernel, use_rope=use_rope),
        grid=(bsz, t_len // tm), in_specs=in_specs, out_specs=out_specs, out_shape=out_shape,
        compiler_params=_params("parallel", "parallel"),
        name="mlstm_inproj",
    )(*args)


def _scan_kernel(q_ref, kt_ref, v_ref, r_ref, cn0_ref, m0_ref, spread_ref, h_ref, cn_out_ref, m_out_ref,
                 cn_scr, m_scr, *, reverse, n_chunks):
    L = SCAN_CHUNK
    j = pl.program_id(1)

    @pl.when(j == 0)
    def _():
        cn_scr[...] = cn0_ref[0]
        m_scr[...] = m0_ref[0]

    row_i = lax.broadcasted_iota(jnp.int32, (L, L), 0)
    col_i = lax.broadcasted_iota(jnp.int32, (L, L), 1)
    visible = (col_i >= row_i) if reverse else (col_i <= row_i)
    upper_lanes = lax.broadcasted_iota(jnp.int32, (L, LANES), 1) >= HEAD_DK
    ones = jnp.ones((L, LANES), BF16)
    no_rows = jnp.zeros((HEAD_DK, 2 * HEAD_DV), BF16)
    end_lane = lax.broadcasted_iota(jnp.int32, (1, L), 1) == (0 if reverse else L - 1)
    order = range(n_chunks - 1, -1, -1) if reverse else range(n_chunks)
    head_rows = [slice((h % 2) * HEAD_DK, (h % 2 + 1) * HEAD_DK) for h in range(N_HEADS)]
    states = [(cn_scr[h // 2, head_rows[h], :], m_scr[h:h + 1, 0:1]) for h in range(N_HEADS)]

    def columns(c):
        x = r_ref[0, N_HEADS:3 * N_HEADS, c * L:(c + 1) * L]
        hi = x.astype(BF16).astype(F32)
        mid = (x - hi).astype(BF16).astype(F32)
        lo = x - hi - mid
        parts = jnp.concatenate([hi, mid, lo, jnp.zeros((L - 6 * N_HEADS, L), F32)], axis=0)
        return _dot(parts.T.astype(BF16), spread_ref[...])

    cols = {c: columns(c) for c in order}
    new_states = []
    for h in range(N_HEADS):
        p, e = h // 2, h % 2
        cn, m = states[h]
        for c in order:
            r0 = c * L
            q_pair = q_ref[0, r0:r0 + L, p * LANES:(p + 1) * LANES]
            q_m = jnp.where(upper_lanes if e else jnp.logical_not(upper_lanes), q_pair, jnp.zeros_like(q_pair))
            kt_pair = kt_ref[0, p * LANES:(p + 1) * LANES, r0:r0 + L]
            kt_h = kt_ref[0, h * HEAD_DK:(h + 1) * HEAD_DK, r0:r0 + L]
            a_row = r_ref[0, h:h + 1, r0:r0 + L]
            b_row = r_ref[0, N_HEADS + h:N_HEADS + h + 1, r0:r0 + L]
            v_aug = jnp.concatenate([v_ref[0, r0:r0 + L, h * HEAD_DV:(h + 1) * HEAD_DV], ones], axis=1)
            am = jnp.where(visible, a_row, NEG_INF)
            a_max = jnp.max(a_row, axis=1, keepdims=True)
            kw = (kt_h.astype(F32) * jnp.exp(a_row - a_max)).astype(BF16)
            upd = _dot(kw, v_aug)
            b_col = cols[c][:, h * LANES:(h + 1) * LANES]
            b_end = jnp.sum(jnp.where(end_lane, b_row, 0.0), axis=1, keepdims=True)
            u = jnp.maximum(m, cols[c][:, (N_HEADS + h) * LANES:(N_HEADS + h + 1) * LANES])
            guard = jnp.exp(-(b_col + u))
            s = (_dot(q_m, kt_pair) * jnp.exp(am - u)).astype(BF16)
            q_old = (q_m.astype(F32) * jnp.exp(m - u)).astype(BF16)
            cn_b = cn.astype(BF16)
            cn_pair = jnp.concatenate([no_rows, cn_b] if e else [cn_b, no_rows], axis=0)
            both = _dot(jnp.concatenate([q_old, s], axis=1), jnp.concatenate([cn_pair, v_aug], axis=0))
            num, den = both[:, :HEAD_DV], both[:, HEAD_DV:]
            h_ref[0, r0:r0 + L, h * HEAD_DV:(h + 1) * HEAD_DV] = (num / jnp.maximum(jnp.abs(den), guard)).astype(BF16)
            u_end = jnp.maximum(m, a_max)
            cn = jnp.exp(m - u_end) * cn + jnp.exp(a_max - u_end) * upd
            m = b_end + u_end
        new_states.append((cn, m))
    for h in range(N_HEADS):
        cn_scr[h // 2, head_rows[h], :] = new_states[h][0]
        m_scr[h:h + 1, :] = jnp.broadcast_to(new_states[h][1], (1, LANES))

    @pl.when(j == pl.num_programs(1) - 1)
    def _():
        cn_out_ref[0] = cn_scr[...]
        m_out_ref[0] = m_scr[...]


def _mlstm_scan(q, kt, v, r, cn0, m0, reverse):
    bsz, t_len, _ = q.shape
    tb = _tile(t_len, 4 * SCAN_CHUNK)
    nblk = t_len // tb
    blk = (lambda j: nblk - 1 - j) if reverse else (lambda j: j)
    d = 1 if reverse else 0
    half = N_HEADS // 2
    src = np.arange(LANES)[:, None]
    dst = np.arange(2 * N_HEADS * LANES)[None, :]
    spread = jnp.asarray(((src < 6 * N_HEADS) & (src % (2 * N_HEADS) == dst // LANES)).astype(np.float32)).astype(BF16)
    return pl.pallas_call(
        functools.partial(_scan_kernel, reverse=reverse, n_chunks=tb // SCAN_CHUNK),
        grid=(bsz, nblk),
        in_specs=[pl.BlockSpec((1, tb, QK_W), lambda b, j: (b, blk(j), 0)),
                  pl.BlockSpec((1, QK_W, tb), lambda b, j: (b, 0, blk(j))),
                  pl.BlockSpec((1, tb, V_W), lambda b, j: (b, blk(j), 0)),
                  pl.BlockSpec((1, SCAN_ROWS, tb), lambda b, j: (b, d, blk(j))),
                  pl.BlockSpec((1, half, 2 * HEAD_DK, 2 * HEAD_DV), lambda b, j: (b, 0, 0, 0)),
                  pl.BlockSpec((1, N_HEADS, LANES), lambda b, j: (b, 0, 0)),
                  pl.BlockSpec((LANES, 2 * N_HEADS * LANES), lambda b, j: (0, 0))],
        out_specs=(pl.BlockSpec((1, tb, V_W), lambda b, j: (b, blk(j), 0)),
                   pl.BlockSpec((1, half, 2 * HEAD_DK, 2 * HEAD_DV), lambda b, j: (b, 0, 0, 0)),
                   pl.BlockSpec((1, N_HEADS, LANES), lambda b, j: (b, 0, 0))),
        out_shape=(jax.ShapeDtypeStruct((bsz, t_len, V_W), BF16),
                   jax.ShapeDtypeStruct((bsz, half, 2 * HEAD_DK, 2 * HEAD_DV), F32),
                   jax.ShapeDtypeStruct((bsz, N_HEADS, LANES), F32)),
        scratch_shapes=[pltpu.VMEM((half, 2 * HEAD_DK, 2 * HEAD_DV), F32),
                        pltpu.VMEM((N_HEADS, LANES), F32)],
        compiler_params=_params("parallel", "arbitrary"),
        name="mlstm_scan_bwd" if reverse else "mlstm_scan_fwd",
    )(q, kt, v, r, cn0, m0, spread)


def _pack_state(c, n, m):
    bsz = c.shape[0]
    cn = jnp.concatenate([c, jnp.broadcast_to(n[..., None], n.shape + (HEAD_DV,))], axis=-1)
    cn = cn.reshape(bsz, N_HEADS // 2, 2 * HEAD_DK, 2 * HEAD_DV)
    return cn.astype(F32), jnp.broadcast_to(m[..., None], (bsz, N_HEADS, LANES)).astype(F32)


def _unpack_state(cn, m):
    bsz = cn.shape[0]
    cn = cn.reshape(bsz, N_HEADS, HEAD_DK, 2 * HEAD_DV)
    return cn[..., :HEAD_DV], cn[..., HEAD_DV], m[..., 0]


def _mlstm_out_kernel(hf_ref, hb_ref, o_ref, x_ref, mod_ref, hg_ref, w_ref, out_ref):
    hs = hf_ref[0].astype(F32) + hb_ref[0].astype(F32)
    parts = []
    for h in range(N_HEADS):
        z = hs[:, h * HEAD_DV:(h + 1) * HEAD_DV]
        parts.append(z * lax.rsqrt(jnp.mean(z * z, axis=-1, keepdims=True) + EPS))
    hn = jnp.concatenate(parts, axis=1) * hg_ref[...] * jax.nn.sigmoid(o_ref[0].astype(F32))
    y = _dot(hn.astype(BF16), w_ref[...])
    out_ref[0] = x_ref[0] + mod_ref[0, 2:3, :] * y


def _mlstm_out(hf, hb, o, x, mod, head_g, w_out, per_batch_mod):
    bsz, t_len, d = x.shape
    tm = _tile(t_len, 512)
    bidx = (lambda b: b) if per_batch_mod else (lambda b: 0)
    tok = lambda w: pl.BlockSpec((1, tm, w), lambda b, i: (b, i, 0))
    return pl.pallas_call(
        _mlstm_out_kernel,
        grid=(bsz, t_len // tm),
        in_specs=[tok(V_W), tok(V_W), tok(V_W), tok(d),
                  pl.BlockSpec((1, 6, d), lambda b, i: (bidx(b), 0, 0)),
                  pl.BlockSpec((1, V_W), lambda b, i: (0, 0)),
                  pl.BlockSpec((V_W, d), lambda b, i: (0, 0))],
        out_specs=tok(d),
        out_shape=jax.ShapeDtypeStruct((bsz, t_len, d), F32),
        compiler_params=_params("parallel", "parallel"),
        name="mlstm_out",
    )(hf, hb, o, x, mod, head_g.reshape(1, V_W).astype(F32), w_out.astype(BF16))


def _ffn_kernel(x_ref, mod_ref, ng_ref, w1_ref, w3_ref, w2_ref, out_ref, hn_scr, acc_scr):
    f = pl.program_id(2)

    @pl.when(f == 0)
    def _():
        hn = _norm_mod(x_ref[0], ng_ref[...], mod_ref[0, 4:5, :], mod_ref[0, 3:4, :])
        hn_scr[...] = hn.astype(BF16)
        acc_scr[...] = jnp.zeros_like(acc_scr)

    hb = hn_scr[...]
    h1 = _dot(hb, w1_ref[...])
    h3 = _dot(hb, w3_ref[...])
    act = (h1 * jax.nn.sigmoid(h1) * h3).astype(BF16)
    acc_scr[...] += _dot(act, w2_ref[...])

    @pl.when(f == pl.num_programs(2) - 1)
    def _():
        out_ref[0] = x_ref[0] + mod_ref[0, 5:6, :] * acc_scr[...]


def _ffn(x, mod, ng, w1, w3, w2, per_batch_mod):
    bsz, t_len, d = x.shape
    d_ff = w1.shape[1]
    tm = _tile(t_len, 1024)
    tf = _tile(d_ff, 512)
    bidx = (lambda b: b) if per_batch_mod else (lambda b: 0)
    return pl.pallas_call(
        _ffn_kernel,
        grid=(bsz, t_len // tm, d_ff // tf),
        in_specs=[pl.BlockSpec((1, tm, d), lambda b, i, f: (b, i, 0)),
                  pl.BlockSpec((1, 6, d), lambda b, i, f: (bidx(b), 0, 0)),
                  pl.BlockSpec((1, d), lambda b, i, f: (0, 0)),
                  pl.BlockSpec((d, tf), lambda b, i, f: (0, f)),
                  pl.BlockSpec((d, tf), lambda b, i, f: (0, f)),
                  pl.BlockSpec((tf, d), lambda b, i, f: (f, 0))],
        out_specs=pl.BlockSpec((1, tm, d), lambda b, i, f: (b, i, 0)),
        out_shape=jax.ShapeDtypeStruct((bsz, t_len, d), F32),
        scratch_shapes=[pltpu.VMEM((tm, d), BF16), pltpu.VMEM((tm, d), F32)],
        compiler_params=_params("parallel", "parallel", "arbitrary"),
        name="dense_swiglu",
    )(x, mod, ng.reshape(1, d), w1.astype(BF16), w3.astype(BF16), w2.astype(BF16))


ROUTE_TILE = 512
GROUP_TILE = 512
INFO_E1, INFO_E2, INFO_P1, INFO_P2, INFO_R1, INFO_R2 = range(6)
INFO_ROWS = 8


def _lane_pick(rec, lane, k):
    return jnp.sum(jnp.where(lane == k, rec, 0.0), axis=-1, keepdims=True)


def _router_kernel(x_ref, mod_ref, ng_ref, wr_ref, tri_ref, hn_ref, info_ref, infot_ref, cnt_ref, carry, *, n_exp):
    @pl.when((pl.program_id(0) == 0) & (pl.program_id(1) == 0))
    def _():
        carry[...] = jnp.zeros_like(carry)

    hn = _norm_mod(x_ref[0], ng_ref[...], mod_ref[0, 4:5, :], mod_ref[0, 3:4, :])
    hn_ref[0] = hn
    hn_hi = hn.astype(BF16)
    hn_lo = (hn - hn_hi.astype(F32)).astype(BF16)
    hh = _dot(hn_hi, wr_ref[...])
    logits = hh[:, :LANES] + hh[:, LANES:] + _dot(hn_lo, wr_ref[:, :LANES])
    lane = lax.broadcasted_iota(jnp.int32, logits.shape, 1).astype(F32)
    logits = jnp.where(lane < n_exp, logits, NEG_INF)
    v1 = jnp.max(logits, axis=-1, keepdims=True)
    i1 = jnp.min(jnp.where(logits == v1, lane, float(LANES)), axis=-1, keepdims=True)
    rest = jnp.where(lane == i1, NEG_INF, logits)
    v2 = jnp.max(rest, axis=-1, keepdims=True)
    i2 = jnp.min(jnp.where(rest == v2, lane, float(LANES)), axis=-1, keepdims=True)
    e2 = jnp.exp(v2 - v1)
    den = 1.0 + e2
    chosen = jnp.where((lane == i1) | (lane == i2), 1.0, 0.0)
    before = _dot(tri_ref[...], chosen.astype(BF16)) + carry[...]
    r1 = jnp.sum(jnp.where(lane == i1, before, 0.0), axis=-1, keepdims=True)
    r2 = jnp.sum(jnp.where(lane == i2, before, 0.0), axis=-1, keepdims=True)
    total = carry[...] + jnp.sum(chosen, axis=0, keepdims=True)
    carry[...] = total
    cnt_ref[...] = total
    rec = jnp.zeros_like(logits)
    for k, val in ((INFO_E1, i1), (INFO_E2, i2), (INFO_P1, 1.0 / den), (INFO_P2, e2 / den),
                   (INFO_R1, r1), (INFO_R2, r2)):
        rec = jnp.where(lane == k, val, rec)
    info_ref[0] = rec
    infot_ref[...] = rec.T[0:INFO_ROWS, :]


def _router(x, mod, ng, w_router, per_batch_mod):
    bsz, t_len, d = x.shape
    n_exp = w_router.shape[1]
    tm = _tile(t_len, ROUTE_TILE)
    wr = jnp.zeros((d, LANES), F32).at[:, :n_exp].set(w_router)
    wr_hi = wr.astype(BF16)
    wr = jnp.concatenate([wr_hi, (wr - wr_hi.astype(F32)).astype(BF16)], axis=1)
    pos = np.arange(tm)
    tri = jnp.asarray((pos[None, :] < pos[:, None]).astype(np.float32)).astype(BF16)
    bidx = (lambda b: b) if per_batch_mod else (lambda b: 0)
    return pl.pallas_call(
        functools.partial(_router_kernel, n_exp=n_exp),
        grid=(bsz, t_len // tm),
        in_specs=[pl.BlockSpec((1, tm, d), lambda b, i: (b, i, 0)),
                  pl.BlockSpec((1, 6, d), lambda b, i: (bidx(b), 0, 0)),
                  pl.BlockSpec((1, d), lambda b, i: (0, 0)),
                  pl.BlockSpec((d, 2 * LANES), lambda b, i: (0, 0)),
                  pl.BlockSpec((tm, tm), lambda b, i: (0, 0))],
        out_specs=(pl.BlockSpec((1, tm, d), lambda b, i: (b, i, 0)),
                   pl.BlockSpec((1, tm, LANES), lambda b, i: (b, i, 0)),
                   pl.BlockSpec((INFO_ROWS, tm), lambda b, i: (0, b * (t_len // tm) + i)),
                   pl.BlockSpec((1, LANES), lambda b, i: (0, 0))),
        out_shape=(jax.ShapeDtypeStruct((bsz, t_len, d), F32),
                   jax.ShapeDtypeStruct((bsz, t_len, LANES), F32),
                   jax.ShapeDtypeStruct((INFO_ROWS, bsz * t_len), F32),
                   jax.ShapeDtypeStruct((1, LANES), F32)),
        scratch_shapes=[pltpu.VMEM((1, LANES), F32)],
        compiler_params=_params("arbitrary", "arbitrary"),
        name="moe_router",
    )(x, mod, ng.reshape(1, d), wr, tri)


def _dispatch_kernel(zmask_ref, pos_ref, hn_ref, xs_ref, zeros, sem, *, tb, n_row_tiles):
    def row_copy(r, slot):
        return pltpu.make_async_copy(hn_ref.at[pl.ds(r, 1)], xs_ref.at[pl.ds(slot, 1)], sem)

    def tile_fill(r):
        row0 = pl.multiple_of(r * GROUP_TILE, GROUP_TILE)
        return pltpu.make_async_copy(zeros, xs_ref.at[pl.ds(row0, GROUP_TILE)], sem)

    @pl.when(pl.program_id(0) == 0)
    def _():
        zeros[...] = jnp.zeros_like(zeros)

        def fill(r, c):
            @pl.when(zmask_ref[r] != 0)
            def _():
                tile_fill(r).start()
            return c

        def fill_done(r, c):
            @pl.when(zmask_ref[r] != 0)
            def _():
                tile_fill(r).wait()
            return c

        lax.fori_loop(0, n_row_tiles, fill, 0)
        lax.fori_loop(0, n_row_tiles, fill_done, 0)

    def issue(r, c):
        row_copy(r, pos_ref[0, 0, r]).start()
        row_copy(r, pos_ref[0, 1, r]).start()
        return c

    def drain(r, c):
        row_copy(0, 0).wait()
        row_copy(0, 0).wait()
        return c

    lax.fori_loop(0, tb, issue, 0, unroll=8)
    lax.fori_loop(0, tb, drain, 0, unroll=8)


def _dispatch(zmask, pos, hn, s_max):
    n, d = hn.shape
    n_tiles, _, tb = pos.shape
    return pl.pallas_call(
        functools.partial(_dispatch_kernel, tb=tb, n_row_tiles=zmask.shape[0]),
        grid_spec=pltpu.PrefetchScalarGridSpec(
            num_scalar_prefetch=1,
            grid=(n_tiles,),
            in_specs=[pl.BlockSpec((1, 2, tb), lambda i, zm: (i, 0, 0), memory_space=pltpu.SMEM),
                      pl.BlockSpec((tb, d), lambda i, zm: (i, 0))],
            out_specs=pl.BlockSpec(memory_space=pl.ANY),
            scratch_shapes=[pltpu.VMEM((GROUP_TILE, d), F32), pltpu.SemaphoreType.DMA(())]),
        out_shape=jax.ShapeDtypeStruct((s_max, d), F32),
        compiler_params=_params("arbitrary"),
        name="moe_dispatch",
    )(zmask, pos, hn)


def _group_ffn_kernel(te_ref, tx_ref, tv_ref, x_ref, w1_ref, w3_ref, w2_ref, o_ref, xb_scr, acc_scr):
    r, f = pl.program_id(0), pl.program_id(1)
    last = pl.num_programs(1) - 1

    @pl.when(tv_ref[r] != 0)
    def _():
        @pl.when(f == 0)
        def _():
            xb_scr[...] = x_ref[...].astype(BF16)

        xb = xb_scr[...]
        h1 = _dot(xb, w1_ref[0, 0])
        h3 = _dot(xb, w3_ref[0, 0])
        act = (h1 * jax.nn.sigmoid(h1) * h3).astype(BF16)
        y = _dot(act, w2_ref[0, 0])

        @pl.when(f == 0)
        def _():
            acc_scr[...] = y

        @pl.when(f != 0)
        def _():
            acc_scr[...] += y

        @pl.when(f == last)
        def _():
            o_ref[...] = acc_scr[...]

    @pl.when((tv_ref[r] == 0) & (f == last))
    def _():
        o_ref[...] = jnp.zeros_like(o_ref)


def _group_ffn(te, tx, tv, xs, w1, w3, w2, layer):
    s_max, d = xs.shape
    d_ff = w1.shape[3]
    tf = _tile(d_ff, 1792)
    nf = d_ff // tf
    fidx = lambda r, f, tv: f * tv[r] + (nf - 1) * (1 - tv[r])
    return pl.pallas_call(
        _group_ffn_kernel,
        grid_spec=pltpu.PrefetchScalarGridSpec(
            num_scalar_prefetch=3,
            grid=(s_max // GROUP_TILE, nf),
            in_specs=[pl.BlockSpec((GROUP_TILE, d), lambda r, f, te, tx, tv: (tx[r], 0)),
                      pl.BlockSpec((1, 1, d, tf), lambda r, f, te, tx, tv: (layer, te[r], 0, fidx(r, f, tv))),
                      pl.BlockSpec((1, 1, d, tf), lambda r, f, te, tx, tv: (layer, te[r], 0, fidx(r, f, tv))),
                      pl.BlockSpec((1, 1, tf, d), lambda r, f, te, tx, tv: (layer, te[r], fidx(r, f, tv), 0))],
            out_specs=pl.BlockSpec((GROUP_TILE, d), lambda r, f, te, tx, tv: (r, 0)),
            scratch_shapes=[pltpu.VMEM((GROUP_TILE, d), BF16), pltpu.VMEM((GROUP_TILE, d), F32)]),
        out_shape=jax.ShapeDtypeStruct((s_max, d), F32),
        compiler_params=_params("parallel", "arbitrary"),
        name="moe_group_swiglu",
    )(te, tx, tv, xs, w1, w3, w2)


def _combine_kernel(pos_ref, x_ref, mod_ref, info_ref, fg_ref, ys_ref, out_ref, buf, sem, *, tb, final_norm):
    def row_copy(slot, k, r):
        return pltpu.make_async_copy(ys_ref.at[pl.ds(slot, 1)], buf.at[k, pl.ds(r, 1)], sem)

    def issue(r, c):
        row_copy(pos_ref[0, 0, r], 0, r).start(priority=0)
        row_copy(pos_ref[0, 1, r], 1, r).start(priority=1)
        return c

    def drain(r, c):
        row_copy(0, 0, 0).wait()
        row_copy(0, 1, 0).wait()
        return c

    lax.fori_loop(0, tb, issue, 0, unroll=8)
    lax.fori_loop(0, tb, drain, 0, unroll=8)
    rec = info_ref[...]
    lane = lax.broadcasted_iota(jnp.int32, rec.shape, 1)
    y = _lane_pick(rec, lane, INFO_P1) * buf[0] + _lane_pick(rec, lane, INFO_P2) * buf[1]
    out = x_ref[...] + mod_ref[0, 5:6, :] * y
    if final_norm:
        out = out * lax.rsqrt(jnp.mean(out * out, axis=-1, keepdims=True) + EPS) * fg_ref[...]
    out_ref[...] = out


def _combine(pos, x, mod, info, ys, t_len, per_batch_mod, final_g):
    n, d = x.shape
    n_tiles, _, tb = pos.shape
    bidx = (lambda i: (i * tb) // t_len) if per_batch_mod else (lambda i: 0)
    final_norm = final_g is not None
    gain = (final_g if final_norm else jnp.ones((d,), F32)).reshape(1, d).astype(F32)
    return pl.pallas_call(
        functools.partial(_combine_kernel, tb=tb, final_norm=final_norm),
        grid=(n_tiles,),
        in_specs=[pl.BlockSpec((1, 2, tb), lambda i: (i, 0, 0), memory_space=pltpu.SMEM),
                  pl.BlockSpec((tb, d), lambda i: (i, 0)),
                  pl.BlockSpec((1, 6, d), lambda i: (bidx(i), 0, 0)),
                  pl.BlockSpec((tb, LANES), lambda i: (i, 0)),
                  pl.BlockSpec((1, d), lambda i: (0, 0)),
                  pl.BlockSpec(memory_space=pl.ANY)],
        out_specs=pl.BlockSpec((tb, d), lambda i: (i, 0)),
        out_shape=jax.ShapeDtypeStruct((n, d), F32),
        scratch_shapes=[pltpu.VMEM((2, tb, d), F32), pltpu.SemaphoreType.DMA(())],
        compiler_params=_params("arbitrary"),
        name="moe_combine",
    )(pos, x, mod, info, gain, ys)


def _moe(x, mod, ng, w_router, w1, w3, w2, layer, per_batch_mod, final_g=None):
    bsz, t_len, d = x.shape
    n = bsz * t_len
    n_exp = w_router.shape[1]
    hn, info, info_t, cnt = _router(x, mod, ng, w_router, per_batch_mod)
    info = info.reshape(n, LANES)
    e1, e2 = info_t[INFO_E1].astype(jnp.int32), info_t[INFO_E2].astype(jnp.int32)
    r1, r2 = info_t[INFO_R1].astype(jnp.int32), info_t[INFO_R2].astype(jnp.int32)
    counts = cnt[0, :n_exp].astype(jnp.int32)
    padded = ((counts + GROUP_TILE - 1) // GROUP_TILE) * GROUP_TILE
    ends = jnp.cumsum(padded)
    starts = ends - padded
    tb = _tile(n, ROUTE_TILE)
    pos = jnp.stack([(starts[e1] + r1).reshape(n // tb, tb), (starts[e2] + r2).reshape(n // tb, tb)], axis=1)
    s_max = 2 * n + n_exp * GROUP_TILE
    tile_row = jnp.arange(s_max // GROUP_TILE, dtype=jnp.int32) * GROUP_TILE
    tv = (tile_row < ends[-1]).astype(jnp.int32)
    te = jnp.minimum(jnp.searchsorted(ends, tile_row, side="right"), n_exp - 1).astype(jnp.int32)
    tx = (jnp.minimum(tile_row, ends[-1] - GROUP_TILE) // GROUP_TILE).astype(jnp.int32)
    region_end = ((tile_row + GROUP_TILE)[:, None] == ends[None, :]) & (padded > 0)[None, :]
    zmask = jnp.maximum(1 - tv, jnp.any(region_end, axis=1).astype(jnp.int32))
    xs = _dispatch(zmask, pos, hn.reshape(n, d), s_max)
    ys = _group_ffn(te, tx, tv, xs, w1.astype(BF16), w3.astype(BF16), w2.astype(BF16), layer)
    out = _combine(pos, x.reshape(n, d), mod, info, ys, t_len, per_batch_mod, final_g)
    return out.reshape(bsz, t_len, d)


def _dft_channel_kernel(x_ref, mod_ref, ng_ref, cs_ref, y_ref, *, gw):
    hn = _norm_mod(x_ref[0], ng_ref[...], mod_ref[0, 1:2, :], mod_ref[0, 0:1, :]).astype(BF16)
    for g in range(hn.shape[1] // gw):
        y = _dot(hn[:, g * gw:(g + 1) * gw], cs_ref[...])
        y_ref[0, 0, :, g * gw:(g + 1) * gw] = y[:, :gw].astype(BF16)
        y_ref[0, 1, :, g * gw:(g + 1) * gw] = y[:, gw:].astype(BF16)


REV_TILE = 128


def _reverse_shift(src_tiles, wrap_row, m1):
    n_t = len(src_tiles)
    first = lax.broadcasted_iota(jnp.int32, (REV_TILE, 1), 0) == 0
    out = []
    for a in range(n_t):
        body = _dot(m1, src_tiles[n_t - 1 - a])
        head = wrap_row if a == 0 else src_tiles[n_t - a][0:1, :]
        out.append(jnp.where(first, head.astype(F32), body))
    return out


def _dft_fold_kernel(ya_ref, yb_ref, yn_ref, m1_ref, f_ref):
    n_t = ya_ref.shape[2] // REV_TILE
    keep = jnp.where(pl.program_id(1) == 0, 0.0, 1.0)
    for plane, sign in ((0, 1.0), (1, -1.0)):
        tiles = [yb_ref[0, plane, a * REV_TILE:(a + 1) * REV_TILE, :] for a in range(n_t)]
        wrap = yn_ref[0, plane, 0:1, :].astype(F32) * keep
        rev = _reverse_shift(tiles, wrap, m1_ref[...])
        for a in range(n_t):
            rows = slice(a * REV_TILE, (a + 1) * REV_TILE)
            f_ref[0, plane, rows, :] = (ya_ref[0, plane, rows, :].astype(F32) + sign * rev[a]).astype(BF16)


def _dft_time_kernel(wc_ref, ws_ref, wcx_ref, wsx_ref, f_ref, yh_ref, xlo_ref, xhi_ref, mod_ref, fw_ref, fb_ref,
                     m1_ref, lo_ref, hi_ref):
    rk = wc_ref.shape[0]
    ec, od = f_ref[0, 0], f_ref[0, 1]
    y_half = yh_ref[0, 0, 0:1, :].astype(F32) * ((2 * f_ref.shape[2]) ** -0.5)
    parity = lax.broadcasted_iota(jnp.int32, (rk, 1), 0) % 2
    p = _dot(wc_ref[...], ec) + jnp.where(parity == 0, 1.0, -1.0) * y_half
    q = _dot(ws_ref[...], od)
    gate = mod_ref[0, 2:3, :]

    def project(z, x):
        return x + gate * (_dot(z, fw_ref[...]) + fb_ref[...])

    lo_ref[0] = project((p - q).astype(BF16), xlo_ref[0])
    px = _dot(wcx_ref[...], ec)[0:1, :] + y_half
    qx = _dot(wsx_ref[...], od)[0:1, :]
    src = (p + q).astype(BF16)
    tiles = [src[a * REV_TILE:(a + 1) * REV_TILE, :] for a in range(rk // REV_TILE)]
    rev = _reverse_shift(tiles, (px + qx).astype(BF16), m1_ref[...])
    hi_ref[0] = project(jnp.concatenate(rev, axis=0).astype(BF16), xhi_ref[0])


def _dft_matrix(n, scale):
    idx = (np.arange(n)[:, None] * np.arange(n)[None, :]) % n
    ang = 2.0 * np.pi * idx.astype(np.float64) / n
    return np.cos(ang) * scale, np.sin(ang) * scale


def _fourier_mixer(x, mod, ng, fn_w, fn_b, per_batch_mod):
    bsz, t_len, d = x.shape
    gw = d // N_GROUPS
    cg, sg = _dft_matrix(gw, gw ** -0.5)
    cs = jnp.asarray(np.concatenate([cg, sg], axis=1).astype(np.float32)).astype(BF16)
    tm = _tile(t_len, 512)
    bidx = (lambda b: b) if per_batch_mod else (lambda b: 0)
    y = pl.pallas_call(
        functools.partial(_dft_channel_kernel, gw=gw),
        grid=(bsz, t_len // tm),
        in_specs=[pl.BlockSpec((1, tm, d), lambda b, i: (b, i, 0)),
                  pl.BlockSpec((1, 6, d), lambda b, i: (bidx(b), 0, 0)),
                  pl.BlockSpec((1, d), lambda b, i: (0, 0)),
                  pl.BlockSpec((gw, 2 * gw), lambda b, i: (0, 0))],
        out_specs=pl.BlockSpec((1, 2, tm, d), lambda b, i: (b, 0, i, 0)),
        out_shape=jax.ShapeDtypeStruct((bsz, 2, t_len, d), BF16),
        compiler_params=_params("parallel", "parallel"),
        name="dft_channel",
    )(x, mod, ng.reshape(1, d), cs)
    half = t_len // 2
    assert half % REV_TILE == 0 and half % DFT_SPLIT == 0
    pos = np.arange(REV_TILE)
    m1 = jnp.asarray((pos[None, :] == REV_TILE - pos[:, None]).astype(np.float32)).astype(BF16)
    rf = _tile(half, 512)
    nb_f = t_len // rf
    sub = 16
    folded = pl.pallas_call(
        _dft_fold_kernel,
        grid=(bsz, half // rf),
        in_specs=[pl.BlockSpec((1, 2, rf, d), lambda b, i: (b, 0, i, 0)),
                  pl.BlockSpec((1, 2, rf, d), lambda b, i: (b, 0, nb_f - 1 - i, 0)),
                  pl.BlockSpec((1, 2, sub, d), lambda b, i: (b, 0, ((nb_f - i) % nb_f) * (rf // sub), 0)),
                  pl.BlockSpec((REV_TILE, REV_TILE), lambda b, i: (0, 0))],
        out_specs=pl.BlockSpec((1, 2, rf, d), lambda b, i: (b, 0, i, 0)),
        out_shape=jax.ShapeDtypeStruct((bsz, 2, half, d), BF16),
        compiler_params=_params("parallel", "parallel"),
        name="dft_fold",
    )(y, y, y, m1)
    kk = np.arange(half + sub)[:, None]
    ang_hi = 2.0 * np.pi * ((kk * DFT_SPLIT * np.arange(half // DFT_SPLIT)[None, :]) % t_len) / t_len
    ang_lo = 2.0 * np.pi * ((kk * np.arange(DFT_SPLIT)[None, :]) % t_len) / t_len
    scale = t_len ** -0.5
    c_hi, s_hi = (jnp.asarray((f(ang_hi) * scale).astype(np.float32))[:, :, None] for f in (np.cos, np.sin))
    c_lo, s_lo = (jnp.asarray(f(ang_lo).astype(np.float32))[:, None, :] for f in (np.cos, np.sin))
    wc = (c_hi * c_lo - s_hi * s_lo).reshape(half + sub, half).astype(BF16)
    ws = (s_hi * c_lo + c_hi * s_lo).reshape(half + sub, half).astype(BF16)
    rk = _tile(half, 512)
    nk = half // rk
    lo, hi = pl.pallas_call(
        _dft_time_kernel,
        grid=(bsz, nk),
        in_specs=[pl.BlockSpec((rk, half), lambda b, i: (i, 0)),
                  pl.BlockSpec((rk, half), lambda b, i: (i, 0)),
                  pl.BlockSpec((sub, half), lambda b, i: ((i + 1) * (rk // sub), 0)),
                  pl.BlockSpec((sub, half), lambda b, i: ((i + 1) * (rk // sub), 0)),
                  pl.BlockSpec((1, 2, half, d), lambda b, i: (b, 0, 0, 0)),
                  pl.BlockSpec((1, 1, sub, d), lambda b, i: (b, 0, half // sub, 0)),
                  pl.BlockSpec((1, rk, d), lambda b, i: (b, i, 0)),
                  pl.BlockSpec((1, rk, d), lambda b, i: (b, 2 * nk - 1 - i, 0)),
                  pl.BlockSpec((1, 6, d), lambda b, i: (bidx(b), 0, 0)),
                  pl.BlockSpec((d, d), lambda b, i: (0, 0)),
                  pl.BlockSpec((1, d), lambda b, i: (0, 0)),
                  pl.BlockSpec((REV_TILE, REV_TILE), lambda b, i: (0, 0))],
        out_specs=(pl.BlockSpec((1, rk, d), lambda b, i: (b, i, 0)),
                   pl.BlockSpec((1, rk, d), lambda b, i: (b, nk - 1 - i, 0))),
        out_shape=(jax.ShapeDtypeStruct((bsz, half, d), F32), jax.ShapeDtypeStruct((bsz, half, d), F32)),
        compiler_params=_params("parallel", "parallel"),
        name="dft_time",
    )(wc, ws, wc, ws, folded, y, x, x, mod, fn_w.astype(BF16), fn_b.reshape(1, d).astype(F32), m1)
    return jnp.concatenate([lo, hi], axis=1)


def _final_norm_kernel(x_ref, g_ref, o_ref):
    x = x_ref[0]
    o_ref[0] = x * lax.rsqrt(jnp.mean(x * x, axis=-1, keepdims=True) + EPS) * g_ref[...]


def _final_norm(x, g):
    bsz, t_len, d = x.shape
    tm = _tile(t_len, 1024)
    return pl.pallas_call(
        _final_norm_kernel,
        grid=(bsz, t_len // tm),
        in_specs=[pl.BlockSpec((1, tm, d), lambda b, i: (b, i, 0)), pl.BlockSpec((1, d), lambda b, i: (0, 0))],
        out_specs=pl.BlockSpec((1, tm, d), lambda b, i: (b, i, 0)),
        out_shape=jax.ShapeDtypeStruct((bsz, t_len, d), F32),
        compiler_params=_params("parallel", "parallel"),
        name="final_norm",
    )(x, g.reshape(1, d))


def _trunk(x, mod, cache, use_rope, per_batch_mod, p):
    bsz, t_len, d = x.shape
    depth = mod.shape[0]
    flat = (lambda a: a) if per_batch_mod else (lambda a: a.reshape(1, bsz * t_len, a.shape[-1]))
    unflat = (lambda a: a) if per_batch_mod else (lambda a: a.reshape(bsz, t_len, a.shape[-1]))
    states = []
    for i in range(depth):
        j = i // 2
        m_i = mod[i]
        if i % 2 == 0:
            q, kt, v, o, r = _mlstm_inproj(x, m_i, p["norm_g"][i, 0], p["ml_w_in"][j], p["ml_b_gate"][j],
                                           use_rope, per_batch_mod)
            outs = []
            for direction in range(2):
                if cache is None:
                    cn0 = jnp.zeros((bsz, N_HEADS // 2, 2 * HEAD_DK, 2 * HEAD_DV), F32)
                    m0 = jnp.zeros((bsz, N_HEADS, LANES), F32)
                else:
                    cn0, m0 = _pack_state(cache[0][:, j, direction], cache[1][:, j, direction],
                                          cache[2][:, j, direction])
                outs.append(_mlstm_scan(q, kt, v, r, cn0, m0, reverse=bool(direction)))
            states.append([_unpack_state(cn, m) for (_, cn, m) in outs])
            x = _mlstm_out(outs[0][0], outs[1][0], o, x, m_i, p["ml_head_g"][j], p["ml_w_out"][j], per_batch_mod)
            x = unflat(_ffn(flat(x), m_i, p["norm_g"][i, 1], p["ffn_w1"][j], p["ffn_w3"][j], p["ffn_w2"][j],
                            per_batch_mod))
        else:
            x = _fourier_mixer(x, m_i, p["norm_g"][i, 0], p["fn_w"][j], p["fn_b"][j], per_batch_mod)
            closing = p["final_g"] if i == depth - 1 else None
            x = unflat(_moe(flat(x), m_i, p["norm_g"][i, 1], p["moe_router"][j], p["moe_w1"], p["moe_w3"],
                            p["moe_w2"], j, per_batch_mod, closing))
    if depth % 2 == 1:
        x = unflat(_final_norm(flat(x), p["final_g"]))
    return x, states


def kernel(x_prompt, x_sample, state_C, state_n, state_m, c, c_ctx, w_mod, b_mod, norm_g, final_g,
           ml_w_in, ml_b_gate, ml_head_g, ml_w_out, fn_w, fn_b, ffn_w1, ffn_w3, ffn_w2,
           moe_router, moe_w1, moe_w3, moe_w2):
    p = dict(norm_g=norm_g, final_g=final_g, ml_w_in=ml_w_in, ml_b_gate=ml_b_gate, ml_head_g=ml_head_g,
             ml_w_out=ml_w_out, fn_w=fn_w, fn_b=fn_b, ffn_w1=ffn_w1, ffn_w3=ffn_w3, ffn_w2=ffn_w2,
             moe_router=moe_router, moe_w1=moe_w1, moe_w3=moe_w3, moe_w2=moe_w2)
    depth, d = w_mod.shape[0], w_mod.shape[1]
    n_dec = c.shape[0]
    rows = ((n_dec + 1 + 7) // 8) * 8
    cond = jnp.zeros((rows, d), F32).at[:n_dec].set(c).at[n_dec].set(c_ctx)
    mod = _mod_table(cond, w_mod, b_mod).reshape(depth, rows, 6, d)
    y_prompt, st = _trunk(x_prompt, mod[:, n_dec:n_dec + 1], None, False, False, p)
    y_sample, _ = _trunk(x_sample, mod[:, :n_dec], (state_C, state_n, state_m), True, True, p)
    new_c = jnp.stack([jnp.stack([s[0][0], s[1][0]], axis=1) for s in st], axis=1)
    new_n = jnp.stack([jnp.stack([s[0][1], s[1][1]], axis=1) for s in st], axis=1)
    new_m = jnp.stack([jnp.stack([s[0][2], s[1][2]], axis=1) for s in st], axis=1)
    return (y_prompt, y_sample, new_c.astype(x_prompt.dtype), new_n.astype(x_prompt.dtype),
            new_m.astype(x_prompt.dtype))
```

```python
import functools

import numpy as np
import jax
import jax.numpy as jnp
from jax import lax
from jax.experimental import pallas as pl
from jax.experimental.pallas import tpu as pltpu

F32 = jnp.float32
BF16 = jnp.bfloat16
HIGHEST = lax.Precision.HIGHEST

EPS = 1e-6
N_HEADS = 8
HEAD_DK = 64
HEAD_DV = 128
QK_W = N_HEADS * HEAD_DK
V_W = N_HEADS * HEAD_DV
GRID_W = 64
ROPE_BASE = 10000.0
N_GROUPS = 4
DFT_SPLIT = 64
N_GATES = 4 * N_HEADS
LANES = 128
SCAN_CHUNK = 128
SCAN_ROWS = 3 * N_HEADS
VMEM_LIMIT = 56 * 1024 * 1024
NEG_INF = float("-inf")


def _params(*sem):
    return pltpu.CompilerParams(dimension_semantics=sem, vmem_limit_bytes=VMEM_LIMIT)


def _tile(n, pref):
    t = min(n, pref)
    assert n % t == 0, (n, pref)
    return t


def _norm_mod(x, gain, scale, shift):
    ms = jnp.mean(x * x, axis=-1, keepdims=True)
    return x * lax.rsqrt(ms + EPS) * gain * (1.0 + scale) + shift


def _dot(a, b):
    return jnp.dot(a, b, preferred_element_type=F32)


def _dot_nt(a, b):
    return lax.dot_general(a, b, (((1,), (1,)), ((), ())), preferred_element_type=F32)


def _log_sigmoid(x):
    return jnp.minimum(x, 0.0) - jnp.log1p(jnp.exp(-jnp.abs(x)))


def _mod_kernel(c_ref, w_ref, b_ref, o_ref):
    c = c_ref[...]
    s = c * jax.nn.sigmoid(c)
    o_ref[0] = jnp.dot(s, w_ref[0], preferred_element_type=F32, precision=HIGHEST) + b_ref[0]


def _mod_table(cond, w_mod, b_mod):
    depth, d, n = w_mod.shape
    rows = cond.shape[0]
    tn = _tile(n, 1536)
    return pl.pallas_call(
        _mod_kernel,
        grid=(depth, n // tn),
        in_specs=[pl.BlockSpec((rows, d), lambda l, j: (0, 0)),
                  pl.BlockSpec((1, d, tn), lambda l, j: (l, 0, j)),
                  pl.BlockSpec((1, 1, tn), lambda l, j: (l, 0, j))],
        out_specs=pl.BlockSpec((1, rows, tn), lambda l, j: (l, 0, j)),
        out_shape=jax.ShapeDtypeStruct((depth, rows, n), F32),
        compiler_params=_params("parallel", "parallel"),
        name="adaln_table",
    )(cond, w_mod, b_mod.reshape(depth, 1, n))


def _rope_tables(t_len):
    pos = np.arange(t_len)
    row = (pos // GRID_W).astype(np.float32)
    col = (pos % GRID_W).astype(np.float32)
    nf = HEAD_DK // 4
    inv = (np.float32(ROPE_BASE) ** (-np.arange(nf, dtype=np.float32) / nf)).astype(np.float32)
    d = np.arange(HEAD_DK)
    p = np.where(d[None, :] < HEAD_DK // 2, row[:, None], col[:, None]).astype(np.float32)
    ang = p * inv[d % nf][None, :]
    sign = np.where((d % (2 * nf)) < nf, -1.0, 1.0).astype(np.float32)
    return np.cos(ang).astype(np.float32), (np.sin(ang) * sign[None, :]).astype(np.float32)


def _inproj_kernel(*refs, use_rope):
    if use_rope:
        (x_ref, mod_ref, ng_ref, wq_ref, wkt_ref, wv_ref, wo_ref, wgt_ref, bg_ref, trif_ref, trib_ref,
         cq_ref, sq_ref, ck_ref, sk_ref, q_ref, kt_ref, v_ref, o_ref, r_ref) = refs
    else:
        (x_ref, mod_ref, ng_ref, wq_ref, wkt_ref, wv_ref, wo_ref, wgt_ref, bg_ref, trif_ref, trib_ref,
         q_ref, kt_ref, v_ref, o_ref, r_ref) = refs
    hn = _norm_mod(x_ref[0], ng_ref[...], mod_ref[0, 1:2, :], mod_ref[0, 0:1, :])
    hb = hn.astype(BF16)
    q = _dot(hb, wq_ref[...]) * (HEAD_DK ** -0.5)
    kt = _dot_nt(wkt_ref[...], hb)
    if use_rope:
        nf = HEAD_DK // 4
        lane = lax.broadcasted_iota(jnp.int32, (q.shape[0], LANES), 1)
        first_q = (lane % (2 * nf)) < nf
        sub = lax.broadcasted_iota(jnp.int32, (LANES, kt.shape[1]), 0)
        first_k = (sub % (2 * nf)) < nf
        cq, sq, ck, sk = cq_ref[...], sq_ref[...], ck_ref[...], sk_ref[...]
        for s in range(QK_W // LANES):
            qs = q[:, s * LANES:(s + 1) * LANES]
            sw = jnp.where(first_q, pltpu.roll(qs, LANES - nf, 1), pltpu.roll(qs, nf, 1))
            q_ref[0, :, s * LANES:(s + 1) * LANES] = (qs * cq + sw * sq).astype(BF16)
            ks = kt[s * LANES:(s + 1) * LANES, :]
            sw = jnp.where(first_k, pltpu.roll(ks, LANES - nf, 0), pltpu.roll(ks, nf, 0))
            kt_ref[0, s * LANES:(s + 1) * LANES, :] = (ks * ck + sw * sk).astype(BF16)
    else:
        q_ref[0] = q.astype(BF16)
        kt_ref[0] = kt.astype(BF16)
    v_ref[0] = _dot(hb, wv_ref[...]).astype(BF16)
    o_ref[0] = _dot(hb, wo_ref[...]).astype(BF16)
    gt = _dot_nt(wgt_ref[...], hb) + bg_ref[...]
    h = N_HEADS
    i_f, f_f = gt[0:h], _log_sigmoid(gt[h:2 * h])
    i_b, f_b = gt[2 * h:3 * h], _log_sigmoid(gt[3 * h:4 * h])
    b_f = jnp.dot(f_f, trif_ref[...], preferred_element_type=F32, precision=HIGHEST)
    b_b = jnp.dot(f_b, trib_ref[...], preferred_element_type=F32, precision=HIGHEST)
    a_f, a_b = i_f - b_f, i_b - b_b
    width = a_f.shape[1]
    in_chunk = lax.broadcasted_iota(jnp.int32, a_f.shape, 1) % SCAN_CHUNK
    c_f, c_b = a_f, a_b
    step = 1
    while step < SCAN_CHUNK:
        c_f = jnp.where(in_chunk >= step, jnp.maximum(c_f, pltpu.roll(c_f, step, 1)), c_f)
        c_b = jnp.where(in_chunk < SCAN_CHUNK - step, jnp.maximum(c_b, pltpu.roll(c_b, width - step, 1)), c_b)
        step *= 2
    for k, rows in enumerate((a_f, b_f, c_f, a_b, b_b, c_b)):
        r_ref[0, k * h:(k + 1) * h, :] = rows


def _mlstm_inproj(x, mod, ng, w_in, b_gate, use_rope, per_batch_mod):
    bsz, t_len, d = x.shape
    tm = _tile(t_len, 512)
    wq = w_in[:, :QK_W].astype(BF16)
    wkt = w_in[:, QK_W:2 * QK_W].T.astype(BF16)
    wv = w_in[:, 2 * QK_W:2 * QK_W + V_W].astype(BF16)
    wo = w_in[:, 2 * QK_W + V_W:2 * QK_W + 2 * V_W].astype(BF16)
    wgt = w_in[:, 2 * QK_W + 2 * V_W:].T.astype(BF16)
    bg = b_gate.reshape(N_GATES, 1).astype(F32)
    pos = np.arange(tm)
    same = (pos[:, None] // SCAN_CHUNK) == (pos[None, :] // SCAN_CHUNK)
    trif = jnp.asarray((same & (pos[:, None] <= pos[None, :])).astype(np.float32))
    trib = jnp.asarray((same & (pos[:, None] >= pos[None, :])).astype(np.float32))
    bidx = (lambda b: b) if per_batch_mod else (lambda b: 0)
    const = lambda shp: pl.BlockSpec(shp, lambda b, i: (0,) * len(shp))
    in_specs = [pl.BlockSpec((1, tm, d), lambda b, i: (b, i, 0)),
                pl.BlockSpec((1, 6, d), lambda b, i: (bidx(b), 0, 0)),
                const((1, d)), const((d, QK_W)), const((QK_W, d)), const((d, V_W)), const((d, V_W)),
                const((N_GATES, d)), const((N_GATES, 1)), const((tm, tm)), const((tm, tm))]
    args = [x, mod, ng.reshape(1, d), wq, wkt, wv, wo, wgt, bg, trif, trib]
    if use_rope:
        cos, sin = _rope_tables(t_len)
        rep = LANES // HEAD_DK
        args += [jnp.asarray(np.tile(cos, (1, rep))), jnp.asarray(np.tile(sin, (1, rep))),
                 jnp.asarray(np.tile(cos.T, (rep, 1))), jnp.asarray(np.tile(sin.T, (rep, 1)))]
        in_specs += [pl.BlockSpec((tm, LANES), lambda b, i: (i, 0)), pl.BlockSpec((tm, LANES), lambda b, i: (i, 0)),
                     pl.BlockSpec((LANES, tm), lambda b, i: (0, i)), pl.BlockSpec((LANES, tm), lambda b, i: (0, i))]
    out_shape = (jax.ShapeDtypeStruct((bsz, t_len, QK_W), BF16),
                 jax.ShapeDtypeStruct((bsz, QK_W, t_len), BF16),
                 jax.ShapeDtypeStruct((bsz, t_len, V_W), BF16),
                 jax.ShapeDtypeStruct((bsz, t_len, V_W), BF16),
                 jax.ShapeDtypeStruct((bsz, 2 * SCAN_ROWS, t_len), F32))
    out_specs = (pl.BlockSpec((1, tm, QK_W), lambda b, i: (b, i, 0)),
                 pl.BlockSpec((1, QK_W, tm), lambda b, i: (b, 0, i)),
                 pl.BlockSpec((1, tm, V_W), lambda b, i: (b, i, 0)),
                 pl.BlockSpec((1, tm, V_W), lambda b, i: (b, i, 0)),
                 pl.BlockSpec((1, 2 * SCAN_ROWS, tm), lambda b, i: (b, 0, i)))
    return pl.pallas_call(
        functools.partial(_inproj_kernel, use_rope=use_rope),
        grid=(bsz, t_len // tm), in_specs=in_specs, out_specs=out_specs, out_shape=out_shape,
        compiler_params=_params("parallel", "parallel"),
        name="mlstm_inproj",
    )(*args)


def _scan_kernel(q_ref, kt_ref, v_ref, r_ref, cn0_ref, m0_ref, spread_ref, h_ref, cn_out_ref, m_out_ref,
                 cn_scr, m_scr, *, reverse, n_chunks):
    L = SCAN_CHUNK
    j = pl.program_id(1)

    @pl.when(j == 0)
    def _():
        cn_scr[...] = cn0_ref[0]
        m_scr[...] = m0_ref[0]

    row_i = lax.broadcasted_iota(jnp.int32, (L, L), 0)
    col_i = lax.broadcasted_iota(jnp.int32, (L, L), 1)
    visible = (col_i >= row_i) if reverse else (col_i <= row_i)
    upper_lanes = lax.broadcasted_iota(jnp.int32, (L, LANES), 1) >= HEAD_DK
    ones = jnp.ones((L, LANES), BF16)
    no_rows = jnp.zeros((HEAD_DK, 2 * HEAD_DV), BF16)
    end_lane = lax.broadcasted_iota(jnp.int32, (1, L), 1) == (0 if reverse else L - 1)
    order = range(n_chunks - 1, -1, -1) if reverse else range(n_chunks)
    head_rows = [slice((h % 2) * HEAD_DK, (h % 2 + 1) * HEAD_DK) for h in range(N_HEADS)]
    states = [(cn_scr[h // 2, head_rows[h], :], m_scr[h:h + 1, 0:1]) for h in range(N_HEADS)]

    def columns(c):
        x = r_ref[0, N_HEADS:3 * N_HEADS, c * L:(c + 1) * L]
        hi = x.astype(BF16).astype(F32)
        mid = (x - hi).astype(BF16).astype(F32)
        lo = x - hi - mid
        parts = jnp.concatenate([hi, mid, lo, jnp.zeros((L - 6 * N_HEADS, L), F32)], axis=0)
        return _dot(parts.T.astype(BF16), spread_ref[...])

    cols = {c: columns(c) for c in order}
    new_states = []
    for h in range(N_HEADS):
        p, e = h // 2, h % 2
        cn, m = states[h]
        for c in order:
            r0 = c * L
            q_pair = q_ref[0, r0:r0 + L, p * LANES:(p + 1) * LANES]
            q_m = jnp.where(upper_lanes if e else jnp.logical_not(upper_lanes), q_pair, jnp.zeros_like(q_pair))
            kt_pair = kt_ref[0, p * LANES:(p + 1) * LANES, r0:r0 + L]
            kt_h = kt_ref[0, h * HEAD_DK:(h + 1) * HEAD_DK, r0:r0 + L]
            a_row = r_ref[0, h:h + 1, r0:r0 + L]
            b_row = r_ref[0, N_HEADS + h:N_HEADS + h + 1, r0:r0 + L]
            v_aug = jnp.concatenate([v_ref[0, r0:r0 + L, h * HEAD_DV:(h + 1) * HEAD_DV], ones], axis=1)
            am = jnp.where(visible, a_row, NEG_INF)
            a_max = jnp.max(a_row, axis=1, keepdims=True)
            kw = (kt_h.astype(F32) * jnp.exp(a_row - a_max)).astype(BF16)
            upd = _dot(kw, v_aug)
            b_col = cols[c][:, h * LANES:(h + 1) * LANES]
            b_end = jnp.sum(jnp.where(end_lane, b_row, 0.0), axis=1, keepdims=True)
            u = jnp.maximum(m, cols[c][:, (N_HEADS + h) * LANES:(N_HEADS + h + 1) * LANES])
            guard = jnp.exp(-(b_col + u))
            s = (_dot(q_m, kt_pair) * jnp.exp(am - u)).astype(BF16)
            q_old = (q_m.astype(F32) * jnp.exp(m - u)).astype(BF16)
            cn_b = cn.astype(BF16)
            cn_pair = jnp.concatenate([no_rows, cn_b] if e else [cn_b, no_rows], axis=0)
            both = _dot(jnp.concatenate([q_old, s], axis=1), jnp.concatenate([cn_pair, v_aug], axis=0))
            num, den = both[:, :HEAD_DV], both[:, HEAD_DV:]
            h_ref[0, r0:r0 + L, h * HEAD_DV:(h + 1) * HEAD_DV] = (num / jnp.maximum(jnp.abs(den), guard)).astype(BF16)
            u_end = jnp.maximum(m, a_max)
            cn = jnp.exp(m - u_end) * cn + jnp.exp(a_max - u_end) * upd
            m = b_end + u_end
        new_states.append((cn, m))
    for h in range(N_HEADS):
        cn_scr[h // 2, head_rows[h], :] = new_states[h][0]
        m_scr[h:h + 1, :] = jnp.broadcast_to(new_states[h][1], (1, LANES))

    @pl.when(j == pl.num_programs(1) - 1)
    def _():
        cn_out_ref[0] = cn_scr[...]
        m_out_ref[0] = m_scr[...]


def _mlstm_scan(q, kt, v, r, cn0, m0, reverse):
    bsz, t_len, _ = q.shape
    tb = _tile(t_len, 4 * SCAN_CHUNK)
    nblk = t_len // tb
    blk = (lambda j: nblk - 1 - j) if reverse else (lambda j: j)
    d = 1 if reverse else 0
    half = N_HEADS // 2
    src = np.arange(LANES)[:, None]
    dst = np.arange(2 * N_HEADS * LANES)[None, :]
    spread = jnp.asarray(((src < 6 * N_HEADS) & (src % (2 * N_HEADS) == dst // LANES)).astype(np.float32)).astype(BF16)
    return pl.pallas_call(
        functools.partial(_scan_kernel, reverse=reverse, n_chunks=tb // SCAN_CHUNK),
        grid=(bsz, nblk),
        in_specs=[pl.BlockSpec((1, tb, QK_W), lambda b, j: (b, blk(j), 0)),
                  pl.BlockSpec((1, QK_W, tb), lambda b, j: (b, 0, blk(j))),
                  pl.BlockSpec((1, tb, V_W), lambda b, j: (b, blk(j), 0)),
                  pl.BlockSpec((1, SCAN_ROWS, tb), lambda b, j: (b, d, blk(j))),
                  pl.BlockSpec((1, half, 2 * HEAD_DK, 2 * HEAD_DV), lambda b, j: (b, 0, 0, 0)),
                  pl.BlockSpec((1, N_HEADS, LANES), lambda b, j: (b, 0, 0)),
                  pl.BlockSpec((LANES, 2 * N_HEADS * LANES), lambda b, j: (0, 0))],
        out_specs=(pl.BlockSpec((1, tb, V_W), lambda b, j: (b, blk(j), 0)),
                   pl.BlockSpec((1, half, 2 * HEAD_DK, 2 * HEAD_DV), lambda b, j: (b, 0, 0, 0)),
                   pl.BlockSpec((1, N_HEADS, LANES), lambda b, j: (b, 0, 0))),
        out_shape=(jax.ShapeDtypeStruct((bsz, t_len, V_W), BF16),
                   jax.ShapeDtypeStruct((bsz, half, 2 * HEAD_DK, 2 * HEAD_DV), F32),
                   jax.ShapeDtypeStruct((bsz, N_HEADS, LANES), F32)),
        scratch_shapes=[pltpu.VMEM((half, 2 * HEAD_DK, 2 * HEAD_DV), F32),
                        pltpu.VMEM((N_HEADS, LANES), F32)],
        compiler_params=_params("parallel", "arbitrary"),
        name="mlstm_scan_bwd" if reverse else "mlstm_scan_fwd",
    )(q, kt, v, r, cn0, m0, spread)


def _pack_state(c, n, m):
    bsz = c.shape[0]
    cn = jnp.concatenate([c, jnp.broadcast_to(n[..., None], n.shape + (HEAD_DV,))], axis=-1)
    cn = cn.reshape(bsz, N_HEADS // 2, 2 * HEAD_DK, 2 * HEAD_DV)
    return cn.astype(F32), jnp.broadcast_to(m[..., None], (bsz, N_HEADS, LANES)).astype(F32)


def _unpack_state(cn, m):
    bsz = cn.shape[0]
    cn = cn.reshape(bsz, N_HEADS, HEAD_DK, 2 * HEAD_DV)
    return cn[..., :HEAD_DV], cn[..., HEAD_DV], m[..., 0]


def _mlstm_out_kernel(hf_ref, hb_ref, o_ref, x_ref, mod_ref, hg_ref, w_ref, out_ref):
    hs = hf_ref[0].astype(F32) + hb_ref[0].astype(F32)
    parts = []
    for h in range(N_HEADS):
        z = hs[:, h * HEAD_DV:(h + 1) * HEAD_DV]
        parts.append(z * lax.rsqrt(jnp.mean(z * z, axis=-1, keepdims=True) + EPS))
    hn = jnp.concatenate(parts, axis=1) * hg_ref[...] * jax.nn.sigmoid(o_ref[0].astype(F32))
    y = _dot(hn.astype(BF16), w_ref[...])
    out_ref[0] = x_ref[0] + mod_ref[0, 2:3, :] * y


def _mlstm_out(hf, hb, o, x, mod, head_g, w_out, per_batch_mod):
    bsz, t_len, d = x.shape
    tm = _tile(t_len, 512)
    bidx = (lambda b: b) if per_batch_mod else (lambda b: 0)
    tok = lambda w: pl.BlockSpec((1, tm, w), lambda b, i: (b, i, 0))
    return pl.pallas_call(
        _mlstm_out_kernel,
        grid=(bsz, t_len // tm),
        in_specs=[tok(V_W), tok(V_W), tok(V_W), tok(d),
                  pl.BlockSpec((1, 6, d), lambda b, i: (bidx(b), 0, 0)),
                  pl.BlockSpec((1, V_W), lambda b, i: (0, 0)),
                  pl.BlockSpec((V_W, d), lambda b, i: (0, 0))],
        out_specs=tok(d),
        out_shape=jax.ShapeDtypeStruct((bsz, t_len, d), F32),
        compiler_params=_params("parallel", "parallel"),
        name="mlstm_out",
    )(hf, hb, o, x, mod, head_g.reshape(1, V_W).astype(F32), w_out.astype(BF16))


def _ffn_kernel(x_ref, mod_ref, ng_ref, w1_ref, w3_ref, w2_ref, out_ref, hn_scr, acc_scr):
    f = pl.program_id(2)

    @pl.when(f == 0)
    def _():
        hn = _norm_mod(x_ref[0], ng_ref[...], mod_ref[0, 4:5, :], mod_ref[0, 3:4, :])
        hn_scr[...] = hn.astype(BF16)
        acc_scr[...] = jnp.zeros_like(acc_scr)

    hb = hn_scr[...]
    h1 = _dot(hb, w1_ref[...])
    h3 = _dot(hb, w3_ref[...])
    act = (h1 * jax.nn.sigmoid(h1) * h3).astype(BF16)
    acc_scr[...] += _dot(act, w2_ref[...])

    @pl.when(f == pl.num_programs(2) - 1)
    def _():
        out_ref[0] = x_ref[0] + mod_ref[0, 5:6, :] * acc_scr[...]


def _ffn(x, mod, ng, w1, w3, w2, per_batch_mod):
    bsz, t_len, d = x.shape
    d_ff = w1.shape[1]
    tm = _tile(t_len, 1024)
    tf = _tile(d_ff, 512)
    bidx = (lambda b: b) if per_batch_mod else (lambda b: 0)
    return pl.pallas_call(
        _ffn_kernel,
        grid=(bsz, t_len // tm, d_ff // tf),
        in_specs=[pl.BlockSpec((1, tm, d), lambda b, i, f: (b, i, 0)),
                  pl.BlockSpec((1, 6, d), lambda b, i, f: (bidx(b), 0, 0)),
                  pl.BlockSpec((1, d), lambda b, i, f: (0, 0)),
                  pl.BlockSpec((d, tf), lambda b, i, f: (0, f)),
                  pl.BlockSpec((d, tf), lambda b, i, f: (0, f)),
                  pl.BlockSpec((tf, d), lambda b, i, f: (f, 0))],
        out_specs=pl.BlockSpec((1, tm, d), lambda b, i, f: (b, i, 0)),
        out_shape=jax.ShapeDtypeStruct((bsz, t_len, d), F32),
        scratch_shapes=[pltpu.VMEM((tm, d), BF16), pltpu.VMEM((tm, d), F32)],
        compiler_params=_params("parallel", "parallel", "arbitrary"),
        name="dense_swiglu",
    )(x, mod, ng.reshape(1, d), w1.astype(BF16), w3.astype(BF16), w2.astype(BF16))


ROUTE_TILE = 512
GROUP_TILE = 512
INFO_E1, INFO_E2, INFO_P1, INFO_P2, INFO_R1, INFO_R2 = range(6)
INFO_ROWS = 8


def _lane_pick(rec, lane, k):
    return jnp.sum(jnp.where(lane == k, rec, 0.0), axis=-1, keepdims=True)


def _router_kernel(x_ref, mod_ref, ng_ref, wr_ref, tri_ref, hn_ref, info_ref, infot_ref, cnt_ref, carry, *, n_exp):
    @pl.when((pl.program_id(0) == 0) & (pl.program_id(1) == 0))
    def _():
        carry[...] = jnp.zeros_like(carry)

    hn = _norm_mod(x_ref[0], ng_ref[...], mod_ref[0, 4:5, :], mod_ref[0, 3:4, :])
    _store_row_tiles(hn_ref, (0,), hn)
    hn_hi = hn.astype(BF16)
    hn_lo = (hn - hn_hi.astype(F32)).astype(BF16)
    hh = _dot(hn_hi, wr_ref[...])
    logits = hh[:, :LANES] + hh[:, LANES:] + _dot(hn_lo, wr_ref[:, :LANES])
    lane = lax.broadcasted_iota(jnp.int32, logits.shape, 1).astype(F32)
    logits = jnp.where(lane < n_exp, logits, NEG_INF)
    v1 = jnp.max(logits, axis=-1, keepdims=True)
    i1 = jnp.min(jnp.where(logits == v1, lane, float(LANES)), axis=-1, keepdims=True)
    rest = jnp.where(lane == i1, NEG_INF, logits)
    v2 = jnp.max(rest, axis=-1, keepdims=True)
    i2 = jnp.min(jnp.where(rest == v2, lane, float(LANES)), axis=-1, keepdims=True)
    e2 = jnp.exp(v2 - v1)
    den = 1.0 + e2
    chosen = jnp.where((lane == i1) | (lane == i2), 1.0, 0.0)
    before = _dot(tri_ref[...], chosen.astype(BF16)) + carry[...]
    r1 = jnp.sum(jnp.where(lane == i1, before, 0.0), axis=-1, keepdims=True)
    r2 = jnp.sum(jnp.where(lane == i2, before, 0.0), axis=-1, keepdims=True)
    total = carry[...] + jnp.sum(chosen, axis=0, keepdims=True)
    carry[...] = total
    cnt_ref[...] = total
    rec = jnp.zeros_like(logits)
    for k, val in ((INFO_E1, i1), (INFO_E2, i2), (INFO_P1, 1.0 / den), (INFO_P2, e2 / den),
                   (INFO_R1, r1), (INFO_R2, r2)):
        rec = jnp.where(lane == k, val, rec)
    info_ref[0] = rec
    infot_ref[...] = rec.T[0:INFO_ROWS, :]


def _router(x, mod, ng, w_router, per_batch_mod):
    bsz, t_len, d = x.shape
    n_exp = w_router.shape[1]
    tm = _tile(t_len, ROUTE_TILE)
    wr = jnp.zeros((d, LANES), F32).at[:, :n_exp].set(w_router)
    wr_hi = wr.astype(BF16)
    wr = jnp.concatenate([wr_hi, (wr - wr_hi.astype(F32)).astype(BF16)], axis=1)
    pos = np.arange(tm)
    tri = jnp.asarray((pos[None, :] < pos[:, None]).astype(np.float32)).astype(BF16)
    bidx = (lambda b: b) if per_batch_mod else (lambda b: 0)
    return pl.pallas_call(
        functools.partial(_router_kernel, n_exp=n_exp),
        grid=(bsz, t_len // tm),
        in_specs=[pl.BlockSpec((1, tm, d), lambda b, i: (b, i, 0)),
                  pl.BlockSpec((1, 6, d), lambda b, i: (bidx(b), 0, 0)),
                  pl.BlockSpec((1, d), lambda b, i: (0, 0)),
                  pl.BlockSpec((d, 2 * LANES), lambda b, i: (0, 0)),
                  pl.BlockSpec((tm, tm), lambda b, i: (0, 0))],
        out_specs=(pl.BlockSpec((1, tm, d // LANES, LANES), lambda b, i: (b, i, 0, 0)),
                   pl.BlockSpec((1, tm, LANES), lambda b, i: (b, i, 0)),
                   pl.BlockSpec((INFO_ROWS, tm), lambda b, i: (0, b * (t_len // tm) + i)),
                   pl.BlockSpec((1, LANES), lambda b, i: (0, 0))),
        out_shape=(jax.ShapeDtypeStruct((bsz, t_len, d // LANES, LANES), F32),
                   jax.ShapeDtypeStruct((bsz, t_len, LANES), F32),
                   jax.ShapeDtypeStruct((INFO_ROWS, bsz * t_len), F32),
                   jax.ShapeDtypeStruct((1, LANES), F32)),
        scratch_shapes=[pltpu.VMEM((1, LANES), F32)],
        compiler_params=_params("arbitrary", "arbitrary"),
        name="moe_router",
    )(x, mod, ng.reshape(1, d), wr, tri)


def _store_row_tiles(ref, lead, val):
    chunks = jnp.stack([val[:, c * LANES:(c + 1) * LANES] for c in range(val.shape[1] // LANES)], axis=0)
    ref[lead] = jnp.swapaxes(chunks, 0, 1)


def _load_row_tiles(ref, lead):
    n_c = ref.shape[-2]
    x = jnp.swapaxes(ref[lead], 0, 1)
    return jnp.concatenate([x[c] for c in range(n_c)], axis=1)


def _dispatch_kernel(zmask_ref, pos_ref, hn_ref, xs_ref, zeros, sem, *, tb, n_row_tiles):
    def row_copy(r, slot):
        return pltpu.make_async_copy(hn_ref.at[r], xs_ref.at[slot], sem)

    def tile_fill(r):
        row0 = pl.multiple_of(r * GROUP_TILE, GROUP_TILE)
        return pltpu.make_async_copy(zeros, xs_ref.at[pl.ds(row0, GROUP_TILE)], sem)

    @pl.when(pl.program_id(0) == 0)
    def _():
        zeros[...] = jnp.zeros_like(zeros)

        def fill(r, c):
            @pl.when(zmask_ref[r] != 0)
            def _():
                tile_fill(r).start()
            return c

        def fill_done(r, c):
            @pl.when(zmask_ref[r] != 0)
            def _():
                tile_fill(r).wait()
            return c

        lax.fori_loop(0, n_row_tiles, fill, 0)
        lax.fori_loop(0, n_row_tiles, fill_done, 0)

    def issue(r, c):
        row_copy(r, pos_ref[0, 0, r]).start()
        row_copy(r, pos_ref[0, 1, r]).start()
        return c

    def drain(r, c):
        row_copy(0, 0).wait()
        row_copy(0, 0).wait()
        return c

    lax.fori_loop(0, tb, issue, 0, unroll=8)
    lax.fori_loop(0, tb, drain, 0, unroll=8)


def _dispatch(zmask, pos, hn, s_max):
    n, n_c, _ = hn.shape
    n_tiles, _, tb = pos.shape
    return pl.pallas_call(
        functools.partial(_dispatch_kernel, tb=tb, n_row_tiles=zmask.shape[0]),
        grid_spec=pltpu.PrefetchScalarGridSpec(
            num_scalar_prefetch=1,
            grid=(n_tiles,),
            in_specs=[pl.BlockSpec((1, 2, tb), lambda i, zm: (i, 0, 0), memory_space=pltpu.SMEM),
                      pl.BlockSpec((tb, n_c, LANES), lambda i, zm: (i, 0, 0))],
            out_specs=pl.BlockSpec(memory_space=pl.ANY),
            scratch_shapes=[pltpu.VMEM((GROUP_TILE, n_c, LANES), F32), pltpu.SemaphoreType.DMA(())]),
        out_shape=jax.ShapeDtypeStruct((s_max, n_c, LANES), F32),
        compiler_params=_params("arbitrary"),
        name="moe_dispatch",
    )(zmask, pos, hn)


def _group_ffn_kernel(te_ref, tx_ref, tv_ref, x_ref, w1_ref, w3_ref, w2_ref, o_ref, xb_scr, acc_scr):
    r, f = pl.program_id(0), pl.program_id(1)
    last = pl.num_programs(1) - 1

    @pl.when(tv_ref[r] != 0)
    def _():
        @pl.when(f == 0)
        def _():
            xb_scr[...] = _load_row_tiles(x_ref, ()).astype(BF16)

        xb = xb_scr[...]
        h1 = _dot(xb, w1_ref[0, 0])
        h3 = _dot(xb, w3_ref[0, 0])
        act = (h1 * jax.nn.sigmoid(h1) * h3).astype(BF16)
        y = _dot(act, w2_ref[0, 0])

        @pl.when(f == 0)
        def _():
            acc_scr[...] = y

        @pl.when(f != 0)
        def _():
            acc_scr[...] += y

        @pl.when(f == last)
        def _():
            _store_row_tiles(o_ref, (), acc_scr[...])

    @pl.when((tv_ref[r] == 0) & (f == last))
    def _():
        o_ref[...] = jnp.zeros_like(o_ref)


def _group_ffn(te, tx, tv, xs, w1, w3, w2, layer):
    s_max, n_c, _ = xs.shape
    d = n_c * LANES
    d_ff = w1.shape[3]
    tf = _tile(d_ff, 1792)
    nf = d_ff // tf
    fidx = lambda r, f, tv: f * tv[r] + (nf - 1) * (1 - tv[r])
    return pl.pallas_call(
        _group_ffn_kernel,
        grid_spec=pltpu.PrefetchScalarGridSpec(
            num_scalar_prefetch=3,
            grid=(s_max // GROUP_TILE, nf),
            in_specs=[pl.BlockSpec((GROUP_TILE, n_c, LANES), lambda r, f, te, tx, tv: (tx[r], 0, 0)),
                      pl.BlockSpec((1, 1, d, tf), lambda r, f, te, tx, tv: (layer, te[r], 0, fidx(r, f, tv))),
                      pl.BlockSpec((1, 1, d, tf), lambda r, f, te, tx, tv: (layer, te[r], 0, fidx(r, f, tv))),
                      pl.BlockSpec((1, 1, tf, d), lambda r, f, te, tx, tv: (layer, te[r], fidx(r, f, tv), 0))],
            out_specs=pl.BlockSpec((GROUP_TILE, n_c, LANES), lambda r, f, te, tx, tv: (r, 0, 0)),
            scratch_shapes=[pltpu.VMEM((GROUP_TILE, d), BF16), pltpu.VMEM((GROUP_TILE, d), F32)]),
        out_shape=jax.ShapeDtypeStruct((s_max, n_c, LANES), F32),
        compiler_params=_params("parallel", "arbitrary"),
        name="moe_group_swiglu",
    )(te, tx, tv, xs, w1, w3, w2)


def _combine_kernel(pos_ref, x_ref, mod_ref, info_ref, fg_ref, ys_ref, out_ref, buf, sem, *, tb, final_norm):
    def row_copy(slot, k, r):
        return pltpu.make_async_copy(ys_ref.at[slot], buf.at[k, r], sem)

    def issue(r, c):
        row_copy(pos_ref[0, 0, r], 0, r).start(priority=0)
        row_copy(pos_ref[0, 1, r], 1, r).start(priority=1)
        return c

    def drain(r, c):
        row_copy(0, 0, 0).wait()
        row_copy(0, 1, 0).wait()
        return c

    lax.fori_loop(0, tb, issue, 0, unroll=8)
    lax.fori_loop(0, tb, drain, 0, unroll=8)
    rec = info_ref[...]
    lane = lax.broadcasted_iota(jnp.int32, rec.shape, 1)
    y = (_lane_pick(rec, lane, INFO_P1) * _load_row_tiles(buf, (0,))
         + _lane_pick(rec, lane, INFO_P2) * _load_row_tiles(buf, (1,)))
    out = x_ref[...] + mod_ref[0, 5:6, :] * y
    if final_norm:
        out = out * lax.rsqrt(jnp.mean(out * out, axis=-1, keepdims=True) + EPS) * fg_ref[...]
    out_ref[...] = out


def _combine(pos, x, mod, info, ys, t_len, per_batch_mod, final_g):
    n, d = x.shape
    n_tiles, _, tb = pos.shape
    bidx = (lambda i: (i * tb) // t_len) if per_batch_mod else (lambda i: 0)
    final_norm = final_g is not None
    gain = (final_g if final_norm else jnp.ones((d,), F32)).reshape(1, d).astype(F32)
    return pl.pallas_call(
        functools.partial(_combine_kernel, tb=tb, final_norm=final_norm),
        grid=(n_tiles,),
        in_specs=[pl.BlockSpec((1, 2, tb), lambda i: (i, 0, 0), memory_space=pltpu.SMEM),
                  pl.BlockSpec((tb, d), lambda i: (i, 0)),
                  pl.BlockSpec((1, 6, d), lambda i: (bidx(i), 0, 0)),
                  pl.BlockSpec((tb, LANES), lambda i: (i, 0)),
                  pl.BlockSpec((1, d), lambda i: (0, 0)),
                  pl.BlockSpec(memory_space=pl.ANY)],
        out_specs=pl.BlockSpec((tb, d), lambda i: (i, 0)),
        out_shape=jax.ShapeDtypeStruct((n, d), F32),
        scratch_shapes=[pltpu.VMEM((2, tb, d // LANES, LANES), F32), pltpu.SemaphoreType.DMA(())],
        compiler_params=_params("arbitrary"),
        name="moe_combine",
    )(pos, x, mod, info, gain, ys)


def _moe(x, mod, ng, w_router, w1, w3, w2, layer, per_batch_mod, final_g=None):
    bsz, t_len, d = x.shape
    n = bsz * t_len
    n_exp = w_router.shape[1]
    hn, info, info_t, cnt = _router(x, mod, ng, w_router, per_batch_mod)
    info = info.reshape(n, LANES)
    e1, e2 = info_t[INFO_E1].astype(jnp.int32), info_t[INFO_E2].astype(jnp.int32)
    r1, r2 = info_t[INFO_R1].astype(jnp.int32), info_t[INFO_R2].astype(jnp.int32)
    counts = cnt[0, :n_exp].astype(jnp.int32)
    padded = ((counts + GROUP_TILE - 1) // GROUP_TILE) * GROUP_TILE
    ends = jnp.cumsum(padded)
    starts = ends - padded
    tb = _tile(n, ROUTE_TILE)
    pos = jnp.stack([(starts[e1] + r1).reshape(n // tb, tb), (starts[e2] + r2).reshape(n // tb, tb)], axis=1)
    s_max = 2 * n + n_exp * GROUP_TILE
    tile_row = jnp.arange(s_max // GROUP_TILE, dtype=jnp.int32) * GROUP_TILE
    tv = (tile_row < ends[-1]).astype(jnp.int32)
    te = jnp.minimum(jnp.searchsorted(ends, tile_row, side="right"), n_exp - 1).astype(jnp.int32)
    tx = (jnp.minimum(tile_row, ends[-1] - GROUP_TILE) // GROUP_TILE).astype(jnp.int32)
    region_end = ((tile_row + GROUP_TILE)[:, None] == ends[None, :]) & (padded > 0)[None, :]
    zmask = jnp.maximum(1 - tv, jnp.any(region_end, axis=1).astype(jnp.int32))
    xs = _dispatch(zmask, pos, hn.reshape(n, d // LANES, LANES), s_max)
    ys = _group_ffn(te, tx, tv, xs, w1.astype(BF16), w3.astype(BF16), w2.astype(BF16), layer)
    out = _combine(pos, x.reshape(n, d), mod, info, ys, t_len, per_batch_mod, final_g)
    return out.reshape(bsz, t_len, d)


def _dft_channel_kernel(x_ref, mod_ref, ng_ref, cs_ref, y_ref, *, gw):
    hn = _norm_mod(x_ref[0], ng_ref[...], mod_ref[0, 1:2, :], mod_ref[0, 0:1, :]).astype(BF16)
    for g in range(hn.shape[1] // gw):
        y = _dot(hn[:, g * gw:(g + 1) * gw], cs_ref[...])
        y_ref[0, 0, :, g * gw:(g + 1) * gw] = y[:, :gw].astype(BF16)
        y_ref[0, 1, :, g * gw:(g + 1) * gw] = y[:, gw:].astype(BF16)


REV_TILE = 128


def _reverse_shift(src_tiles, wrap_row, m1):
    n_t = len(src_tiles)
    first = lax.broadcasted_iota(jnp.int32, (REV_TILE, 1), 0) == 0
    out = []
    for a in range(n_t):
        body = _dot(m1, src_tiles[n_t - 1 - a])
        head = wrap_row if a == 0 else src_tiles[n_t - a][0:1, :]
        out.append(jnp.where(first, head.astype(F32), body))
    return out


def _dft_fold_kernel(ya_ref, yb_ref, yn_ref, m1_ref, f_ref):
    n_t = ya_ref.shape[2] // REV_TILE
    keep = jnp.where(pl.program_id(1) == 0, 0.0, 1.0)
    for plane, sign in ((0, 1.0), (1, -1.0)):
        tiles = [yb_ref[0, plane, a * REV_TILE:(a + 1) * REV_TILE, :] for a in range(n_t)]
        wrap = yn_ref[0, plane, 0:1, :].astype(F32) * keep
        rev = _reverse_shift(tiles, wrap, m1_ref[...])
        for a in range(n_t):
            rows = slice(a * REV_TILE, (a + 1) * REV_TILE)
            f_ref[0, plane, rows, :] = (ya_ref[0, plane, rows, :].astype(F32) + sign * rev[a]).astype(BF16)


def _dft_time_kernel(wc_ref, ws_ref, wcx_ref, wsx_ref, f_ref, yh_ref, xlo_ref, xhi_ref, mod_ref, fw_ref, fb_ref,
                     m1_ref, lo_ref, hi_ref):
    rk = wc_ref.shape[0]
    ec, od = f_ref[0, 0], f_ref[0, 1]
    y_half = yh_ref[0, 0, 0:1, :].astype(F32) * ((2 * f_ref.shape[2]) ** -0.5)
    parity = lax.broadcasted_iota(jnp.int32, (rk, 1), 0) % 2
    p = _dot(wc_ref[...], ec) + jnp.where(parity == 0, 1.0, -1.0) * y_half
    q = _dot(ws_ref[...], od)
    gate = mod_ref[0, 2:3, :]

    def project(z, x):
        return x + gate * (_dot(z, fw_ref[...]) + fb_ref[...])

    lo_ref[0] = project((p - q).astype(BF16), xlo_ref[0])
    px = _dot(wcx_ref[...], ec)[0:1, :] + y_half
    qx = _dot(wsx_ref[...], od)[0:1, :]
    src = (p + q).astype(BF16)
    tiles = [src[a * REV_TILE:(a + 1) * REV_TILE, :] for a in range(rk // REV_TILE)]
    rev = _reverse_shift(tiles, (px + qx).astype(BF16), m1_ref[...])
    hi_ref[0] = project(jnp.concatenate(rev, axis=0).astype(BF16), xhi_ref[0])


def _dft_matrix(n, scale):
    idx = (np.arange(n)[:, None] * np.arange(n)[None, :]) % n
    ang = 2.0 * np.pi * idx.astype(np.float64) / n
    return np.cos(ang) * scale, np.sin(ang) * scale


def _fourier_mixer(x, mod, ng, fn_w, fn_b, per_batch_mod):
    bsz, t_len, d = x.shape
    gw = d // N_GROUPS
    cg, sg = _dft_matrix(gw, gw ** -0.5)
    cs = jnp.asarray(np.concatenate([cg, sg], axis=1).astype(np.float32)).astype(BF16)
    tm = _tile(t_len, 512)
    bidx = (lambda b: b) if per_batch_mod else (lambda b: 0)
    y = pl.pallas_call(
        functools.partial(_dft_channel_kernel, gw=gw),
        grid=(bsz, t_len // tm),
        in_specs=[pl.BlockSpec((1, tm, d), lambda b, i: (b, i, 0)),
                  pl.BlockSpec((1, 6, d), lambda b, i: (bidx(b), 0, 0)),
                  pl.BlockSpec((1, d), lambda b, i: (0, 0)),
                  pl.BlockSpec((gw, 2 * gw), lambda b, i: (0, 0))],
        out_specs=pl.BlockSpec((1, 2, tm, d), lambda b, i: (b, 0, i, 0)),
        out_shape=jax.ShapeDtypeStruct((bsz, 2, t_len, d), BF16),
        compiler_params=_params("parallel", "parallel"),
        name="dft_channel",
    )(x, mod, ng.reshape(1, d), cs)
    half = t_len // 2
    assert half % REV_TILE == 0 and half % DFT_SPLIT == 0
    pos = np.arange(REV_TILE)
    m1 = jnp.asarray((pos[None, :] == REV_TILE - pos[:, None]).astype(np.float32)).astype(BF16)
    rf = _tile(half, 512)
    nb_f = t_len // rf
    sub = 16
    folded = pl.pallas_call(
        _dft_fold_kernel,
        grid=(bsz, half // rf),
        in_specs=[pl.BlockSpec((1, 2, rf, d), lambda b, i: (b, 0, i, 0)),
                  pl.BlockSpec((1, 2, rf, d), lambda b, i: (b, 0, nb_f - 1 - i, 0)),
                  pl.BlockSpec((1, 2, sub, d), lambda b, i: (b, 0, ((nb_f - i) % nb_f) * (rf // sub), 0)),
                  pl.BlockSpec((REV_TILE, REV_TILE), lambda b, i: (0, 0))],
        out_specs=pl.BlockSpec((1, 2, rf, d), lambda b, i: (b, 0, i, 0)),
        out_shape=jax.ShapeDtypeStruct((bsz, 2, half, d), BF16),
        compiler_params=_params("parallel", "parallel"),
        name="dft_fold",
    )(y, y, y, m1)
    kk = np.arange(half + sub)[:, None]
    ang_hi = 2.0 * np.pi * ((kk * DFT_SPLIT * np.arange(half // DFT_SPLIT)[None, :]) % t_len) / t_len
    ang_lo = 2.0 * np.pi * ((kk * np.arange(DFT_SPLIT)[None, :]) % t_len) / t_len
    scale = t_len ** -0.5
    c_hi, s_hi = (jnp.asarray((f(ang_hi) * scale).astype(np.float32))[:, :, None] for f in (np.cos, np.sin))
    c_lo, s_lo = (jnp.asarray(f(ang_lo).astype(np.float32))[:, None, :] for f in (np.cos, np.sin))
    wc = (c_hi * c_lo - s_hi * s_lo).reshape(half + sub, half).astype(BF16)
    ws = (s_hi * c_lo + c_hi * s_lo).reshape(half + sub, half).astype(BF16)
    rk = _tile(half, 512)
    nk = half // rk
    lo, hi = pl.pallas_call(
        _dft_time_kernel,
        grid=(bsz, nk),
        in_specs=[pl.BlockSpec((rk, half), lambda b, i: (i, 0)),
                  pl.BlockSpec((rk, half), lambda b, i: (i, 0)),
                  pl.BlockSpec((sub, half), lambda b, i: ((i + 1) * (rk // sub), 0)),
                  pl.BlockSpec((sub, half), lambda b, i: ((i + 1) * (rk // sub), 0)),
                  pl.BlockSpec((1, 2, half, d), lambda b, i: (b, 0, 0, 0)),
                  pl.BlockSpec((1, 1, sub, d), lambda b, i: (b, 0, half // sub, 0)),
                  pl.BlockSpec((1, rk, d), lambda b, i: (b, i, 0)),
                  pl.BlockSpec((1, rk, d), lambda b, i: (b, 2 * nk - 1 - i, 0)),
                  pl.BlockSpec((1, 6, d), lambda b, i: (bidx(b), 0, 0)),
                  pl.BlockSpec((d, d), lambda b, i: (0, 0)),
                  pl.BlockSpec((1, d), lambda b, i: (0, 0)),
                  pl.BlockSpec((REV_TILE, REV_TILE), lambda b, i: (0, 0))],
        out_specs=(pl.BlockSpec((1, rk, d), lambda b, i: (b, i, 0)),
                   pl.BlockSpec((1, rk, d), lambda b, i: (b, nk - 1 - i, 0))),
        out_shape=(jax.ShapeDtypeStruct((bsz, half, d), F32), jax.ShapeDtypeStruct((bsz, half, d), F32)),
        compiler_params=_params("parallel", "parallel"),
        name="dft_time",
    )(wc, ws, wc, ws, folded, y, x, x, mod, fn_w.astype(BF16), fn_b.reshape(1, d).astype(F32), m1)
    return jnp.concatenate([lo, hi], axis=1)


def _final_norm_kernel(x_ref, g_ref, o_ref):
    x = x_ref[0]
    o_ref[0] = x * lax.rsqrt(jnp.mean(x * x, axis=-1, keepdims=True) + EPS) * g_ref[...]


def _final_norm(x, g):
    bsz, t_len, d = x.shape
    tm = _tile(t_len, 1024)
    return pl.pallas_call(
        _final_norm_kernel,
        grid=(bsz, t_len // tm),
        in_specs=[pl.BlockSpec((1, tm, d), lambda b, i: (b, i, 0)), pl.BlockSpec((1, d), lambda b, i: (0, 0))],
        out_specs=pl.BlockSpec((1, tm, d), lambda b, i: (b, i, 0)),
        out_shape=jax.ShapeDtypeStruct((bsz, t_len, d), F32),
        compiler_params=_params("parallel", "parallel"),
        name="final_norm",
    )(x, g.reshape(1, d))


def _trunk(x, mod, cache, use_rope, per_batch_mod, p):
    bsz, t_len, d = x.shape
    depth = mod.shape[0]
    flat = (lambda a: a) if per_batch_mod else (lambda a: a.reshape(1, bsz * t_len, a.shape[-1]))
    unflat = (lambda a: a) if per_batch_mod else (lambda a: a.reshape(bsz, t_len, a.shape[-1]))
    states = []
    for i in range(depth):
        j = i // 2
        m_i = mod[i]
        if i % 2 == 0:
            q, kt, v, o, r = _mlstm_inproj(x, m_i, p["norm_g"][i, 0], p["ml_w_in"][j], p["ml_b_gate"][j],
                                           use_rope, per_batch_mod)
            outs = []
            for direction in range(2):
                if cache is None:
                    cn0 = jnp.zeros((bsz, N_HEADS // 2, 2 * HEAD_DK, 2 * HEAD_DV), F32)
                    m0 = jnp.zeros((bsz, N_HEADS, LANES), F32)
                else:
                    cn0, m0 = _pack_state(cache[0][:, j, direction], cache[1][:, j, direction],
                                          cache[2][:, j, direction])
                outs.append(_mlstm_scan(q, kt, v, r, cn0, m0, reverse=bool(direction)))
            states.append([_unpack_state(cn, m) for (_, cn, m) in outs])
            x = _mlstm_out(outs[0][0], outs[1][0], o, x, m_i, p["ml_head_g"][j], p["ml_w_out"][j], per_batch_mod)
            x = unflat(_ffn(flat(x), m_i, p["norm_g"][i, 1], p["ffn_w1"][j], p["ffn_w3"][j], p["ffn_w2"][j],
                            per_batch_mod))
        else:
            x = _fourier_mixer(x, m_i, p["norm_g"][i, 0], p["fn_w"][j], p["fn_b"][j], per_batch_mod)
            closing = p["final_g"] if i == depth - 1 else None
            x = unflat(_moe(flat(x), m_i, p["norm_g"][i, 1], p["moe_router"][j], p["moe_w1"], p["moe_w3"],
                            p["moe_w2"], j, per_batch_mod, closing))
    if depth % 2 == 1:
        x = unflat(_final_norm(flat(x), p["final_g"]))
    return x, states


def kernel(x_prompt, x_sample, state_C, state_n, state_m, c, c_ctx, w_mod, b_mod, norm_g, final_g,
           ml_w_in, ml_b_gate, ml_head_g, ml_w_out, fn_w, fn_b, ffn_w1, ffn_w3, ffn_w2,
           moe_router, moe_w1, moe_w3, moe_w2):
    p = dict(norm_g=norm_g, final_g=final_g, ml_w_in=ml_w_in, ml_b_gate=ml_b_gate, ml_head_g=ml_head_g,
             ml_w_out=ml_w_out, fn_w=fn_w, fn_b=fn_b, ffn_w1=ffn_w1, ffn_w3=ffn_w3, ffn_w2=ffn_w2,
             moe_router=moe_router, moe_w1=moe_w1, moe_w3=moe_w3, moe_w2=moe_w2)
    depth, d = w_mod.shape[0], w_mod.shape[1]
    n_dec = c.shape[0]
    rows = ((n_dec + 1 + 7) // 8) * 8
    cond = jnp.zeros((rows, d), F32).at[:n_dec].set(c).at[n_dec].set(c_ctx)
    mod = _mod_table(cond, w_mod, b_mod).reshape(depth, rows, 6, d)
    y_prompt, st = _trunk(x_prompt, mod[:, n_dec:n_dec + 1], None, False, False, p)
    y_sample, _ = _trunk(x_sample, mod[:, :n_dec], (state_C, state_n, state_m), True, True, p)
    new_c = jnp.stack([jnp.stack([s[0][0], s[1][0]], axis=1) for s in st], axis=1)
    new_n = jnp.stack([jnp.stack([s[0][1], s[1][1]], axis=1) for s in st], axis=1)
    new_m = jnp.stack([jnp.stack([s[0][2], s[1][2]], axis=1) for s in st], axis=1)
    return (y_prompt, y_sample, new_c.astype(x_prompt.dtype), new_n.astype(x_prompt.dtype),
            new_m.astype(x_prompt.dtype))
```

```python
import functools

import numpy as np
import jax
import jax.numpy as jnp
from jax import lax
from jax.experimental import pallas as pl
from jax.experimental.pallas import tpu as pltpu

F32 = jnp.float32
BF16 = jnp.bfloat16
HIGHEST = lax.Precision.HIGHEST

EPS = 1e-6
N_HEADS = 8
HEAD_DK = 64
HEAD_DV = 128
QK_W = N_HEADS * HEAD_DK
V_W = N_HEADS * HEAD_DV
GRID_W = 64
ROPE_BASE = 10000.0
N_GROUPS = 4
DFT_SPLIT = 64
N_GATES = 4 * N_HEADS
LANES = 128
SCAN_CHUNK = 128
SCAN_ROWS = 3 * N_HEADS
VMEM_LIMIT = 56 * 1024 * 1024
NEG_INF = float("-inf")


def _params(*sem):
    return pltpu.CompilerParams(dimension_semantics=sem, vmem_limit_bytes=VMEM_LIMIT)


def _tile(n, pref):
    t = min(n, pref)
    assert n % t == 0, (n, pref)
    return t


def _norm_mod(x, gain, scale, shift):
    ms = jnp.mean(x * x, axis=-1, keepdims=True)
    return x * lax.rsqrt(ms + EPS) * gain * (1.0 + scale) + shift


def _dot(a, b):
    return jnp.dot(a, b, preferred_element_type=F32)


def _dot_nt(a, b):
    return lax.dot_general(a, b, (((1,), (1,)), ((), ())), preferred_element_type=F32)


def _exact_rows_dot(x, ones_mat):
    hi = x.astype(BF16)
    rest = x - hi.astype(F32)
    mid = rest.astype(BF16)
    lo = (rest - mid.astype(F32)).astype(BF16)
    n = x.shape[0]
    out = _dot(jnp.concatenate([hi, mid, lo, jnp.zeros_like(hi)], axis=0), ones_mat)
    return out[0:n] + out[n:2 * n] + out[2 * n:3 * n]


def _log_sigmoid(x):
    return jnp.minimum(x, 0.0) - jnp.log1p(jnp.exp(-jnp.abs(x)))


def _mod_kernel(c_ref, w_ref, b_ref, o_ref):
    c = c_ref[...]
    s = c * jax.nn.sigmoid(c)
    o_ref[0] = jnp.dot(s, w_ref[0], preferred_element_type=F32, precision=HIGHEST) + b_ref[0]


def _mod_table(cond, w_mod, b_mod):
    depth, d, n = w_mod.shape
    rows = cond.shape[0]
    tn = _tile(n, 1536)
    return pl.pallas_call(
        _mod_kernel,
        grid=(depth, n // tn),
        in_specs=[pl.BlockSpec((rows, d), lambda l, j: (0, 0)),
                  pl.BlockSpec((1, d, tn), lambda l, j: (l, 0, j)),
                  pl.BlockSpec((1, 1, tn), lambda l, j: (l, 0, j))],
        out_specs=pl.BlockSpec((1, rows, tn), lambda l, j: (l, 0, j)),
        out_shape=jax.ShapeDtypeStruct((depth, rows, n), F32),
        compiler_params=_params("parallel", "parallel"),
        name="adaln_table",
    )(cond, w_mod, b_mod.reshape(depth, 1, n))


def _rope_tables(t_len):
    pos = np.arange(t_len)
    row = (pos // GRID_W).astype(np.float32)
    col = (pos % GRID_W).astype(np.float32)
    nf = HEAD_DK // 4
    inv = (np.float32(ROPE_BASE) ** (-np.arange(nf, dtype=np.float32) / nf)).astype(np.float32)
    d = np.arange(HEAD_DK)
    p = np.where(d[None, :] < HEAD_DK // 2, row[:, None], col[:, None]).astype(np.float32)
    ang = p * inv[d % nf][None, :]
    sign = np.where((d % (2 * nf)) < nf, -1.0, 1.0).astype(np.float32)
    return np.cos(ang).astype(np.float32), (np.sin(ang) * sign[None, :]).astype(np.float32)


def _inproj_kernel(*refs, use_rope):
    if use_rope:
        (x_ref, mod_ref, ng_ref, wq_ref, wkt_ref, wv_ref, wo_ref, wgt_ref, bg_ref, trif_ref, trib_ref,
         cq_ref, sq_ref, ck_ref, sk_ref, q_ref, kt_ref, v_ref, o_ref, r_ref) = refs
    else:
        (x_ref, mod_ref, ng_ref, wq_ref, wkt_ref, wv_ref, wo_ref, wgt_ref, bg_ref, trif_ref, trib_ref,
         q_ref, kt_ref, v_ref, o_ref, r_ref) = refs
    hn = _norm_mod(x_ref[0], ng_ref[...], mod_ref[0, 1:2, :], mod_ref[0, 0:1, :])
    hb = hn.astype(BF16)
    q = _dot(hb, wq_ref[...]) * (HEAD_DK ** -0.5)
    kt = _dot_nt(wkt_ref[...], hb)
    if use_rope:
        nf = HEAD_DK // 4
        lane = lax.broadcasted_iota(jnp.int32, (q.shape[0], LANES), 1)
        first_q = (lane % (2 * nf)) < nf
        sub = lax.broadcasted_iota(jnp.int32, (LANES, kt.shape[1]), 0)
        first_k = (sub % (2 * nf)) < nf
        cq, sq, ck, sk = cq_ref[...], sq_ref[...], ck_ref[...], sk_ref[...]
        for s in range(QK_W // LANES):
            qs = q[:, s * LANES:(s + 1) * LANES]
            sw = jnp.where(first_q, pltpu.roll(qs, LANES - nf, 1), pltpu.roll(qs, nf, 1))
            q_ref[0, :, s * LANES:(s + 1) * LANES] = (qs * cq + sw * sq).astype(BF16)
            ks = kt[s * LANES:(s + 1) * LANES, :]
            sw = jnp.where(first_k, pltpu.roll(ks, LANES - nf, 0), pltpu.roll(ks, nf, 0))
            kt_ref[0, s * LANES:(s + 1) * LANES, :] = (ks * ck + sw * sk).astype(BF16)
    else:
        q_ref[0] = q.astype(BF16)
        kt_ref[0] = kt.astype(BF16)
    v_ref[0] = _dot(hb, wv_ref[...]).astype(BF16)
    o_ref[0] = _dot(hb, wo_ref[...]).astype(BF16)
    gt = _dot_nt(wgt_ref[...], hb) + bg_ref[...]
    h = N_HEADS
    i_f, f_f = gt[0:h], _log_sigmoid(gt[h:2 * h])
    i_b, f_b = gt[2 * h:3 * h], _log_sigmoid(gt[3 * h:4 * h])
    b_f = _exact_rows_dot(f_f, trif_ref[...])
    b_b = _exact_rows_dot(f_b, trib_ref[...])
    a_f, a_b = i_f - b_f, i_b - b_b
    width = a_f.shape[1]
    in_chunk = lax.broadcasted_iota(jnp.int32, a_f.shape, 1) % SCAN_CHUNK
    c_f, c_b = a_f, a_b
    step = 1
    while step < SCAN_CHUNK:
        c_f = jnp.where(in_chunk >= step, jnp.maximum(c_f, pltpu.roll(c_f, step, 1)), c_f)
        c_b = jnp.where(in_chunk < SCAN_CHUNK - step, jnp.maximum(c_b, pltpu.roll(c_b, width - step, 1)), c_b)
        step *= 2
    for k, rows in enumerate((a_f, b_f, c_f, a_b, b_b, c_b)):
        r_ref[0, k * h:(k + 1) * h, :] = rows


def _mlstm_inproj(x, mod, ng, w_in, b_gate, use_rope, per_batch_mod):
    bsz, t_len, d = x.shape
    tm = _tile(t_len, 512)
    wq = w_in[:, :QK_W].astype(BF16)
    wkt = w_in[:, QK_W:2 * QK_W].T.astype(BF16)
    wv = w_in[:, 2 * QK_W:2 * QK_W + V_W].astype(BF16)
    wo = w_in[:, 2 * QK_W + V_W:2 * QK_W + 2 * V_W].astype(BF16)
    wgt = w_in[:, 2 * QK_W + 2 * V_W:].T.astype(BF16)
    bg = b_gate.reshape(N_GATES, 1).astype(F32)
    pos = np.arange(tm)
    same = (pos[:, None] // SCAN_CHUNK) == (pos[None, :] // SCAN_CHUNK)
    trif = jnp.asarray((same & (pos[:, None] <= pos[None, :])).astype(np.float32)).astype(BF16)
    trib = jnp.asarray((same & (pos[:, None] >= pos[None, :])).astype(np.float32)).astype(BF16)
    bidx = (lambda b: b) if per_batch_mod else (lambda b: 0)
    const = lambda shp: pl.BlockSpec(shp, lambda b, i: (0,) * len(shp))
    in_specs = [pl.BlockSpec((1, tm, d), lambda b, i: (b, i, 0)),
                pl.BlockSpec((1, 6, d), lambda b, i: (bidx(b), 0, 0)),
                const((1, d)), const((d, QK_W)), const((QK_W, d)), const((d, V_W)), const((d, V_W)),
                const((N_GATES, d)), const((N_GATES, 1)), const((tm, tm)), const((tm, tm))]
    args = [x, mod, ng.reshape(1, d), wq, wkt, wv, wo, wgt, bg, trif, trib]
    if use_rope:
        cos, sin = _rope_tables(t_len)
        rep = LANES // HEAD_DK
        args += [jnp.asarray(np.tile(cos, (1, rep))), jnp.asarray(np.tile(sin, (1, rep))),
                 jnp.asarray(np.tile(cos.T, (rep, 1))), jnp.asarray(np.tile(sin.T, (rep, 1)))]
        in_specs += [pl.BlockSpec((tm, LANES), lambda b, i: (i, 0)), pl.BlockSpec((tm, LANES), lambda b, i: (i, 0)),
                     pl.BlockSpec((LANES, tm), lambda b, i: (0, i)), pl.BlockSpec((LANES, tm), lambda b, i: (0, i))]
    out_shape = (jax.ShapeDtypeStruct((bsz, t_len, QK_W), BF16),
                 jax.ShapeDtypeStruct((bsz, QK_W, t_len), BF16),
                 jax.ShapeDtypeStruct((bsz, t_len, V_W), BF16),
                 jax.ShapeDtypeStruct((bsz, t_len, V_W), BF16),
                 jax.ShapeDtypeStruct((bsz, 2 * SCAN_ROWS, t_len), F32))
    out_specs = (pl.BlockSpec((1, tm, QK_W), lambda b, i: (b, i, 0)),
                 pl.BlockSpec((1, QK_W, tm), lambda b, i: (b, 0, i)),
                 pl.BlockSpec((1, tm, V_W), lambda b, i: (b, i, 0)),
                 pl.BlockSpec((1, tm, V_W), lambda b, i: (b, i, 0)),
                 pl.BlockSpec((1, 2 * SCAN_ROWS, tm), lambda b, i: (b, 0, i)))
    return pl.pallas_call(
        functools.partial(_inproj_kernel, use_rope=use_rope),
        grid=(bsz, t_len // tm), in_specs=in_specs, out_specs=out_specs, out_shape=out_shape,
        compiler_params=_params("parallel", "parallel"),
        name="mlstm_inproj",
    )(*args)


def _scan_kernel(q_ref, kt_ref, v_ref, r_ref, cn0_ref, m0_ref, spread_ref, h_ref, cn_out_ref, m_out_ref,
                 cn_scr, m_scr, *, reverse, n_chunks):
    L = SCAN_CHUNK
    j = pl.program_id(1)

    @pl.when(j == 0)
    def _():
        cn_scr[...] = cn0_ref[0]
        m_scr[...] = m0_ref[0]

    row_i = lax.broadcasted_iota(jnp.int32, (L, L), 0)
    col_i = lax.broadcasted_iota(jnp.int32, (L, L), 1)
    visible = (col_i >= row_i) if reverse else (col_i <= row_i)
    upper_lanes = lax.broadcasted_iota(jnp.int32, (L, LANES), 1) >= HEAD_DK
    ones = jnp.ones((L, LANES), BF16)
    no_rows = jnp.zeros((HEAD_DK, 2 * HEAD_DV), BF16)
    end_lane = lax.broadcasted_iota(jnp.int32, (1, L), 1) == (0 if reverse else L - 1)
    order = range(n_chunks - 1, -1, -1) if reverse else range(n_chunks)
    head_rows = [slice((h % 2) * HEAD_DK, (h % 2 + 1) * HEAD_DK) for h in range(N_HEADS)]
    states = [(cn_scr[h // 2, head_rows[h], :], m_scr[h:h + 1, 0:1]) for h in range(N_HEADS)]

    def columns(c):
        x = r_ref[0, N_HEADS:3 * N_HEADS, c * L:(c + 1) * L]
        hi = x.astype(BF16).astype(F32)
        mid = (x - hi).astype(BF16).astype(F32)
        lo = x - hi - mid
        parts = jnp.concatenate([hi, mid, lo, jnp.zeros((L - 6 * N_HEADS, L), F32)], axis=0)
        return _dot(parts.T.astype(BF16), spread_ref[...])

    cols = {c: columns(c) for c in order}
    new_states = []
    for h in range(N_HEADS):
        p, e = h // 2, h % 2
        cn, m = states[h]
        for c in order:
            r0 = c * L
            q_pair = q_ref[0, r0:r0 + L, p * LANES:(p + 1) * LANES]
            q_m = jnp.where(upper_lanes if e else jnp.logical_not(upper_lanes), q_pair, jnp.zeros_like(q_pair))
            kt_pair = kt_ref[0, p * LANES:(p + 1) * LANES, r0:r0 + L]
            kt_h = kt_ref[0, h * HEAD_DK:(h + 1) * HEAD_DK, r0:r0 + L]
            a_row = r_ref[0, h:h + 1, r0:r0 + L]
            b_row = r_ref[0, N_HEADS + h:N_HEADS + h + 1, r0:r0 + L]
            v_aug = jnp.concatenate([v_ref[0, r0:r0 + L, h * HEAD_DV:(h + 1) * HEAD_DV], ones], axis=1)
            am = jnp.where(visible, a_row, NEG_INF)
            a_max = jnp.max(a_row, axis=1, keepdims=True)
            kw = (kt_h.astype(F32) * jnp.exp(a_row - a_max)).astype(BF16)
            upd = _dot(kw, v_aug)
            b_col = cols[c][:, h * LANES:(h + 1) * LANES]
            b_end = jnp.sum(jnp.where(end_lane, b_row, 0.0), axis=1, keepdims=True)
            u = jnp.maximum(m, cols[c][:, (N_HEADS + h) * LANES:(N_HEADS + h + 1) * LANES])
            guard = jnp.exp(-(b_col + u))
            s = (_dot(q_m, kt_pair) * jnp.exp(am - u)).astype(BF16)
            q_old = (q_m.astype(F32) * jnp.exp(m - u)).astype(BF16)
            cn_b = cn.astype(BF16)
            cn_pair = jnp.concatenate([no_rows, cn_b] if e else [cn_b, no_rows], axis=0)
            both = _dot(jnp.concatenate([q_old, s], axis=1), jnp.concatenate([cn_pair, v_aug], axis=0))
            num, den = both[:, :HEAD_DV], both[:, HEAD_DV:]
            h_ref[0, r0:r0 + L, h * HEAD_DV:(h + 1) * HEAD_DV] = (num / jnp.maximum(jnp.abs(den), guard)).astype(BF16)
            u_end = jnp.maximum(m, a_max)
            cn = jnp.exp(m - u_end) * cn + jnp.exp(a_max - u_end) * upd
            m = b_end + u_end
        new_states.append((cn, m))
    for h in range(N_HEADS):
        cn_scr[h // 2, head_rows[h], :] = new_states[h][0]
        m_scr[h:h + 1, :] = jnp.broadcast_to(new_states[h][1], (1, LANES))

    @pl.when(j == pl.num_programs(1) - 1)
    def _():
        cn_out_ref[0] = cn_scr[...]
        m_out_ref[0] = m_scr[...]


def _mlstm_scan(q, kt, v, r, cn0, m0, reverse):
    bsz, t_len, _ = q.shape
    tb = _tile(t_len, 4 * SCAN_CHUNK)
    nblk = t_len // tb
    blk = (lambda j: nblk - 1 - j) if reverse else (lambda j: j)
    d = 1 if reverse else 0
    half = N_HEADS // 2
    src = np.arange(LANES)[:, None]
    dst = np.arange(2 * N_HEADS * LANES)[None, :]
    spread = jnp.asarray(((src < 6 * N_HEADS) & (src % (2 * N_HEADS) == dst // LANES)).astype(np.float32)).astype(BF16)
    return pl.pallas_call(
        functools.partial(_scan_kernel, reverse=reverse, n_chunks=tb // SCAN_CHUNK),
        grid=(bsz, nblk),
        in_specs=[pl.BlockSpec((1, tb, QK_W), lambda b, j: (b, blk(j), 0)),
                  pl.BlockSpec((1, QK_W, tb), lambda b, j: (b, 0, blk(j))),
                  pl.BlockSpec((1, tb, V_W), lambda b, j: (b, blk(j), 0)),
                  pl.BlockSpec((1, SCAN_ROWS, tb), lambda b, j: (b, d, blk(j))),
                  pl.BlockSpec((1, half, 2 * HEAD_DK, 2 * HEAD_DV), lambda b, j: (b, 0, 0, 0)),
                  pl.BlockSpec((1, N_HEADS, LANES), lambda b, j: (b, 0, 0)),
                  pl.BlockSpec((LANES, 2 * N_HEADS * LANES), lambda b, j: (0, 0))],
        out_specs=(pl.BlockSpec((1, tb, V_W), lambda b, j: (b, blk(j), 0)),
                   pl.BlockSpec((1, half, 2 * HEAD_DK, 2 * HEAD_DV), lambda b, j: (b, 0, 0, 0)),
                   pl.BlockSpec((1, N_HEADS, LANES), lambda b, j: (b, 0, 0))),
        out_shape=(jax.ShapeDtypeStruct((bsz, t_len, V_W), BF16),
                   jax.ShapeDtypeStruct((bsz, half, 2 * HEAD_DK, 2 * HEAD_DV), F32),
                   jax.ShapeDtypeStruct((bsz, N_HEADS, LANES), F32)),
        scratch_shapes=[pltpu.VMEM((half, 2 * HEAD_DK, 2 * HEAD_DV), F32),
                        pltpu.VMEM((N_HEADS, LANES), F32)],
        compiler_params=_params("parallel", "arbitrary"),
        name="mlstm_scan_bwd" if reverse else "mlstm_scan_fwd",
    )(q, kt, v, r, cn0, m0, spread)


def _pack_state(c, n, m):
    bsz = c.shape[0]
    cn = jnp.concatenate([c, jnp.broadcast_to(n[..., None], n.shape + (HEAD_DV,))], axis=-1)
    cn = cn.reshape(bsz, N_HEADS // 2, 2 * HEAD_DK, 2 * HEAD_DV)
    return cn.astype(F32), jnp.broadcast_to(m[..., None], (bsz, N_HEADS, LANES)).astype(F32)


def _unpack_state(cn, m):
    bsz = cn.shape[0]
    cn = cn.reshape(bsz, N_HEADS, HEAD_DK, 2 * HEAD_DV)
    return cn[..., :HEAD_DV], cn[..., HEAD_DV], m[..., 0]


def _mlstm_out_kernel(hf_ref, hb_ref, o_ref, x_ref, mod_ref, hg_ref, w_ref, out_ref):
    hs = hf_ref[0].astype(F32) + hb_ref[0].astype(F32)
    parts = []
    for h in range(N_HEADS):
        z = hs[:, h * HEAD_DV:(h + 1) * HEAD_DV]
        parts.append(z * lax.rsqrt(jnp.mean(z * z, axis=-1, keepdims=True) + EPS))
    hn = jnp.concatenate(parts, axis=1) * hg_ref[...] * jax.nn.sigmoid(o_ref[0].astype(F32))
    y = _dot(hn.astype(BF16), w_ref[...])
    out_ref[0] = x_ref[0] + mod_ref[0, 2:3, :] * y


def _mlstm_out(hf, hb, o, x, mod, head_g, w_out, per_batch_mod):
    bsz, t_len, d = x.shape
    tm = _tile(t_len, 512)
    bidx = (lambda b: b) if per_batch_mod else (lambda b: 0)
    tok = lambda w: pl.BlockSpec((1, tm, w), lambda b, i: (b, i, 0))
    return pl.pallas_call(
        _mlstm_out_kernel,
        grid=(bsz, t_len // tm),
        in_specs=[tok(V_W), tok(V_W), tok(V_W), tok(d),
                  pl.BlockSpec((1, 6, d), lambda b, i: (bidx(b), 0, 0)),
                  pl.BlockSpec((1, V_W), lambda b, i: (0, 0)),
                  pl.BlockSpec((V_W, d), lambda b, i: (0, 0))],
        out_specs=tok(d),
        out_shape=jax.ShapeDtypeStruct((bsz, t_len, d), F32),
        compiler_params=_params("parallel", "parallel"),
        name="mlstm_out",
    )(hf, hb, o, x, mod, head_g.reshape(1, V_W).astype(F32), w_out.astype(BF16))


def _ffn_kernel(x_ref, mod_ref, ng_ref, w1_ref, w3_ref, w2_ref, out_ref, hn_scr, acc_scr):
    f = pl.program_id(2)

    @pl.when(f == 0)
    def _():
        hn = _norm_mod(x_ref[0], ng_ref[...], mod_ref[0, 4:5, :], mod_ref[0, 3:4, :])
        hn_scr[...] = hn.astype(BF16)
        acc_scr[...] = jnp.zeros_like(acc_scr)

    hb = hn_scr[...]
    h1 = _dot(hb, w1_ref[...])
    h3 = _dot(hb, w3_ref[...])
    act = (h1 * jax.nn.sigmoid(h1) * h3).astype(BF16)
    acc_scr[...] += _dot(act, w2_ref[...])

    @pl.when(f == pl.num_programs(2) - 1)
    def _():
        out_ref[0] = x_ref[0] + mod_ref[0, 5:6, :] * acc_scr[...]


def _ffn(x, mod, ng, w1, w3, w2, per_batch_mod):
    bsz, t_len, d = x.shape
    d_ff = w1.shape[1]
    tm = _tile(t_len, 1024)
    tf = _tile(d_ff, 512)
    bidx = (lambda b: b) if per_batch_mod else (lambda b: 0)
    return pl.pallas_call(
        _ffn_kernel,
        grid=(bsz, t_len // tm, d_ff // tf),
        in_specs=[pl.BlockSpec((1, tm, d), lambda b, i, f: (b, i, 0)),
                  pl.BlockSpec((1, 6, d), lambda b, i, f: (bidx(b), 0, 0)),
                  pl.BlockSpec((1, d), lambda b, i, f: (0, 0)),
                  pl.BlockSpec((d, tf), lambda b, i, f: (0, f)),
                  pl.BlockSpec((d, tf), lambda b, i, f: (0, f)),
                  pl.BlockSpec((tf, d), lambda b, i, f: (f, 0))],
        out_specs=pl.BlockSpec((1, tm, d), lambda b, i, f: (b, i, 0)),
        out_shape=jax.ShapeDtypeStruct((bsz, t_len, d), F32),
        scratch_shapes=[pltpu.VMEM((tm, d), BF16), pltpu.VMEM((tm, d), F32)],
        compiler_params=_params("parallel", "parallel", "arbitrary"),
        name="dense_swiglu",
    )(x, mod, ng.reshape(1, d), w1.astype(BF16), w3.astype(BF16), w2.astype(BF16))


ROUTE_TILE = 512
GROUP_TILE = 512
INFO_E1, INFO_E2, INFO_P1, INFO_P2, INFO_R1, INFO_R2 = range(6)
INFO_ROWS = 8


def _lane_pick(rec, lane, k):
    return jnp.sum(jnp.where(lane == k, rec, 0.0), axis=-1, keepdims=True)


def _token_tile(x_refs, first_half):
    if len(x_refs) == 1:
        return x_refs[0][0]
    return jnp.where(first_half, x_refs[0][0], x_refs[1][0])


def _half_specs(tm, d, tiles_per_seq, locate):
    nh = tiles_per_seq // 2

    def lo(*g):
        b, i = locate(*g)
        return (b, jnp.minimum(i, nh - 1), 0)

    def hi(*g):
        b, i = locate(*g)
        return (b, jnp.maximum(i - nh, 0), 0)

    return [pl.BlockSpec((1, tm, d), lo), pl.BlockSpec((1, tm, d), hi)]


def _router_kernel(*refs, n_exp, n_x):
    x_refs = refs[:n_x]
    mod_ref, ng_ref, wr_ref, tri_ref, hn_ref, info_ref, infot_ref, cnt_ref, carry = refs[n_x:]

    @pl.when((pl.program_id(0) == 0) & (pl.program_id(1) == 0))
    def _():
        carry[...] = jnp.zeros_like(carry)

    x = _token_tile(x_refs, pl.program_id(1) < pl.num_programs(1) // 2)
    hn = _norm_mod(x, ng_ref[...], mod_ref[0, 4:5, :], mod_ref[0, 3:4, :])
    _store_row_tiles(hn_ref, (0,), hn)
    hn_hi = hn.astype(BF16)
    hn_lo = (hn - hn_hi.astype(F32)).astype(BF16)
    hh = _dot(hn_hi, wr_ref[...])
    logits = hh[:, :LANES] + hh[:, LANES:] + _dot(hn_lo, wr_ref[:, :LANES])
    lane = lax.broadcasted_iota(jnp.int32, logits.shape, 1).astype(F32)
    logits = jnp.where(lane < n_exp, logits, NEG_INF)
    v1 = jnp.max(logits, axis=-1, keepdims=True)
    i1 = jnp.min(jnp.where(logits == v1, lane, float(LANES)), axis=-1, keepdims=True)
    rest = jnp.where(lane == i1, NEG_INF, logits)
    v2 = jnp.max(rest, axis=-1, keepdims=True)
    i2 = jnp.min(jnp.where(rest == v2, lane, float(LANES)), axis=-1, keepdims=True)
    e2 = jnp.exp(v2 - v1)
    den = 1.0 + e2
    chosen = jnp.where((lane == i1) | (lane == i2), 1.0, 0.0)
    before = _dot(tri_ref[...], chosen.astype(BF16)) + carry[...]
    r1 = jnp.sum(jnp.where(lane == i1, before, 0.0), axis=-1, keepdims=True)
    r2 = jnp.sum(jnp.where(lane == i2, before, 0.0), axis=-1, keepdims=True)
    total = carry[...] + jnp.sum(chosen, axis=0, keepdims=True)
    carry[...] = total
    cnt_ref[...] = total
    rec = jnp.zeros_like(logits)
    for k, val in ((INFO_E1, i1), (INFO_E2, i2), (INFO_P1, 1.0 / den), (INFO_P2, e2 / den),
                   (INFO_R1, r1), (INFO_R2, r2)):
        rec = jnp.where(lane == k, val, rec)
    info_ref[0] = rec
    infot_ref[...] = rec.T[0:INFO_ROWS, :]


def _router(xs, mod, ng, w_router, per_batch_mod):
    bsz, t_len, d = xs[0].shape
    t_len *= len(xs)
    n_exp = w_router.shape[1]
    tm = _tile(t_len, ROUTE_TILE)
    wr = jnp.zeros((d, LANES), F32).at[:, :n_exp].set(w_router)
    wr_hi = wr.astype(BF16)
    wr = jnp.concatenate([wr_hi, (wr - wr_hi.astype(F32)).astype(BF16)], axis=1)
    pos = np.arange(tm)
    tri = jnp.asarray((pos[None, :] < pos[:, None]).astype(np.float32)).astype(BF16)
    bidx = (lambda b: b) if per_batch_mod else (lambda b: 0)
    if len(xs) == 2:
        x_specs = _half_specs(tm, d, t_len // tm, lambda b, i: (b, i))
    else:
        x_specs = [pl.BlockSpec((1, tm, d), lambda b, i: (b, i, 0))]
    return pl.pallas_call(
        functools.partial(_router_kernel, n_exp=n_exp, n_x=len(xs)),
        grid=(bsz, t_len // tm),
        in_specs=x_specs + [
                  pl.BlockSpec((1, 6, d), lambda b, i: (bidx(b), 0, 0)),
                  pl.BlockSpec((1, d), lambda b, i: (0, 0)),
                  pl.BlockSpec((d, 2 * LANES), lambda b, i: (0, 0)),
                  pl.BlockSpec((tm, tm), lambda b, i: (0, 0))],
        out_specs=(pl.BlockSpec((1, tm, d // LANES, LANES), lambda b, i: (b, i, 0, 0)),
                   pl.BlockSpec((1, tm, LANES), lambda b, i: (b, i, 0)),
                   pl.BlockSpec((INFO_ROWS, tm), lambda b, i: (0, b * (t_len // tm) + i)),
                   pl.BlockSpec((1, LANES), lambda b, i: (0, 0))),
        out_shape=(jax.ShapeDtypeStruct((bsz, t_len, d // LANES, LANES), F32),
                   jax.ShapeDtypeStruct((bsz, t_len, LANES), F32),
                   jax.ShapeDtypeStruct((INFO_ROWS, bsz * t_len), F32),
                   jax.ShapeDtypeStruct((1, LANES), F32)),
        scratch_shapes=[pltpu.VMEM((1, LANES), F32)],
        compiler_params=_params("arbitrary", "arbitrary"),
        name="moe_router",
    )(*xs, mod, ng.reshape(1, d), wr, tri)


def _store_row_tiles(ref, lead, val):
    chunks = jnp.stack([val[:, c * LANES:(c + 1) * LANES] for c in range(val.shape[1] // LANES)], axis=0)
    ref[lead] = jnp.swapaxes(chunks, 0, 1)


def _load_row_tiles(ref, lead):
    n_c = ref.shape[-2]
    x = jnp.swapaxes(ref[lead], 0, 1)
    return jnp.concatenate([x[c] for c in range(n_c)], axis=1)


def _dispatch_kernel(zmask_ref, pos_ref, hn_ref, xs_ref, zeros, sem, *, tb, n_row_tiles):
    def row_copy(r, slot):
        return pltpu.make_async_copy(hn_ref.at[r], xs_ref.at[slot], sem)

    def tile_fill(r):
        row0 = pl.multiple_of(r * GROUP_TILE, GROUP_TILE)
        return pltpu.make_async_copy(zeros, xs_ref.at[pl.ds(row0, GROUP_TILE)], sem)

    @pl.when(pl.program_id(0) == 0)
    def _():
        zeros[...] = jnp.zeros_like(zeros)

        def fill(r, c):
            @pl.when(zmask_ref[r] != 0)
            def _():
                tile_fill(r).start()
            return c

        def fill_done(r, c):
            @pl.when(zmask_ref[r] != 0)
            def _():
                tile_fill(r).wait()
            return c

        lax.fori_loop(0, n_row_tiles, fill, 0)
        lax.fori_loop(0, n_row_tiles, fill_done, 0)

    def issue(r, c):
        row_copy(r, pos_ref[0, 0, r]).start(priority=0)
        row_copy(r, pos_ref[0, 1, r]).start(priority=1)
        return c

    def drain(r, c):
        row_copy(0, 0).wait()
        row_copy(0, 0).wait()
        return c

    lax.fori_loop(0, tb, issue, 0, unroll=8)
    lax.fori_loop(0, tb, drain, 0, unroll=8)


def _dispatch(zmask, pos, hn, s_max):
    n, n_c, _ = hn.shape
    n_tiles, _, tb = pos.shape
    return pl.pallas_call(
        functools.partial(_dispatch_kernel, tb=tb, n_row_tiles=zmask.shape[0]),
        grid_spec=pltpu.PrefetchScalarGridSpec(
            num_scalar_prefetch=1,
            grid=(n_tiles,),
            in_specs=[pl.BlockSpec((1, 2, tb), lambda i, zm: (i, 0, 0), memory_space=pltpu.SMEM),
                      pl.BlockSpec((tb, n_c, LANES), lambda i, zm: (i, 0, 0))],
            out_specs=pl.BlockSpec(memory_space=pl.ANY),
            scratch_shapes=[pltpu.VMEM((GROUP_TILE, n_c, LANES), F32), pltpu.SemaphoreType.DMA(())]),
        out_shape=jax.ShapeDtypeStruct((s_max, n_c, LANES), F32),
        compiler_params=_params("arbitrary"),
        name="moe_dispatch",
    )(zmask, pos, hn)


def _group_ffn_kernel(te_ref, tx_ref, tv_ref, x_ref, w1_ref, w3_ref, w2_ref, o_ref, xb_scr, acc_scr):
    r, f = pl.program_id(0), pl.program_id(1)
    last = pl.num_programs(1) - 1

    @pl.when(tv_ref[r] != 0)
    def _():
        @pl.when(f == 0)
        def _():
            xb_scr[...] = _load_row_tiles(x_ref, ()).astype(BF16)

        xb = xb_scr[...]
        h1 = _dot(xb, w1_ref[0, 0])
        h3 = _dot(xb, w3_ref[0, 0])
        act = (h1 * jax.nn.sigmoid(h1) * h3).astype(BF16)
        y = _dot(act, w2_ref[0, 0])

        @pl.when(f == 0)
        def _():
            acc_scr[...] = y

        @pl.when(f != 0)
        def _():
            acc_scr[...] += y

        @pl.when(f == last)
        def _():
            _store_row_tiles(o_ref, (), acc_scr[...])

    @pl.when((tv_ref[r] == 0) & (f == last))
    def _():
        o_ref[...] = jnp.zeros_like(o_ref)


def _group_ffn(te, tx, tv, xs, w1, w3, w2, layer):
    s_max, n_c, _ = xs.shape
    d = n_c * LANES
    d_ff = w1.shape[3]
    tf = _tile(d_ff, 1792)
    nf = d_ff // tf
    fidx = lambda r, f, tv: f * tv[r] + (nf - 1) * (1 - tv[r])
    return pl.pallas_call(
        _group_ffn_kernel,
        grid_spec=pltpu.PrefetchScalarGridSpec(
            num_scalar_prefetch=3,
            grid=(s_max // GROUP_TILE, nf),
            in_specs=[pl.BlockSpec((GROUP_TILE, n_c, LANES), lambda r, f, te, tx, tv: (tx[r], 0, 0)),
                      pl.BlockSpec((1, 1, d, tf), lambda r, f, te, tx, tv: (layer, te[r], 0, fidx(r, f, tv))),
                      pl.BlockSpec((1, 1, d, tf), lambda r, f, te, tx, tv: (layer, te[r], 0, fidx(r, f, tv))),
                      pl.BlockSpec((1, 1, tf, d), lambda r, f, te, tx, tv: (layer, te[r], fidx(r, f, tv), 0))],
            out_specs=pl.BlockSpec((GROUP_TILE, n_c, LANES), lambda r, f, te, tx, tv: (r, 0, 0)),
            scratch_shapes=[pltpu.VMEM((GROUP_TILE, d), BF16), pltpu.VMEM((GROUP_TILE, d), F32)]),
        out_shape=jax.ShapeDtypeStruct((s_max, n_c, LANES), F32),
        compiler_params=_params("parallel", "arbitrary"),
        name="moe_group_swiglu",
    )(te, tx, tv, xs, w1, w3, w2)


def _combine_kernel(*refs, tb, final_norm, n_x, tiles_per_seq):
    pos_ref = refs[0]
    x_refs = refs[1:1 + n_x]
    mod_ref, info_ref, fg_ref, ys_ref, out_ref, buf, sem = refs[1 + n_x:]

    def row_copy(slot, k, r):
        return pltpu.make_async_copy(ys_ref.at[slot], buf.at[k, r], sem)

    def issue(r, c):
        row_copy(pos_ref[0, 0, r], 0, r).start(priority=0)
        row_copy(pos_ref[0, 1, r], 1, r).start(priority=1)
        return c

    def drain(r, c):
        row_copy(0, 0, 0).wait()
        row_copy(0, 1, 0).wait()
        return c

    lax.fori_loop(0, tb, issue, 0, unroll=8)
    lax.fori_loop(0, tb, drain, 0, unroll=8)
    rec = info_ref[...]
    lane = lax.broadcasted_iota(jnp.int32, rec.shape, 1)
    y = (_lane_pick(rec, lane, INFO_P1) * _load_row_tiles(buf, (0,))
         + _lane_pick(rec, lane, INFO_P2) * _load_row_tiles(buf, (1,)))
    x = _token_tile(x_refs, (pl.program_id(0) % tiles_per_seq) < tiles_per_seq // 2)
    out = x + mod_ref[0, 5:6, :] * y
    if final_norm:
        out = out * lax.rsqrt(jnp.mean(out * out, axis=-1, keepdims=True) + EPS) * fg_ref[...]
    out_ref[...] = out


def _combine(pos, xs, mod, info, ys, per_batch_mod, final_g):
    bsz, t_len, d = xs[0].shape
    t_len *= len(xs)
    n = bsz * t_len
    n_tiles, _, tb = pos.shape
    tps = t_len // tb
    bidx = (lambda i: i // tps) if per_batch_mod else (lambda i: 0)
    final_norm = final_g is not None
    gain = (final_g if final_norm else jnp.ones((d,), F32)).reshape(1, d).astype(F32)
    if len(xs) == 2:
        x_specs = _half_specs(tb, d, tps, lambda i: (i // tps, i % tps))
    else:
        x_specs = [pl.BlockSpec((1, tb, d), lambda i: (i // tps, i % tps, 0))]
    return pl.pallas_call(
        functools.partial(_combine_kernel, tb=tb, final_norm=final_norm, n_x=len(xs), tiles_per_seq=tps),
        grid=(n_tiles,),
        in_specs=[pl.BlockSpec((1, 2, tb), lambda i: (i, 0, 0), memory_space=pltpu.SMEM)] + x_specs + [
                  pl.BlockSpec((1, 6, d), lambda i: (bidx(i), 0, 0)),
                  pl.BlockSpec((tb, LANES), lambda i: (i, 0)),
                  pl.BlockSpec((1, d), lambda i: (0, 0)),
                  pl.BlockSpec(memory_space=pl.ANY)],
        out_specs=pl.BlockSpec((tb, d), lambda i: (i, 0)),
        out_shape=jax.ShapeDtypeStruct((n, d), F32),
        scratch_shapes=[pltpu.VMEM((2, tb, d // LANES, LANES), F32), pltpu.SemaphoreType.DMA(())],
        compiler_params=_params("arbitrary"),
        name="moe_combine",
    )(pos, *xs, mod, info, gain, ys)


def _moe(xs, mod, ng, w_router, w1, w3, w2, layer, per_batch_mod, final_g=None):
    bsz, t_len, d = xs[0].shape
    t_len *= len(xs)
    n = bsz * t_len
    n_exp = w_router.shape[1]
    hn, info, info_t, cnt = _router(xs, mod, ng, w_router, per_batch_mod)
    info = info.reshape(n, LANES)
    e1, e2 = info_t[INFO_E1].astype(jnp.int32), info_t[INFO_E2].astype(jnp.int32)
    r1, r2 = info_t[INFO_R1].astype(jnp.int32), info_t[INFO_R2].astype(jnp.int32)
    counts = cnt[0, :n_exp].astype(jnp.int32)
    padded = ((counts + GROUP_TILE - 1) // GROUP_TILE) * GROUP_TILE
    ends = jnp.cumsum(padded)
    starts = ends - padded
    tb = _tile(n, ROUTE_TILE)
    pos = jnp.stack([(starts[e1] + r1).reshape(n // tb, tb), (starts[e2] + r2).reshape(n // tb, tb)], axis=1)
    s_max = 2 * n + n_exp * GROUP_TILE
    tile_row = jnp.arange(s_max // GROUP_TILE, dtype=jnp.int32) * GROUP_TILE
    tv = (tile_row < ends[-1]).astype(jnp.int32)
    te = jnp.minimum(jnp.searchsorted(ends, tile_row, side="right"), n_exp - 1).astype(jnp.int32)
    tx = (jnp.minimum(tile_row, ends[-1] - GROUP_TILE) // GROUP_TILE).astype(jnp.int32)
    region_end = ((tile_row + GROUP_TILE)[:, None] == ends[None, :]) & (padded > 0)[None, :]
    zmask = jnp.maximum(1 - tv, jnp.any(region_end, axis=1).astype(jnp.int32))
    rows = _dispatch(zmask, pos, hn.reshape(n, d // LANES, LANES), s_max)
    ys = _group_ffn(te, tx, tv, rows, w1.astype(BF16), w3.astype(BF16), w2.astype(BF16), layer)
    out = _combine(pos, xs, mod, info, ys, per_batch_mod, final_g)
    return out.reshape(bsz, t_len, d)


def _dft_channel_kernel(x_ref, mod_ref, ng_ref, cs_ref, y_ref, *, gw):
    hn = _norm_mod(x_ref[0], ng_ref[...], mod_ref[0, 1:2, :], mod_ref[0, 0:1, :]).astype(BF16)
    for g in range(hn.shape[1] // gw):
        y = _dot(hn[:, g * gw:(g + 1) * gw], cs_ref[...])
        y_ref[0, 0, :, g * gw:(g + 1) * gw] = y[:, :gw].astype(BF16)
        y_ref[0, 1, :, g * gw:(g + 1) * gw] = y[:, gw:].astype(BF16)


REV_TILE = 128


def _reverse_shift(src_tiles, wrap_row, m1):
    n_t = len(src_tiles)
    first = lax.broadcasted_iota(jnp.int32, (REV_TILE, 1), 0) == 0
    out = []
    for a in range(n_t):
        body = _dot(m1, src_tiles[n_t - 1 - a])
        head = wrap_row if a == 0 else src_tiles[n_t - a][0:1, :]
        out.append(jnp.where(first, head.astype(F32), body))
    return out


def _dft_fold_kernel(ya_ref, yb_ref, yn_ref, m1_ref, f_ref):
    n_t = ya_ref.shape[2] // REV_TILE
    keep = jnp.where(pl.program_id(1) == 0, 0.0, 1.0)
    for plane, sign in ((0, 1.0), (1, -1.0)):
        tiles = [yb_ref[0, plane, a * REV_TILE:(a + 1) * REV_TILE, :] for a in range(n_t)]
        wrap = yn_ref[0, plane, 0:1, :].astype(F32) * keep
        rev = _reverse_shift(tiles, wrap, m1_ref[...])
        for a in range(n_t):
            rows = slice(a * REV_TILE, (a + 1) * REV_TILE)
            f_ref[0, plane, rows, :] = (ya_ref[0, plane, rows, :].astype(F32) + sign * rev[a]).astype(BF16)


def _dft_time_kernel(wc_ref, ws_ref, wcx_ref, wsx_ref, f_ref, yh_ref, xlo_ref, xhi_ref, mod_ref, fw_ref, fb_ref,
                     m1_ref, lo_ref, hi_ref):
    rk = wc_ref.shape[0]
    ec, od = f_ref[0, 0], f_ref[0, 1]
    y_half = yh_ref[0, 0, 0:1, :].astype(F32) * ((2 * f_ref.shape[2]) ** -0.5)
    parity = lax.broadcasted_iota(jnp.int32, (rk, 1), 0) % 2
    p = _dot(wc_ref[...], ec) + jnp.where(parity == 0, 1.0, -1.0) * y_half
    q = _dot(ws_ref[...], od)
    gate = mod_ref[0, 2:3, :]

    def project(z, x):
        return x + gate * (_dot(z, fw_ref[...]) + fb_ref[...])

    lo_ref[0] = project((p - q).astype(BF16), xlo_ref[0])
    px = _dot(wcx_ref[...], ec)[0:1, :] + y_half
    qx = _dot(wsx_ref[...], od)[0:1, :]
    src = (p + q).astype(BF16)
    tiles = [src[a * REV_TILE:(a + 1) * REV_TILE, :] for a in range(rk // REV_TILE)]
    rev = _reverse_shift(tiles, (px + qx).astype(BF16), m1_ref[...])
    hi_ref[0] = project(jnp.concatenate(rev, axis=0).astype(BF16), xhi_ref[0])


def _dft_matrix(n, scale):
    idx = (np.arange(n)[:, None] * np.arange(n)[None, :]) % n
    ang = 2.0 * np.pi * idx.astype(np.float64) / n
    return np.cos(ang) * scale, np.sin(ang) * scale


def _fourier_mixer(x, mod, ng, fn_w, fn_b, per_batch_mod):
    bsz, t_len, d = x.shape
    gw = d // N_GROUPS
    cg, sg = _dft_matrix(gw, gw ** -0.5)
    cs = jnp.asarray(np.concatenate([cg, sg], axis=1).astype(np.float32)).astype(BF16)
    tm = _tile(t_len, 512)
    bidx = (lambda b: b) if per_batch_mod else (lambda b: 0)
    y = pl.pallas_call(
        functools.partial(_dft_channel_kernel, gw=gw),
        grid=(bsz, t_len // tm),
        in_specs=[pl.BlockSpec((1, tm, d), lambda b, i: (b, i, 0)),
                  pl.BlockSpec((1, 6, d), lambda b, i: (bidx(b), 0, 0)),
                  pl.BlockSpec((1, d), lambda b, i: (0, 0)),
                  pl.BlockSpec((gw, 2 * gw), lambda b, i: (0, 0))],
        out_specs=pl.BlockSpec((1, 2, tm, d), lambda b, i: (b, 0, i, 0)),
        out_shape=jax.ShapeDtypeStruct((bsz, 2, t_len, d), BF16),
        compiler_params=_params("parallel", "parallel"),
        name="dft_channel",
    )(x, mod, ng.reshape(1, d), cs)
    half = t_len // 2
    assert half % REV_TILE == 0 and half % DFT_SPLIT == 0
    pos = np.arange(REV_TILE)
    m1 = jnp.asarray((pos[None, :] == REV_TILE - pos[:, None]).astype(np.float32)).astype(BF16)
    rf = _tile(half, 512)
    nb_f = t_len // rf
    sub = 16
    folded = pl.pallas_call(
        _dft_fold_kernel,
        grid=(bsz, half // rf),
        in_specs=[pl.BlockSpec((1, 2, rf, d), lambda b, i: (b, 0, i, 0)),
                  pl.BlockSpec((1, 2, rf, d), lambda b, i: (b, 0, nb_f - 1 - i, 0)),
                  pl.BlockSpec((1, 2, sub, d), lambda b, i: (b, 0, ((nb_f - i) % nb_f) * (rf // sub), 0)),
                  pl.BlockSpec((REV_TILE, REV_TILE), lambda b, i: (0, 0))],
        out_specs=pl.BlockSpec((1, 2, rf, d), lambda b, i: (b, 0, i, 0)),
        out_shape=jax.ShapeDtypeStruct((bsz, 2, half, d), BF16),
        compiler_params=_params("parallel", "parallel"),
        name="dft_fold",
    )(y, y, y, m1)
    kk = np.arange(half + sub)[:, None]
    ang_hi = 2.0 * np.pi * ((kk * DFT_SPLIT * np.arange(half // DFT_SPLIT)[None, :]) % t_len) / t_len
    ang_lo = 2.0 * np.pi * ((kk * np.arange(DFT_SPLIT)[None, :]) % t_len) / t_len
    scale = t_len ** -0.5
    c_hi, s_hi = (jnp.asarray((f(ang_hi) * scale).astype(np.float32))[:, :, None] for f in (np.cos, np.sin))
    c_lo, s_lo = (jnp.asarray(f(ang_lo).astype(np.float32))[:, None, :] for f in (np.cos, np.sin))
    wc = (c_hi * c_lo - s_hi * s_lo).reshape(half + sub, half).astype(BF16)
    ws = (s_hi * c_lo + c_hi * s_lo).reshape(half + sub, half).astype(BF16)
    rk = _tile(half, 512)
    nk = half // rk
    lo, hi = pl.pallas_call(
        _dft_time_kernel,
        grid=(bsz, nk),
        in_specs=[pl.BlockSpec((rk, half), lambda b, i: (i, 0)),
                  pl.BlockSpec((rk, half), lambda b, i: (i, 0)),
                  pl.BlockSpec((sub, half), lambda b, i: ((i + 1) * (rk // sub), 0)),
                  pl.BlockSpec((sub, half), lambda b, i: ((i + 1) * (rk // sub), 0)),
                  pl.BlockSpec((1, 2, half, d), lambda b, i: (b, 0, 0, 0)),
                  pl.BlockSpec((1, 1, sub, d), lambda b, i: (b, 0, half // sub, 0)),
                  pl.BlockSpec((1, rk, d), lambda b, i: (b, i, 0)),
                  pl.BlockSpec((1, rk, d), lambda b, i: (b, 2 * nk - 1 - i, 0)),
                  pl.BlockSpec((1, 6, d), lambda b, i: (bidx(b), 0, 0)),
                  pl.BlockSpec((d, d), lambda b, i: (0, 0)),
                  pl.BlockSpec((1, d), lambda b, i: (0, 0)),
                  pl.BlockSpec((REV_TILE, REV_TILE), lambda b, i: (0, 0))],
        out_specs=(pl.BlockSpec((1, rk, d), lambda b, i: (b, i, 0)),
                   pl.BlockSpec((1, rk, d), lambda b, i: (b, nk - 1 - i, 0))),
        out_shape=(jax.ShapeDtypeStruct((bsz, half, d), F32), jax.ShapeDtypeStruct((bsz, half, d), F32)),
        compiler_params=_params("parallel", "parallel"),
        name="dft_time",
    )(wc, ws, wc, ws, folded, y, x, x, mod, fn_w.astype(BF16), fn_b.reshape(1, d).astype(F32), m1)
    return lo, hi


def _final_norm_kernel(x_ref, g_ref, o_ref):
    x = x_ref[0]
    o_ref[0] = x * lax.rsqrt(jnp.mean(x * x, axis=-1, keepdims=True) + EPS) * g_ref[...]


def _final_norm(x, g):
    bsz, t_len, d = x.shape
    tm = _tile(t_len, 1024)
    return pl.pallas_call(
        _final_norm_kernel,
        grid=(bsz, t_len // tm),
        in_specs=[pl.BlockSpec((1, tm, d), lambda b, i: (b, i, 0)), pl.BlockSpec((1, d), lambda b, i: (0, 0))],
        out_specs=pl.BlockSpec((1, tm, d), lambda b, i: (b, i, 0)),
        out_shape=jax.ShapeDtypeStruct((bsz, t_len, d), F32),
        compiler_params=_params("parallel", "parallel"),
        name="final_norm",
    )(x, g.reshape(1, d))


def _trunk(x, mod, cache, use_rope, per_batch_mod, p):
    bsz, t_len, d = x.shape
    depth = mod.shape[0]
    flat = (lambda a: a) if per_batch_mod else (lambda a: a.reshape(1, bsz * t_len, a.shape[-1]))
    unflat = (lambda a: a) if per_batch_mod else (lambda a: a.reshape(bsz, t_len, a.shape[-1]))
    states = []
    for i in range(depth):
        j = i // 2
        m_i = mod[i]
        if i % 2 == 0:
            q, kt, v, o, r = _mlstm_inproj(x, m_i, p["norm_g"][i, 0], p["ml_w_in"][j], p["ml_b_gate"][j],
                                           use_rope, per_batch_mod)
            outs = []
            for direction in range(2):
                if cache is None:
                    cn0 = jnp.zeros((bsz, N_HEADS // 2, 2 * HEAD_DK, 2 * HEAD_DV), F32)
                    m0 = jnp.zeros((bsz, N_HEADS, LANES), F32)
                else:
                    cn0, m0 = _pack_state(cache[0][:, j, direction], cache[1][:, j, direction],
                                          cache[2][:, j, direction])
                outs.append(_mlstm_scan(q, kt, v, r, cn0, m0, reverse=bool(direction)))
            states.append([_unpack_state(cn, m) for (_, cn, m) in outs])
            x = _mlstm_out(outs[0][0], outs[1][0], o, x, m_i, p["ml_head_g"][j], p["ml_w_out"][j], per_batch_mod)
            x = unflat(_ffn(flat(x), m_i, p["norm_g"][i, 1], p["ffn_w1"][j], p["ffn_w3"][j], p["ffn_w2"][j],
                            per_batch_mod))
        else:
            halves = _fourier_mixer(x, m_i, p["norm_g"][i, 0], p["fn_w"][j], p["fn_b"][j], per_batch_mod)
            in_place = per_batch_mod and (t_len // 2) % ROUTE_TILE == 0
            xs = halves if in_place else (flat(jnp.concatenate(halves, axis=1)),)
            closing = p["final_g"] if i == depth - 1 else None
            x = unflat(_moe(xs, m_i, p["norm_g"][i, 1], p["moe_router"][j], p["moe_w1"], p["moe_w3"],
                            p["moe_w2"], j, per_batch_mod, closing))
    if depth % 2 == 1:
        x = unflat(_final_norm(flat(x), p["final_g"]))
    return x, states


def kernel(x_prompt, x_sample, state_C, state_n, state_m, c, c_ctx, w_mod, b_mod, norm_g, final_g,
           ml_w_in, ml_b_gate, ml_head_g, ml_w_out, fn_w, fn_b, ffn_w1, ffn_w3, ffn_w2,
           moe_router, moe_w1, moe_w3, moe_w2):
    p = dict(norm_g=norm_g, final_g=final_g, ml_w_in=ml_w_in, ml_b_gate=ml_b_gate, ml_head_g=ml_head_g,
             ml_w_out=ml_w_out, fn_w=fn_w, fn_b=fn_b, ffn_w1=ffn_w1, ffn_w3=ffn_w3, ffn_w2=ffn_w2,
             moe_router=moe_router, moe_w1=moe_w1, moe_w3=moe_w3, moe_w2=moe_w2)
    depth, d = w_mod.shape[0], w_mod.shape[1]
    n_dec = c.shape[0]
    rows = ((n_dec + 1 + 7) // 8) * 8
    cond = jnp.zeros((rows, d), F32).at[:n_dec].set(c).at[n_dec].set(c_ctx)
    mod = _mod_table(cond, w_mod, b_mod).reshape(depth, rows, 6, d)
    y_prompt, st = _trunk(x_prompt, mod[:, n_dec:n_dec + 1], None, False, False, p)
    y_sample, _ = _trunk(x_sample, mod[:, :n_dec], (state_C, state_n, state_m), True, True, p)
    new_c = jnp.stack([jnp.stack([s[0][0], s[1][0]], axis=1) for s in st], axis=1)
    new_n = jnp.stack([jnp.stack([s[0][1], s[1][1]], axis=1) for s in st], axis=1)
    new_m = jnp.stack([jnp.stack([s[0][2], s[1][2]], axis=1) for s in st], axis=1)
    return (y_prompt, y_sample, new_c.astype(x_prompt.dtype), new_n.astype(x_prompt.dtype),
            new_m.astype(x_prompt.dtype))
```

```python
import functools

import numpy as np
import jax
import jax.numpy as jnp
from jax import lax
from jax.experimental import pallas as pl
from jax.experimental.pallas import tpu as pltpu

F32 = jnp.float32
BF16 = jnp.bfloat16
HIGHEST = lax.Precision.HIGHEST

EPS = 1e-6
N_HEADS = 8
HEAD_DK = 64
HEAD_DV = 128
QK_W = N_HEADS * HEAD_DK
V_W = N_HEADS * HEAD_DV
GRID_W = 64
ROPE_BASE = 10000.0
N_GROUPS = 4
DFT_SPLIT = 64
N_GATES = 4 * N_HEADS
LANES = 128
SCAN_CHUNK = 128
SCAN_ROWS = 3 * N_HEADS
VMEM_LIMIT = 56 * 1024 * 1024
NEG_INF = float("-inf")


def _params(*sem):
    return pltpu.CompilerParams(dimension_semantics=sem, vmem_limit_bytes=VMEM_LIMIT)


def _tile(n, pref):
    t = min(n, pref)
    assert n % t == 0, (n, pref)
    return t


def _norm_mod(x, gain, scale, shift):
    ms = jnp.mean(x * x, axis=-1, keepdims=True)
    return x * lax.rsqrt(ms + EPS) * gain * (1.0 + scale) + shift


def _dot(a, b):
    return jnp.dot(a, b, preferred_element_type=F32)


def _dot_nt(a, b):
    return lax.dot_general(a, b, (((1,), (1,)), ((), ())), preferred_element_type=F32)


def _exact_rows_dot(x, ones_mat):
    hi = x.astype(BF16)
    rest = x - hi.astype(F32)
    mid = rest.astype(BF16)
    lo = (rest - mid.astype(F32)).astype(BF16)
    n = x.shape[0]
    out = _dot(jnp.concatenate([hi, mid, lo, jnp.zeros_like(hi)], axis=0), ones_mat)
    return out[0:n] + out[n:2 * n] + out[2 * n:3 * n]


def _log_sigmoid(x):
    return jnp.minimum(x, 0.0) - jnp.log1p(jnp.exp(-jnp.abs(x)))


def _mod_kernel(c_ref, w_ref, b_ref, o_ref):
    c = c_ref[...]
    s = c * jax.nn.sigmoid(c)
    o_ref[0] = jnp.dot(s, w_ref[0], preferred_element_type=F32, precision=HIGHEST) + b_ref[0]


def _mod_table(cond, w_mod, b_mod):
    depth, d, n = w_mod.shape
    rows = cond.shape[0]
    tn = _tile(n, 1536)
    return pl.pallas_call(
        _mod_kernel,
        grid=(depth, n // tn),
        in_specs=[pl.BlockSpec((rows, d), lambda l, j: (0, 0)),
                  pl.BlockSpec((1, d, tn), lambda l, j: (l, 0, j)),
                  pl.BlockSpec((1, 1, tn), lambda l, j: (l, 0, j))],
        out_specs=pl.BlockSpec((1, rows, tn), lambda l, j: (l, 0, j)),
        out_shape=jax.ShapeDtypeStruct((depth, rows, n), F32),
        compiler_params=_params("parallel", "parallel"),
        name="adaln_table",
    )(cond, w_mod, b_mod.reshape(depth, 1, n))


def _rope_tables(t_len):
    pos = np.arange(t_len)
    row = (pos // GRID_W).astype(np.float32)
    col = (pos % GRID_W).astype(np.float32)
    nf = HEAD_DK // 4
    inv = (np.float32(ROPE_BASE) ** (-np.arange(nf, dtype=np.float32) / nf)).astype(np.float32)
    d = np.arange(HEAD_DK)
    p = np.where(d[None, :] < HEAD_DK // 2, row[:, None], col[:, None]).astype(np.float32)
    ang = p * inv[d % nf][None, :]
    sign = np.where((d % (2 * nf)) < nf, -1.0, 1.0).astype(np.float32)
    return np.cos(ang).astype(np.float32), (np.sin(ang) * sign[None, :]).astype(np.float32)


def _inproj_kernel(*refs, use_rope):
    if use_rope:
        (x_ref, mod_ref, ng_ref, wq_ref, wkt_ref, wv_ref, wo_ref, wgt_ref, bg_ref, trif_ref, trib_ref,
         cq_ref, sq_ref, ck_ref, sk_ref, q_ref, kt_ref, v_ref, o_ref, r_ref) = refs
    else:
        (x_ref, mod_ref, ng_ref, wq_ref, wkt_ref, wv_ref, wo_ref, wgt_ref, bg_ref, trif_ref, trib_ref,
         q_ref, kt_ref, v_ref, o_ref, r_ref) = refs
    hn = _norm_mod(x_ref[0], ng_ref[...], mod_ref[0, 1:2, :], mod_ref[0, 0:1, :])
    hb = hn.astype(BF16)
    q = _dot(hb, wq_ref[...]) * (HEAD_DK ** -0.5)
    kt = _dot_nt(wkt_ref[...], hb)
    if use_rope:
        nf = HEAD_DK // 4
        lane = lax.broadcasted_iota(jnp.int32, (q.shape[0], LANES), 1)
        first_q = (lane % (2 * nf)) < nf
        sub = lax.broadcasted_iota(jnp.int32, (LANES, kt.shape[1]), 0)
        first_k = (sub % (2 * nf)) < nf
        cq, sq, ck, sk = cq_ref[...], sq_ref[...], ck_ref[...], sk_ref[...]
        for s in range(QK_W // LANES):
            qs = q[:, s * LANES:(s + 1) * LANES]
            sw = jnp.where(first_q, pltpu.roll(qs, LANES - nf, 1), pltpu.roll(qs, nf, 1))
            q_ref[0, :, s * LANES:(s + 1) * LANES] = (qs * cq + sw * sq).astype(BF16)
            ks = kt[s * LANES:(s + 1) * LANES, :]
            sw = jnp.where(first_k, pltpu.roll(ks, LANES - nf, 0), pltpu.roll(ks, nf, 0))
            kt_ref[0, s * LANES:(s + 1) * LANES, :] = (ks * ck + sw * sk).astype(BF16)
    else:
        q_ref[0] = q.astype(BF16)
        kt_ref[0] = kt.astype(BF16)
    v_ref[0] = _dot(hb, wv_ref[...]).astype(BF16)
    o_ref[0] = _dot(hb, wo_ref[...]).astype(BF16)
    gt = _dot_nt(wgt_ref[...], hb) + bg_ref[...]
    h = N_HEADS
    i_f, f_f = gt[0:h], _log_sigmoid(gt[h:2 * h])
    i_b, f_b = gt[2 * h:3 * h], _log_sigmoid(gt[3 * h:4 * h])
    b_f = _exact_rows_dot(f_f, trif_ref[...])
    b_b = _exact_rows_dot(f_b, trib_ref[...])
    a_f, a_b = i_f - b_f, i_b - b_b
    width = a_f.shape[1]
    in_chunk = lax.broadcasted_iota(jnp.int32, a_f.shape, 1) % SCAN_CHUNK
    c_f, c_b = a_f, a_b
    step = 1
    while step < SCAN_CHUNK:
        c_f = jnp.where(in_chunk >= step, jnp.maximum(c_f, pltpu.roll(c_f, step, 1)), c_f)
        c_b = jnp.where(in_chunk < SCAN_CHUNK - step, jnp.maximum(c_b, pltpu.roll(c_b, width - step, 1)), c_b)
        step *= 2
    for k, rows in enumerate((a_f, b_f, c_f, a_b, b_b, c_b)):
        r_ref[0, k * h:(k + 1) * h, :] = rows


def _mlstm_inproj(x, mod, ng, w_in, b_gate, use_rope, per_batch_mod):
    bsz, t_len, d = x.shape
    tm = _tile(t_len, 512)
    wq = w_in[:, :QK_W].astype(BF16)
    wkt = w_in[:, QK_W:2 * QK_W].T.astype(BF16)
    wv = w_in[:, 2 * QK_W:2 * QK_W + V_W].astype(BF16)
    wo = w_in[:, 2 * QK_W + V_W:2 * QK_W + 2 * V_W].astype(BF16)
    wgt = w_in[:, 2 * QK_W + 2 * V_W:].T.astype(BF16)
    bg = b_gate.reshape(N_GATES, 1).astype(F32)
    pos = np.arange(tm)
    same = (pos[:, None] // SCAN_CHUNK) == (pos[None, :] // SCAN_CHUNK)
    trif = jnp.asarray((same & (pos[:, None] <= pos[None, :])).astype(np.float32)).astype(BF16)
    trib = jnp.asarray((same & (pos[:, None] >= pos[None, :])).astype(np.float32)).astype(BF16)
    bidx = (lambda b: b) if per_batch_mod else (lambda b: 0)
    const = lambda shp: pl.BlockSpec(shp, lambda b, i: (0,) * len(shp))
    in_specs = [pl.BlockSpec((1, tm, d), lambda b, i: (b, i, 0)),
                pl.BlockSpec((1, 6, d), lambda b, i: (bidx(b), 0, 0)),
                const((1, d)), const((d, QK_W)), const((QK_W, d)), const((d, V_W)), const((d, V_W)),
                const((N_GATES, d)), const((N_GATES, 1)), const((tm, tm)), const((tm, tm))]
    args = [x, mod, ng.reshape(1, d), wq, wkt, wv, wo, wgt, bg, trif, trib]
    if use_rope:
        cos, sin = _rope_tables(t_len)
        rep = LANES // HEAD_DK
        args += [jnp.asarray(np.tile(cos, (1, rep))), jnp.asarray(np.tile(sin, (1, rep))),
                 jnp.asarray(np.tile(cos.T, (rep, 1))), jnp.asarray(np.tile(sin.T, (rep, 1)))]
        in_specs += [pl.BlockSpec((tm, LANES), lambda b, i: (i, 0)), pl.BlockSpec((tm, LANES), lambda b, i: (i, 0)),
                     pl.BlockSpec((LANES, tm), lambda b, i: (0, i)), pl.BlockSpec((LANES, tm), lambda b, i: (0, i))]
    out_shape = (jax.ShapeDtypeStruct((bsz, t_len, QK_W), BF16),
                 jax.ShapeDtypeStruct((bsz, QK_W, t_len), BF16),
                 jax.ShapeDtypeStruct((bsz, t_len, V_W), BF16),
                 jax.ShapeDtypeStruct((bsz, t_len, V_W), BF16),
                 jax.ShapeDtypeStruct((bsz, 2 * SCAN_ROWS, t_len), F32))
    out_specs = (pl.BlockSpec((1, tm, QK_W), lambda b, i: (b, i, 0)),
                 pl.BlockSpec((1, QK_W, tm), lambda b, i: (b, 0, i)),
                 pl.BlockSpec((1, tm, V_W), lambda b, i: (b, i, 0)),
                 pl.BlockSpec((1, tm, V_W), lambda b, i: (b, i, 0)),
                 pl.BlockSpec((1, 2 * SCAN_ROWS, tm), lambda b, i: (b, 0, i)))
    return pl.pallas_call(
        functools.partial(_inproj_kernel, use_rope=use_rope),
        grid=(bsz, t_len // tm), in_specs=in_specs, out_specs=out_specs, out_shape=out_shape,
        compiler_params=_params("parallel", "parallel"),
        name="mlstm_inproj",
    )(*args)


def _scan_kernel(q_ref, kt_ref, v_ref, r_ref, cn0_ref, m0_ref, spread_ref, h_ref, cn_out_ref, m_out_ref,
                 cn_scr, m_scr, *, reverse, n_chunks):
    L = SCAN_CHUNK
    j = pl.program_id(1)

    @pl.when(j == 0)
    def _():
        cn_scr[...] = cn0_ref[0]
        m_scr[...] = m0_ref[0]

    row_i = lax.broadcasted_iota(jnp.int32, (L, L), 0)
    col_i = lax.broadcasted_iota(jnp.int32, (L, L), 1)
    visible = (col_i >= row_i) if reverse else (col_i <= row_i)
    upper_lanes = lax.broadcasted_iota(jnp.int32, (L, LANES), 1) >= HEAD_DK
    ones = jnp.ones((L, LANES), BF16)
    no_rows = jnp.zeros((HEAD_DK, 2 * HEAD_DV), BF16)
    end_lane = lax.broadcasted_iota(jnp.int32, (1, L), 1) == (0 if reverse else L - 1)
    order = range(n_chunks - 1, -1, -1) if reverse else range(n_chunks)
    head_rows = [slice((h % 2) * HEAD_DK, (h % 2 + 1) * HEAD_DK) for h in range(N_HEADS)]
    states = [(cn_scr[h // 2, head_rows[h], :], m_scr[h:h + 1, 0:1]) for h in range(N_HEADS)]

    def columns(c):
        x = r_ref[0, N_HEADS:3 * N_HEADS, c * L:(c + 1) * L]
        hi = x.astype(BF16).astype(F32)
        mid = (x - hi).astype(BF16).astype(F32)
        lo = x - hi - mid
        parts = jnp.concatenate([hi, mid, lo, jnp.zeros((L - 6 * N_HEADS, L), F32)], axis=0)
        return _dot(parts.T.astype(BF16), spread_ref[...])

    cols = {c: columns(c) for c in order}
    new_states = []
    for h in range(N_HEADS):
        p, e = h // 2, h % 2
        cn, m = states[h]
        for c in order:
            r0 = c * L
            q_pair = q_ref[0, r0:r0 + L, p * LANES:(p + 1) * LANES]
            q_m = jnp.where(upper_lanes if e else jnp.logical_not(upper_lanes), q_pair, jnp.zeros_like(q_pair))
            kt_pair = kt_ref[0, p * LANES:(p + 1) * LANES, r0:r0 + L]
            kt_h = kt_ref[0, h * HEAD_DK:(h + 1) * HEAD_DK, r0:r0 + L]
            a_row = r_ref[0, h:h + 1, r0:r0 + L]
            b_row = r_ref[0, N_HEADS + h:N_HEADS + h + 1, r0:r0 + L]
            v_aug = jnp.concatenate([v_ref[0, r0:r0 + L, h * HEAD_DV:(h + 1) * HEAD_DV], ones], axis=1)
            am = jnp.where(visible, a_row, NEG_INF)
            a_max = jnp.max(a_row, axis=1, keepdims=True)
            kw = (kt_h.astype(F32) * jnp.exp(a_row - a_max)).astype(BF16)
            upd = _dot(kw, v_aug)
            b_col = cols[c][:, h * LANES:(h + 1) * LANES]
            b_end = jnp.sum(jnp.where(end_lane, b_row, 0.0), axis=1, keepdims=True)
            u = jnp.maximum(m, cols[c][:, (N_HEADS + h) * LANES:(N_HEADS + h + 1) * LANES])
            guard = jnp.exp(-(b_col + u))
            s = (_dot(q_m, kt_pair) * jnp.exp(am - u)).astype(BF16)
            q_old = (q_m.astype(F32) * jnp.exp(m - u)).astype(BF16)
            cn_b = cn.astype(BF16)
            cn_pair = jnp.concatenate([no_rows, cn_b] if e else [cn_b, no_rows], axis=0)
            both = _dot(jnp.concatenate([q_old, s], axis=1), jnp.concatenate([cn_pair, v_aug], axis=0))
            num, den = both[:, :HEAD_DV], both[:, HEAD_DV:]
            h_ref[0, r0:r0 + L, h * HEAD_DV:(h + 1) * HEAD_DV] = (num / jnp.maximum(jnp.abs(den), guard)).astype(BF16)
            u_end = jnp.maximum(m, a_max)
            cn = jnp.exp(m - u_end) * cn + jnp.exp(a_max - u_end) * upd
            m = b_end + u_end
        new_states.append((cn, m))
    for h in range(N_HEADS):
        cn_scr[h // 2, head_rows[h], :] = new_states[h][0]
        m_scr[h:h + 1, :] = jnp.broadcast_to(new_states[h][1], (1, LANES))

    @pl.when(j == pl.num_programs(1) - 1)
    def _():
        cn_out_ref[0] = cn_scr[...]
        m_out_ref[0] = m_scr[...]


def _mlstm_scan(q, kt, v, r, cn0, m0, reverse):
    bsz, t_len, _ = q.shape
    tb = _tile(t_len, 4 * SCAN_CHUNK)
    nblk = t_len // tb
    blk = (lambda j: nblk - 1 - j) if reverse else (lambda j: j)
    d = 1 if reverse else 0
    half = N_HEADS // 2
    src = np.arange(LANES)[:, None]
    dst = np.arange(2 * N_HEADS * LANES)[None, :]
    spread = jnp.asarray(((src < 6 * N_HEADS) & (src % (2 * N_HEADS) == dst // LANES)).astype(np.float32)).astype(BF16)
    return pl.pallas_call(
        functools.partial(_scan_kernel, reverse=reverse, n_chunks=tb // SCAN_CHUNK),
        grid=(bsz, nblk),
        in_specs=[pl.BlockSpec((1, tb, QK_W), lambda b, j: (b, blk(j), 0)),
                  pl.BlockSpec((1, QK_W, tb), lambda b, j: (b, 0, blk(j))),
                  pl.BlockSpec((1, tb, V_W), lambda b, j: (b, blk(j), 0)),
                  pl.BlockSpec((1, SCAN_ROWS, tb), lambda b, j: (b, d, blk(j))),
                  pl.BlockSpec((1, half, 2 * HEAD_DK, 2 * HEAD_DV), lambda b, j: (b, 0, 0, 0)),
                  pl.BlockSpec((1, N_HEADS, LANES), lambda b, j: (b, 0, 0)),
                  pl.BlockSpec((LANES, 2 * N_HEADS * LANES), lambda b, j: (0, 0))],
        out_specs=(pl.BlockSpec((1, tb, V_W), lambda b, j: (b, blk(j), 0)),
                   pl.BlockSpec((1, half, 2 * HEAD_DK, 2 * HEAD_DV), lambda b, j: (b, 0, 0, 0)),
                   pl.BlockSpec((1, N_HEADS, LANES), lambda b, j: (b, 0, 0))),
        out_shape=(jax.ShapeDtypeStruct((bsz, t_len, V_W), BF16),
                   jax.ShapeDtypeStruct((bsz, half, 2 * HEAD_DK, 2 * HEAD_DV), F32),
                   jax.ShapeDtypeStruct((bsz, N_HEADS, LANES), F32)),
        scratch_shapes=[pltpu.VMEM((half, 2 * HEAD_DK, 2 * HEAD_DV), F32),
                        pltpu.VMEM((N_HEADS, LANES), F32)],
        compiler_params=_params("parallel", "arbitrary"),
        name="mlstm_scan_bwd" if reverse else "mlstm_scan_fwd",
    )(q, kt, v, r, cn0, m0, spread)


def _pack_state(c, n, m):
    bsz = c.shape[0]
    cn = jnp.concatenate([c, jnp.broadcast_to(n[..., None], n.shape + (HEAD_DV,))], axis=-1)
    cn = cn.reshape(bsz, N_HEADS // 2, 2 * HEAD_DK, 2 * HEAD_DV)
    return cn.astype(F32), jnp.broadcast_to(m[..., None], (bsz, N_HEADS, LANES)).astype(F32)


def _unpack_state(cn, m):
    bsz = cn.shape[0]
    cn = cn.reshape(bsz, N_HEADS, HEAD_DK, 2 * HEAD_DV)
    return cn[..., :HEAD_DV], cn[..., HEAD_DV], m[..., 0]


def _mlstm_out_kernel(hf_ref, hb_ref, o_ref, x_ref, mod_ref, hg_ref, w_ref, out_ref):
    hs = hf_ref[0].astype(F32) + hb_ref[0].astype(F32)
    parts = []
    for h in range(N_HEADS):
        z = hs[:, h * HEAD_DV:(h + 1) * HEAD_DV]
        parts.append(z * lax.rsqrt(jnp.mean(z * z, axis=-1, keepdims=True) + EPS))
    hn = jnp.concatenate(parts, axis=1) * hg_ref[...] * jax.nn.sigmoid(o_ref[0].astype(F32))
    y = _dot(hn.astype(BF16), w_ref[...])
    out_ref[0] = x_ref[0] + mod_ref[0, 2:3, :] * y


def _mlstm_out(hf, hb, o, x, mod, head_g, w_out, per_batch_mod):
    bsz, t_len, d = x.shape
    tm = _tile(t_len, 512)
    bidx = (lambda b: b) if per_batch_mod else (lambda b: 0)
    tok = lambda w: pl.BlockSpec((1, tm, w), lambda b, i: (b, i, 0))
    return pl.pallas_call(
        _mlstm_out_kernel,
        grid=(bsz, t_len // tm),
        in_specs=[tok(V_W), tok(V_W), tok(V_W), tok(d),
                  pl.BlockSpec((1, 6, d), lambda b, i: (bidx(b), 0, 0)),
                  pl.BlockSpec((1, V_W), lambda b, i: (0, 0)),
                  pl.BlockSpec((V_W, d), lambda b, i: (0, 0))],
        out_specs=tok(d),
        out_shape=jax.ShapeDtypeStruct((bsz, t_len, d), F32),
        compiler_params=_params("parallel", "parallel"),
        name="mlstm_out",
    )(hf, hb, o, x, mod, head_g.reshape(1, V_W).astype(F32), w_out.astype(BF16))


def _ffn_kernel(x_ref, mod_ref, ng_ref, w1_ref, w3_ref, w2_ref, out_ref, hn_scr, acc_scr):
    f = pl.program_id(2)

    @pl.when(f == 0)
    def _():
        hn = _norm_mod(x_ref[0], ng_ref[...], mod_ref[0, 4:5, :], mod_ref[0, 3:4, :])
        hn_scr[...] = hn.astype(BF16)
        acc_scr[...] = jnp.zeros_like(acc_scr)

    hb = hn_scr[...]
    h1 = _dot(hb, w1_ref[...])
    h3 = _dot(hb, w3_ref[...])
    act = (h1 * jax.nn.sigmoid(h1) * h3).astype(BF16)
    acc_scr[...] += _dot(act, w2_ref[...])

    @pl.when(f == pl.num_programs(2) - 1)
    def _():
        out_ref[0] = x_ref[0] + mod_ref[0, 5:6, :] * acc_scr[...]


def _ffn(x, mod, ng, w1, w3, w2, per_batch_mod):
    bsz, t_len, d = x.shape
    d_ff = w1.shape[1]
    tm = _tile(t_len, 512)
    tf = _tile(d_ff, 1792)
    bidx = (lambda b: b) if per_batch_mod else (lambda b: 0)
    return pl.pallas_call(
        _ffn_kernel,
        grid=(bsz, t_len // tm, d_ff // tf),
        in_specs=[pl.BlockSpec((1, tm, d), lambda b, i, f: (b, i, 0)),
                  pl.BlockSpec((1, 6, d), lambda b, i, f: (bidx(b), 0, 0)),
                  pl.BlockSpec((1, d), lambda b, i, f: (0, 0)),
                  pl.BlockSpec((d, tf), lambda b, i, f: (0, f)),
                  pl.BlockSpec((d, tf), lambda b, i, f: (0, f)),
                  pl.BlockSpec((tf, d), lambda b, i, f: (f, 0))],
        out_specs=pl.BlockSpec((1, tm, d), lambda b, i, f: (b, i, 0)),
        out_shape=jax.ShapeDtypeStruct((bsz, t_len, d), F32),
        scratch_shapes=[pltpu.VMEM((tm, d), BF16), pltpu.VMEM((tm, d), F32)],
        compiler_params=_params("parallel", "parallel", "arbitrary"),
        name="dense_swiglu",
    )(x, mod, ng.reshape(1, d), w1.astype(BF16), w3.astype(BF16), w2.astype(BF16))


ROUTE_TILE = 512
MOVE_TILE = 1024
GROUP_TILE = 512
INFO_E1, INFO_E2, INFO_P1, INFO_P2, INFO_R1, INFO_R2 = range(6)
INFO_ROWS = 8


def _lane_pick(rec, lane, k):
    return jnp.sum(jnp.where(lane == k, rec, 0.0), axis=-1, keepdims=True)


def _token_tile(x_refs, first_half):
    if len(x_refs) == 1:
        return x_refs[0][0]
    return jnp.where(first_half, x_refs[0][0], x_refs[1][0])


def _half_specs(tm, d, tiles_per_seq, locate):
    nh = tiles_per_seq // 2

    def lo(*g):
        b, i = locate(*g)
        return (b, jnp.minimum(i, nh - 1), 0)

    def hi(*g):
        b, i = locate(*g)
        return (b, jnp.maximum(i - nh, 0), 0)

    return [pl.BlockSpec((1, tm, d), lo), pl.BlockSpec((1, tm, d), hi)]


def _router_kernel(*refs, n_exp, n_x):
    x_refs = refs[:n_x]
    mod_ref, ng_ref, wr_ref, tri_ref, hn_ref, info_ref, infot_ref, cnt_ref, carry = refs[n_x:]

    @pl.when((pl.program_id(0) == 0) & (pl.program_id(1) == 0))
    def _():
        carry[...] = jnp.zeros_like(carry)

    x = _token_tile(x_refs, pl.program_id(1) < pl.num_programs(1) // 2)
    hn = _norm_mod(x, ng_ref[...], mod_ref[0, 4:5, :], mod_ref[0, 3:4, :])
    _store_row_tiles(hn_ref, (0,), hn)
    hn_hi = hn.astype(BF16)
    hn_lo = (hn - hn_hi.astype(F32)).astype(BF16)
    hh = _dot(hn_hi, wr_ref[...])
    logits = hh[:, :LANES] + hh[:, LANES:] + _dot(hn_lo, wr_ref[:, :LANES])
    lane = lax.broadcasted_iota(jnp.int32, logits.shape, 1).astype(F32)
    logits = jnp.where(lane < n_exp, logits, NEG_INF)
    v1 = jnp.max(logits, axis=-1, keepdims=True)
    i1 = jnp.min(jnp.where(logits == v1, lane, float(LANES)), axis=-1, keepdims=True)
    rest = jnp.where(lane == i1, NEG_INF, logits)
    v2 = jnp.max(rest, axis=-1, keepdims=True)
    i2 = jnp.min(jnp.where(rest == v2, lane, float(LANES)), axis=-1, keepdims=True)
    e2 = jnp.exp(v2 - v1)
    den = 1.0 + e2
    chosen = jnp.where((lane == i1) | (lane == i2), 1.0, 0.0)
    before = _dot(tri_ref[...], chosen.astype(BF16)) + carry[...]
    r1 = jnp.sum(jnp.where(lane == i1, before, 0.0), axis=-1, keepdims=True)
    r2 = jnp.sum(jnp.where(lane == i2, before, 0.0), axis=-1, keepdims=True)
    total = carry[...] + jnp.sum(chosen, axis=0, keepdims=True)
    carry[...] = total
    cnt_ref[...] = total
    rec = jnp.zeros_like(logits)
    for k, val in ((INFO_E1, i1), (INFO_E2, i2), (INFO_P1, 1.0 / den), (INFO_P2, e2 / den),
                   (INFO_R1, r1), (INFO_R2, r2)):
        rec = jnp.where(lane == k, val, rec)
    info_ref[0] = rec
    infot_ref[...] = rec.T[0:INFO_ROWS, :]


def _router(xs, mod, ng, w_router, per_batch_mod):
    bsz, t_len, d = xs[0].shape
    t_len *= len(xs)
    n_exp = w_router.shape[1]
    tm = _tile(t_len, ROUTE_TILE)
    wr = jnp.zeros((d, LANES), F32).at[:, :n_exp].set(w_router)
    wr_hi = wr.astype(BF16)
    wr = jnp.concatenate([wr_hi, (wr - wr_hi.astype(F32)).astype(BF16)], axis=1)
    pos = np.arange(tm)
    tri = jnp.asarray((pos[None, :] < pos[:, None]).astype(np.float32)).astype(BF16)
    bidx = (lambda b: b) if per_batch_mod else (lambda b: 0)
    if len(xs) == 2:
        x_specs = _half_specs(tm, d, t_len // tm, lambda b, i: (b, i))
    else:
        x_specs = [pl.BlockSpec((1, tm, d), lambda b, i: (b, i, 0))]
    return pl.pallas_call(
        functools.partial(_router_kernel, n_exp=n_exp, n_x=len(xs)),
        grid=(bsz, t_len // tm),
        in_specs=x_specs + [
                  pl.BlockSpec((1, 6, d), lambda b, i: (bidx(b), 0, 0)),
                  pl.BlockSpec((1, d), lambda b, i: (0, 0)),
                  pl.BlockSpec((d, 2 * LANES), lambda b, i: (0, 0)),
                  pl.BlockSpec((tm, tm), lambda b, i: (0, 0))],
        out_specs=(pl.BlockSpec((1, tm, d // LANES, LANES), lambda b, i: (b, i, 0, 0)),
                   pl.BlockSpec((1, tm, LANES), lambda b, i: (b, i, 0)),
                   pl.BlockSpec((INFO_ROWS, tm), lambda b, i: (0, b * (t_len // tm) + i)),
                   pl.BlockSpec((1, LANES), lambda b, i: (0, 0))),
        out_shape=(jax.ShapeDtypeStruct((bsz, t_len, d // LANES, LANES), F32),
                   jax.ShapeDtypeStruct((bsz, t_len, LANES), F32),
                   jax.ShapeDtypeStruct((INFO_ROWS, bsz * t_len), F32),
                   jax.ShapeDtypeStruct((1, LANES), F32)),
        scratch_shapes=[pltpu.VMEM((1, LANES), F32)],
        compiler_params=_params("arbitrary", "arbitrary"),
        name="moe_router",
    )(*xs, mod, ng.reshape(1, d), wr, tri)


def _store_row_tiles(ref, lead, val):
    chunks = jnp.stack([val[:, c * LANES:(c + 1) * LANES] for c in range(val.shape[1] // LANES)], axis=0)
    ref[lead] = jnp.swapaxes(chunks, 0, 1)


def _load_row_tiles(ref, lead):
    n_c = ref.shape[-2]
    x = jnp.swapaxes(ref[lead], 0, 1)
    return jnp.concatenate([x[c] for c in range(n_c)], axis=1)


def _dispatch_kernel(zmask_ref, pos_ref, hn_ref, xs_ref, zeros, sem, *, tb, n_row_tiles):
    def row_copy(r, slot):
        return pltpu.make_async_copy(hn_ref.at[r], xs_ref.at[slot], sem)

    def tile_fill(r):
        row0 = pl.multiple_of(r * GROUP_TILE, GROUP_TILE)
        return pltpu.make_async_copy(zeros, xs_ref.at[pl.ds(row0, GROUP_TILE)], sem)

    @pl.when(pl.program_id(0) == 0)
    def _():
        zeros[...] = jnp.zeros_like(zeros)

        def fill(r, c):
            @pl.when(zmask_ref[r] != 0)
            def _():
                tile_fill(r).start()
            return c

        def fill_done(r, c):
            @pl.when(zmask_ref[r] != 0)
            def _():
                tile_fill(r).wait()
            return c

        lax.fori_loop(0, n_row_tiles, fill, 0)
        lax.fori_loop(0, n_row_tiles, fill_done, 0)

    def issue(r, c):
        row_copy(r, pos_ref[0, 0, r]).start(priority=0)
        row_copy(r, pos_ref[0, 1, r]).start(priority=1)
        return c

    def drain(r, c):
        row_copy(0, 0).wait()
        row_copy(0, 0).wait()
        return c

    lax.fori_loop(0, tb, issue, 0, unroll=8)
    lax.fori_loop(0, tb, drain, 0, unroll=8)


def _dispatch(zmask, pos, hn, s_max):
    n, n_c, _ = hn.shape
    n_tiles, _, tb = pos.shape
    return pl.pallas_call(
        functools.partial(_dispatch_kernel, tb=tb, n_row_tiles=zmask.shape[0]),
        grid_spec=pltpu.PrefetchScalarGridSpec(
            num_scalar_prefetch=1,
            grid=(n_tiles,),
            in_specs=[pl.BlockSpec((1, 2, tb), lambda i, zm: (i, 0, 0), memory_space=pltpu.SMEM),
                      pl.BlockSpec((tb, n_c, LANES), lambda i, zm: (i, 0, 0))],
            out_specs=pl.BlockSpec(memory_space=pl.ANY),
            scratch_shapes=[pltpu.VMEM((GROUP_TILE, n_c, LANES), F32), pltpu.SemaphoreType.DMA(())]),
        out_shape=jax.ShapeDtypeStruct((s_max, n_c, LANES), F32),
        compiler_params=_params("arbitrary"),
        name="moe_dispatch",
    )(zmask, pos, hn)


def _group_ffn_kernel(te_ref, tx_ref, tv_ref, x_ref, w1_ref, w3_ref, w2_ref, o_ref, xb_scr, acc_scr):
    r, f = pl.program_id(0), pl.program_id(1)
    last = pl.num_programs(1) - 1

    @pl.when(tv_ref[r] != 0)
    def _():
        @pl.when(f == 0)
        def _():
            xb_scr[...] = _load_row_tiles(x_ref, ()).astype(BF16)

        xb = xb_scr[...]
        h1 = _dot(xb, w1_ref[0, 0])
        h3 = _dot(xb, w3_ref[0, 0])
        act = (h1 * jax.nn.sigmoid(h1) * h3).astype(BF16)
        y = _dot(act, w2_ref[0, 0])

        @pl.when(f == 0)
        def _():
            acc_scr[...] = y

        @pl.when(f != 0)
        def _():
            acc_scr[...] += y

        @pl.when(f == last)
        def _():
            _store_row_tiles(o_ref, (), acc_scr[...])

    @pl.when((tv_ref[r] == 0) & (f == last))
    def _():
        o_ref[...] = jnp.zeros_like(o_ref)


def _group_ffn(te, tx, tv, xs, w1, w3, w2, layer):
    s_max, n_c, _ = xs.shape
    d = n_c * LANES
    d_ff = w1.shape[3]
    tf = _tile(d_ff, 1792)
    nf = d_ff // tf
    fidx = lambda r, f, tv: f * tv[r] + (nf - 1) * (1 - tv[r])
    return pl.pallas_call(
        _group_ffn_kernel,
        grid_spec=pltpu.PrefetchScalarGridSpec(
            num_scalar_prefetch=3,
            grid=(s_max // GROUP_TILE, nf),
            in_specs=[pl.BlockSpec((GROUP_TILE, n_c, LANES), lambda r, f, te, tx, tv: (tx[r], 0, 0)),
                      pl.BlockSpec((1, 1, d, tf), lambda r, f, te, tx, tv: (layer, te[r], 0, fidx(r, f, tv))),
                      pl.BlockSpec((1, 1, d, tf), lambda r, f, te, tx, tv: (layer, te[r], 0, fidx(r, f, tv))),
                      pl.BlockSpec((1, 1, tf, d), lambda r, f, te, tx, tv: (layer, te[r], fidx(r, f, tv), 0))],
            out_specs=pl.BlockSpec((GROUP_TILE, n_c, LANES), lambda r, f, te, tx, tv: (r, 0, 0)),
            scratch_shapes=[pltpu.VMEM((GROUP_TILE, d), BF16), pltpu.VMEM((GROUP_TILE, d), F32)]),
        out_shape=jax.ShapeDtypeStruct((s_max, n_c, LANES), F32),
        compiler_params=_params("parallel", "arbitrary"),
        name="moe_group_swiglu",
    )(te, tx, tv, xs, w1, w3, w2)


def _combine_kernel(*refs, tb, final_norm, n_x, tiles_per_seq):
    pos_ref = refs[0]
    x_refs = refs[1:1 + n_x]
    mod_ref, info_ref, fg_ref, ys_ref, out_ref, buf, sem = refs[1 + n_x:]

    def row_copy(slot, k, r):
        return pltpu.make_async_copy(ys_ref.at[slot], buf.at[k, r], sem)

    def issue(r, c):
        row_copy(pos_ref[0, 0, r], 0, r).start(priority=0)
        row_copy(pos_ref[0, 1, r], 1, r).start(priority=1)
        return c

    def drain(r, c):
        row_copy(0, 0, 0).wait()
        row_copy(0, 1, 0).wait()
        return c

    lax.fori_loop(0, tb, issue, 0, unroll=8)
    lax.fori_loop(0, tb, drain, 0, unroll=8)
    rec = info_ref[...]
    lane = lax.broadcasted_iota(jnp.int32, rec.shape, 1)
    y = (_lane_pick(rec, lane, INFO_P1) * _load_row_tiles(buf, (0,))
         + _lane_pick(rec, lane, INFO_P2) * _load_row_tiles(buf, (1,)))
    x = _token_tile(x_refs, (pl.program_id(0) % tiles_per_seq) < tiles_per_seq // 2)
    out = x + mod_ref[0, 5:6, :] * y
    if final_norm:
        out = out * lax.rsqrt(jnp.mean(out * out, axis=-1, keepdims=True) + EPS) * fg_ref[...]
    out_ref[...] = out


def _combine(pos, xs, mod, info, ys, per_batch_mod, final_g):
    bsz, t_len, d = xs[0].shape
    t_len *= len(xs)
    n = bsz * t_len
    n_tiles, _, tb = pos.shape
    tps = t_len // tb
    bidx = (lambda i: i // tps) if per_batch_mod else (lambda i: 0)
    final_norm = final_g is not None
    gain = (final_g if final_norm else jnp.ones((d,), F32)).reshape(1, d).astype(F32)
    if len(xs) == 2:
        x_specs = _half_specs(tb, d, tps, lambda i: (i // tps, i % tps))
    else:
        x_specs = [pl.BlockSpec((1, tb, d), lambda i: (i // tps, i % tps, 0))]
    return pl.pallas_call(
        functools.partial(_combine_kernel, tb=tb, final_norm=final_norm, n_x=len(xs), tiles_per_seq=tps),
        grid=(n_tiles,),
        in_specs=[pl.BlockSpec((1, 2, tb), lambda i: (i, 0, 0), memory_space=pltpu.SMEM)] + x_specs + [
                  pl.BlockSpec((1, 6, d), lambda i: (bidx(i), 0, 0)),
                  pl.BlockSpec((tb, LANES), lambda i: (i, 0)),
                  pl.BlockSpec((1, d), lambda i: (0, 0)),
                  pl.BlockSpec(memory_space=pl.ANY)],
        out_specs=pl.BlockSpec((tb, d), lambda i: (i, 0)),
        out_shape=jax.ShapeDtypeStruct((n, d), F32),
        scratch_shapes=[pltpu.VMEM((2, tb, d // LANES, LANES), F32), pltpu.SemaphoreType.DMA(())],
        compiler_params=_params("arbitrary"),
        name="moe_combine",
    )(pos, *xs, mod, info, gain, ys)


def _moe(xs, mod, ng, w_router, w1, w3, w2, layer, per_batch_mod, final_g=None):
    bsz, t_len, d = xs[0].shape
    t_len *= len(xs)
    n = bsz * t_len
    n_exp = w_router.shape[1]
    hn, info, info_t, cnt = _router(xs, mod, ng, w_router, per_batch_mod)
    info = info.reshape(n, LANES)
    e1, e2 = info_t[INFO_E1].astype(jnp.int32), info_t[INFO_E2].astype(jnp.int32)
    r1, r2 = info_t[INFO_R1].astype(jnp.int32), info_t[INFO_R2].astype(jnp.int32)
    counts = cnt[0, :n_exp].astype(jnp.int32)
    padded = ((counts + GROUP_TILE - 1) // GROUP_TILE) * GROUP_TILE
    ends = jnp.cumsum(padded)
    starts = ends - padded
    tb = _tile(t_len // len(xs), MOVE_TILE)
    pos = jnp.stack([(starts[e1] + r1).reshape(n // tb, tb), (starts[e2] + r2).reshape(n // tb, tb)], axis=1)
    s_max = 2 * n + n_exp * GROUP_TILE
    tile_row = jnp.arange(s_max // GROUP_TILE, dtype=jnp.int32) * GROUP_TILE
    tv = (tile_row < ends[-1]).astype(jnp.int32)
    te = jnp.minimum(jnp.searchsorted(ends, tile_row, side="right"), n_exp - 1).astype(jnp.int32)
    tx = (jnp.minimum(tile_row, ends[-1] - GROUP_TILE) // GROUP_TILE).astype(jnp.int32)
    region_end = ((tile_row + GROUP_TILE)[:, None] == ends[None, :]) & (padded > 0)[None, :]
    zmask = jnp.maximum(1 - tv, jnp.any(region_end, axis=1).astype(jnp.int32))
    rows = _dispatch(zmask, pos, hn.reshape(n, d // LANES, LANES), s_max)
    ys = _group_ffn(te, tx, tv, rows, w1.astype(BF16), w3.astype(BF16), w2.astype(BF16), layer)
    out = _combine(pos, xs, mod, info, ys, per_batch_mod, final_g)
    return out.reshape(bsz, t_len, d)


def _dft_channel_kernel(x_ref, mod_ref, ng_ref, cs_ref, y_ref, *, gw):
    hn = _norm_mod(x_ref[0], ng_ref[...], mod_ref[0, 1:2, :], mod_ref[0, 0:1, :]).astype(BF16)
    for g in range(hn.shape[1] // gw):
        y = _dot(hn[:, g * gw:(g + 1) * gw], cs_ref[...])
        y_ref[0, 0, :, g * gw:(g + 1) * gw] = y[:, :gw].astype(BF16)
        y_ref[0, 1, :, g * gw:(g + 1) * gw] = y[:, gw:].astype(BF16)


REV_TILE = 128


def _reverse_shift(src_tiles, wrap_row, m1):
    n_t = len(src_tiles)
    first = lax.broadcasted_iota(jnp.int32, (REV_TILE, 1), 0) == 0
    out = []
    for a in range(n_t):
        body = _dot(m1, src_tiles[n_t - 1 - a])
        head = wrap_row if a == 0 else src_tiles[n_t - a][0:1, :]
        out.append(jnp.where(first, head.astype(F32), body))
    return out


def _dft_fold_kernel(ya_ref, yb_ref, yn_ref, m1_ref, f_ref):
    n_t = ya_ref.shape[2] // REV_TILE
    keep = jnp.where(pl.program_id(1) == 0, 0.0, 1.0)
    for plane, sign in ((0, 1.0), (1, -1.0)):
        tiles = [yb_ref[0, plane, a * REV_TILE:(a + 1) * REV_TILE, :] for a in range(n_t)]
        wrap = yn_ref[0, plane, 0:1, :].astype(F32) * keep
        rev = _reverse_shift(tiles, wrap, m1_ref[...])
        for a in range(n_t):
            rows = slice(a * REV_TILE, (a + 1) * REV_TILE)
            f_ref[0, plane, rows, :] = (ya_ref[0, plane, rows, :].astype(F32) + sign * rev[a]).astype(BF16)


def _dft_time_kernel(wc_ref, ws_ref, wcx_ref, wsx_ref, f_ref, yh_ref, xlo_ref, xhi_ref, mod_ref, fw_ref, fb_ref,
                     m1_ref, lo_ref, hi_ref):
    rk = wc_ref.shape[0]
    ec, od = f_ref[0, 0], f_ref[0, 1]
    y_half = yh_ref[0, 0, 0:1, :].astype(F32) * ((2 * f_ref.shape[2]) ** -0.5)
    parity = lax.broadcasted_iota(jnp.int32, (rk, 1), 0) % 2
    p = _dot(wc_ref[...], ec) + jnp.where(parity == 0, 1.0, -1.0) * y_half
    q = _dot(ws_ref[...], od)
    gate = mod_ref[0, 2:3, :]

    def project(z, x):
        return x + gate * (_dot(z, fw_ref[...]) + fb_ref[...])

    lo_ref[0] = project((p - q).astype(BF16), xlo_ref[0])
    px = _dot(wcx_ref[...], ec)[0:1, :] + y_half
    qx = _dot(wsx_ref[...], od)[0:1, :]
    src = (p + q).astype(BF16)
    tiles = [src[a * REV_TILE:(a + 1) * REV_TILE, :] for a in range(rk // REV_TILE)]
    rev = _reverse_shift(tiles, (px + qx).astype(BF16), m1_ref[...])
    hi_ref[0] = project(jnp.concatenate(rev, axis=0).astype(BF16), xhi_ref[0])


def _dft_matrix(n, scale):
    idx = (np.arange(n)[:, None] * np.arange(n)[None, :]) % n
    ang = 2.0 * np.pi * idx.astype(np.float64) / n
    return np.cos(ang) * scale, np.sin(ang) * scale


def _fourier_mixer(x, mod, ng, fn_w, fn_b, per_batch_mod):
    bsz, t_len, d = x.shape
    gw = d // N_GROUPS
    cg, sg = _dft_matrix(gw, gw ** -0.5)
    cs = jnp.asarray(np.concatenate([cg, sg], axis=1).astype(np.float32)).astype(BF16)
    tm = _tile(t_len, 512)
    bidx = (lambda b: b) if per_batch_mod else (lambda b: 0)
    y = pl.pallas_call(
        functools.partial(_dft_channel_kernel, gw=gw),
        grid=(bsz, t_len // tm),
        in_specs=[pl.BlockSpec((1, tm, d), lambda b, i: (b, i, 0)),
                  pl.BlockSpec((1, 6, d), lambda b, i: (bidx(b), 0, 0)),
                  pl.BlockSpec((1, d), lambda b, i: (0, 0)),
                  pl.BlockSpec((gw, 2 * gw), lambda b, i: (0, 0))],
        out_specs=pl.BlockSpec((1, 2, tm, d), lambda b, i: (b, 0, i, 0)),
        out_shape=jax.ShapeDtypeStruct((bsz, 2, t_len, d), BF16),
        compiler_params=_params("parallel", "parallel"),
        name="dft_channel",
    )(x, mod, ng.reshape(1, d), cs)
    half = t_len // 2
    assert half % REV_TILE == 0 and half % DFT_SPLIT == 0
    pos = np.arange(REV_TILE)
    m1 = jnp.asarray((pos[None, :] == REV_TILE - pos[:, None]).astype(np.float32)).astype(BF16)
    rf = _tile(half, 512)
    nb_f = t_len // rf
    sub = 16
    folded = pl.pallas_call(
        _dft_fold_kernel,
        grid=(bsz, half // rf),
        in_specs=[pl.BlockSpec((1, 2, rf, d), lambda b, i: (b, 0, i, 0)),
                  pl.BlockSpec((1, 2, rf, d), lambda b, i: (b, 0, nb_f - 1 - i, 0)),
                  pl.BlockSpec((1, 2, sub, d), lambda b, i: (b, 0, ((nb_f - i) % nb_f) * (rf // sub), 0)),
                  pl.BlockSpec((REV_TILE, REV_TILE), lambda b, i: (0, 0))],
        out_specs=pl.BlockSpec((1, 2, rf, d), lambda b, i: (b, 0, i, 0)),
        out_shape=jax.ShapeDtypeStruct((bsz, 2, half, d), BF16),
        compiler_params=_params("parallel", "parallel"),
        name="dft_fold",
    )(y, y, y, m1)
    kk = np.arange(half + sub)[:, None]
    ang_hi = 2.0 * np.pi * ((kk * DFT_SPLIT * np.arange(half // DFT_SPLIT)[None, :]) % t_len) / t_len
    ang_lo = 2.0 * np.pi * ((kk * np.arange(DFT_SPLIT)[None, :]) % t_len) / t_len
    scale = t_len ** -0.5
    c_hi, s_hi = (jnp.asarray((f(ang_hi) * scale).astype(np.float32))[:, :, None] for f in (np.cos, np.sin))
    c_lo, s_lo = (jnp.asarray(f(ang_lo).astype(np.float32))[:, None, :] for f in (np.cos, np.sin))
    wc = (c_hi * c_lo - s_hi * s_lo).reshape(half + sub, half).astype(BF16)
    ws = (s_hi * c_lo + c_hi * s_lo).reshape(half + sub, half).astype(BF16)
    rk = _tile(half, 512)
    nk = half // rk
    lo, hi = pl.pallas_call(
        _dft_time_kernel,
        grid=(bsz, nk),
        in_specs=[pl.BlockSpec((rk, half), lambda b, i: (i, 0)),
                  pl.BlockSpec((rk, half), lambda b, i: (i, 0)),
                  pl.BlockSpec((sub, half), lambda b, i: ((i + 1) * (rk // sub), 0)),
                  pl.BlockSpec((sub, half), lambda b, i: ((i + 1) * (rk // sub), 0)),
                  pl.BlockSpec((1, 2, half, d), lambda b, i: (b, 0, 0, 0)),
                  pl.BlockSpec((1, 1, sub, d), lambda b, i: (b, 0, half // sub, 0)),
                  pl.BlockSpec((1, rk, d), lambda b, i: (b, i, 0)),
                  pl.BlockSpec((1, rk, d), lambda b, i: (b, 2 * nk - 1 - i, 0)),
                  pl.BlockSpec((1, 6, d), lambda b, i: (bidx(b), 0, 0)),
                  pl.BlockSpec((d, d), lambda b, i: (0, 0)),
                  pl.BlockSpec((1, d), lambda b, i: (0, 0)),
                  pl.BlockSpec((REV_TILE, REV_TILE), lambda b, i: (0, 0))],
        out_specs=(pl.BlockSpec((1, rk, d), lambda b, i: (b, i, 0)),
                   pl.BlockSpec((1, rk, d), lambda b, i: (b, nk - 1 - i, 0))),
        out_shape=(jax.ShapeDtypeStruct((bsz, half, d), F32), jax.ShapeDtypeStruct((bsz, half, d), F32)),
        compiler_params=_params("parallel", "parallel"),
        name="dft_time",
    )(wc, ws, wc, ws, folded, y, x, x, mod, fn_w.astype(BF16), fn_b.reshape(1, d).astype(F32), m1)
    return lo, hi


def _final_norm_kernel(x_ref, g_ref, o_ref):
    x = x_ref[0]
    o_ref[0] = x * lax.rsqrt(jnp.mean(x * x, axis=-1, keepdims=True) + EPS) * g_ref[...]


def _final_norm(x, g):
    bsz, t_len, d = x.shape
    tm = _tile(t_len, 1024)
    return pl.pallas_call(
        _final_norm_kernel,
        grid=(bsz, t_len // tm),
        in_specs=[pl.BlockSpec((1, tm, d), lambda b, i: (b, i, 0)), pl.BlockSpec((1, d), lambda b, i: (0, 0))],
        out_specs=pl.BlockSpec((1, tm, d), lambda b, i: (b, i, 0)),
        out_shape=jax.ShapeDtypeStruct((bsz, t_len, d), F32),
        compiler_params=_params("parallel", "parallel"),
        name="final_norm",
    )(x, g.reshape(1, d))


def _trunk(x, mod, cache, use_rope, per_batch_mod, p):
    bsz, t_len, d = x.shape
    depth = mod.shape[0]
    flat = (lambda a: a) if per_batch_mod else (lambda a: a.reshape(1, bsz * t_len, a.shape[-1]))
    unflat = (lambda a: a) if per_batch_mod else (lambda a: a.reshape(bsz, t_len, a.shape[-1]))
    states = []
    for i in range(depth):
        j = i // 2
        m_i = mod[i]
        if i % 2 == 0:
            q, kt, v, o, r = _mlstm_inproj(x, m_i, p["norm_g"][i, 0], p["ml_w_in"][j], p["ml_b_gate"][j],
                                           use_rope, per_batch_mod)
            outs = []
            for direction in range(2):
                if cache is None:
                    cn0 = jnp.zeros((bsz, N_HEADS // 2, 2 * HEAD_DK, 2 * HEAD_DV), F32)
                    m0 = jnp.zeros((bsz, N_HEADS, LANES), F32)
                else:
                    cn0, m0 = _pack_state(cache[0][:, j, direction], cache[1][:, j, direction],
                                          cache[2][:, j, direction])
                outs.append(_mlstm_scan(q, kt, v, r, cn0, m0, reverse=bool(direction)))
            states.append([_unpack_state(cn, m) for (_, cn, m) in outs])
            x = _mlstm_out(outs[0][0], outs[1][0], o, x, m_i, p["ml_head_g"][j], p["ml_w_out"][j], per_batch_mod)
            x = unflat(_ffn(flat(x), m_i, p["norm_g"][i, 1], p["ffn_w1"][j], p["ffn_w3"][j], p["ffn_w2"][j],
                            per_batch_mod))
        else:
            halves = _fourier_mixer(x, m_i, p["norm_g"][i, 0], p["fn_w"][j], p["fn_b"][j], per_batch_mod)
            in_place = per_batch_mod and (t_len // 2) % ROUTE_TILE == 0 and (t_len // 2) % MOVE_TILE == 0
            xs = halves if in_place else (flat(jnp.concatenate(halves, axis=1)),)
            closing = p["final_g"] if i == depth - 1 else None
            x = unflat(_moe(xs, m_i, p["norm_g"][i, 1], p["moe_router"][j], p["moe_w1"], p["moe_w3"],
                            p["moe_w2"], j, per_batch_mod, closing))
    if depth % 2 == 1:
        x = unflat(_final_norm(flat(x), p["final_g"]))
    return x, states


def kernel(x_prompt, x_sample, state_C, state_n, state_m, c, c_ctx, w_mod, b_mod, norm_g, final_g,
           ml_w_in, ml_b_gate, ml_head_g, ml_w_out, fn_w, fn_b, ffn_w1, ffn_w3, ffn_w2,
           moe_router, moe_w1, moe_w3, moe_w2):
    p = dict(norm_g=norm_g, final_g=final_g, ml_w_in=ml_w_in, ml_b_gate=ml_b_gate, ml_head_g=ml_head_g,
             ml_w_out=ml_w_out, fn_w=fn_w, fn_b=fn_b, ffn_w1=ffn_w1, ffn_w3=ffn_w3, ffn_w2=ffn_w2,
             moe_router=moe_router, moe_w1=moe_w1, moe_w3=moe_w3, moe_w2=moe_w2)
    depth, d = w_mod.shape[0], w_mod.shape[1]
    n_dec = c.shape[0]
    rows = ((n_dec + 1 + 7) // 8) * 8
    cond = jnp.zeros((rows, d), F32).at[:n_dec].set(c).at[n_dec].set(c_ctx)
    mod = _mod_table(cond, w_mod, b_mod).reshape(depth, rows, 6, d)
    y_prompt, st = _trunk(x_prompt, mod[:, n_dec:n_dec + 1], None, False, False, p)
    y_sample, _ = _trunk(x_sample, mod[:, :n_dec], (state_C, state_n, state_m), True, True, p)
    new_c = jnp.stack([jnp.stack([s[0][0], s[1][0]], axis=1) for s in st], axis=1)
    new_n = jnp.stack([jnp.stack([s[0][1], s[1][1]], axis=1) for s in st], axis=1)
    new_m = jnp.stack([jnp.stack([s[0][2], s[1][2]], axis=1) for s in st], axis=1)
    return (y_prompt, y_sample, new_c.astype(x_prompt.dtype), new_n.astype(x_prompt.dtype),
            new_m.astype(x_prompt.dtype))
```

```python
import functools

import numpy as np
import jax
import jax.numpy as jnp
from jax import lax
from jax.experimental import pallas as pl
from jax.experimental.pallas import tpu as pltpu

F32 = jnp.float32
BF16 = jnp.bfloat16
HIGHEST = lax.Precision.HIGHEST

EPS = 1e-6
N_HEADS = 8
HEAD_DK = 64
HEAD_DV = 128
QK_W = N_HEADS * HEAD_DK
V_W = N_HEADS * HEAD_DV
GRID_W = 64
ROPE_BASE = 10000.0
N_GROUPS = 4
DFT_SPLIT = 64
N_GATES = 4 * N_HEADS
LANES = 128
SCAN_CHUNK = 128
SCAN_ROWS = 3 * N_HEADS
VMEM_LIMIT = 56 * 1024 * 1024
NEG_INF = float("-inf")


def _params(*sem):
    return pltpu.CompilerParams(dimension_semantics=sem, vmem_limit_bytes=VMEM_LIMIT)


def _tile(n, pref):
    t = min(n, pref)
    assert n % t == 0, (n, pref)
    return t


def _norm_mod(x, gain, scale, shift):
    ms = jnp.mean(x * x, axis=-1, keepdims=True)
    return x * lax.rsqrt(ms + EPS) * gain * (1.0 + scale) + shift


def _dot(a, b):
    return jnp.dot(a, b, preferred_element_type=F32)


def _dot_nt(a, b):
    return lax.dot_general(a, b, (((1,), (1,)), ((), ())), preferred_element_type=F32)


def _exact_rows_dot(x, ones_mat):
    hi = x.astype(BF16)
    rest = x - hi.astype(F32)
    mid = rest.astype(BF16)
    lo = (rest - mid.astype(F32)).astype(BF16)
    n = x.shape[0]
    out = _dot(jnp.concatenate([hi, mid, lo, jnp.zeros_like(hi)], axis=0), ones_mat)
    return out[0:n] + out[n:2 * n] + out[2 * n:3 * n]


def _log_sigmoid(x):
    return jnp.minimum(x, 0.0) - jnp.log1p(jnp.exp(-jnp.abs(x)))


def _mod_kernel(c_ref, w_ref, b_ref, o_ref):
    c = c_ref[...]
    s = c * jax.nn.sigmoid(c)
    o_ref[0] = jnp.dot(s, w_ref[0], preferred_element_type=F32, precision=HIGHEST) + b_ref[0]


def _mod_table(cond, w_mod, b_mod):
    depth, d, n = w_mod.shape
    rows = cond.shape[0]
    tn = _tile(n, 1536)
    return pl.pallas_call(
        _mod_kernel,
        grid=(depth, n // tn),
        in_specs=[pl.BlockSpec((rows, d), lambda l, j: (0, 0)),
                  pl.BlockSpec((1, d, tn), lambda l, j: (l, 0, j)),
                  pl.BlockSpec((1, 1, tn), lambda l, j: (l, 0, j))],
        out_specs=pl.BlockSpec((1, rows, tn), lambda l, j: (l, 0, j)),
        out_shape=jax.ShapeDtypeStruct((depth, rows, n), F32),
        compiler_params=_params("parallel", "parallel"),
        name="adaln_table",
    )(cond, w_mod, b_mod.reshape(depth, 1, n))


def _rope_tables(t_len):
    pos = np.arange(t_len)
    row = (pos // GRID_W).astype(np.float32)
    col = (pos % GRID_W).astype(np.float32)
    nf = HEAD_DK // 4
    inv = (np.float32(ROPE_BASE) ** (-np.arange(nf, dtype=np.float32) / nf)).astype(np.float32)
    d = np.arange(HEAD_DK)
    p = np.where(d[None, :] < HEAD_DK // 2, row[:, None], col[:, None]).astype(np.float32)
    ang = p * inv[d % nf][None, :]
    sign = np.where((d % (2 * nf)) < nf, -1.0, 1.0).astype(np.float32)
    return np.cos(ang).astype(np.float32), (np.sin(ang) * sign[None, :]).astype(np.float32)


def _inproj_kernel(*refs, use_rope):
    if use_rope:
        (x_ref, mod_ref, ng_ref, wq_ref, wkt_ref, wv_ref, wo_ref, wgt_ref, bg_ref, trif_ref, trib_ref,
         cq_ref, sq_ref, ck_ref, sk_ref, q_ref, kt_ref, v_ref, o_ref, r_ref) = refs
    else:
        (x_ref, mod_ref, ng_ref, wq_ref, wkt_ref, wv_ref, wo_ref, wgt_ref, bg_ref, trif_ref, trib_ref,
         q_ref, kt_ref, v_ref, o_ref, r_ref) = refs
    hn = _norm_mod(x_ref[0], ng_ref[...], mod_ref[0, 1:2, :], mod_ref[0, 0:1, :])
    hb = hn.astype(BF16)
    q = _dot(hb, wq_ref[...]) * (HEAD_DK ** -0.5)
    kt = _dot_nt(wkt_ref[...], hb)
    if use_rope:
        nf = HEAD_DK // 4
        lane = lax.broadcasted_iota(jnp.int32, (q.shape[0], LANES), 1)
        first_q = (lane % (2 * nf)) < nf
        sub = lax.broadcasted_iota(jnp.int32, (LANES, kt.shape[1]), 0)
        first_k = (sub % (2 * nf)) < nf
        cq, sq, ck, sk = cq_ref[...], sq_ref[...], ck_ref[...], sk_ref[...]
        for s in range(QK_W // LANES):
            qs = q[:, s * LANES:(s + 1) * LANES]
            sw = jnp.where(first_q, pltpu.roll(qs, LANES - nf, 1), pltpu.roll(qs, nf, 1))
            q_ref[0, :, s * LANES:(s + 1) * LANES] = (qs * cq + sw * sq).astype(BF16)
            ks = kt[s * LANES:(s + 1) * LANES, :]
            sw = jnp.where(first_k, pltpu.roll(ks, LANES - nf, 0), pltpu.roll(ks, nf, 0))
            kt_ref[0, s * LANES:(s + 1) * LANES, :] = (ks * ck + sw * sk).astype(BF16)
    else:
        q_ref[0] = q.astype(BF16)
        kt_ref[0] = kt.astype(BF16)
    v_ref[0] = _dot(hb, wv_ref[...]).astype(BF16)
    o_ref[0] = _dot(hb, wo_ref[...]).astype(BF16)
    gt = _dot_nt(wgt_ref[...], hb) + bg_ref[...]
    h = N_HEADS
    i_f, f_f = gt[0:h], _log_sigmoid(gt[h:2 * h])
    i_b, f_b = gt[2 * h:3 * h], _log_sigmoid(gt[3 * h:4 * h])
    b_f = _exact_rows_dot(f_f, trif_ref[...])
    b_b = _exact_rows_dot(f_b, trib_ref[...])
    a_f, a_b = i_f - b_f, i_b - b_b
    width = a_f.shape[1]
    in_chunk = lax.broadcasted_iota(jnp.int32, a_f.shape, 1) % SCAN_CHUNK
    c_f, c_b = a_f, a_b
    step = 1
    while step < SCAN_CHUNK:
        c_f = jnp.where(in_chunk >= step, jnp.maximum(c_f, pltpu.roll(c_f, step, 1)), c_f)
        c_b = jnp.where(in_chunk < SCAN_CHUNK - step, jnp.maximum(c_b, pltpu.roll(c_b, width - step, 1)), c_b)
        step *= 2
    for k, rows in enumerate((a_f, b_f, c_f, a_b, b_b, c_b)):
        r_ref[0, k * h:(k + 1) * h, :] = rows


def _mlstm_inproj(x, mod, ng, w_in, b_gate, use_rope, per_batch_mod):
    bsz, t_len, d = x.shape
    tm = _tile(t_len, 512)
    wq = w_in[:, :QK_W].astype(BF16)
    wkt = w_in[:, QK_W:2 * QK_W].T.astype(BF16)
    wv = w_in[:, 2 * QK_W:2 * QK_W + V_W].astype(BF16)
    wo = w_in[:, 2 * QK_W + V_W:2 * QK_W + 2 * V_W].astype(BF16)
    wgt = w_in[:, 2 * QK_W + 2 * V_W:].T.astype(BF16)
    bg = b_gate.reshape(N_GATES, 1).astype(F32)
    pos = np.arange(tm)
    same = (pos[:, None] // SCAN_CHUNK) == (pos[None, :] // SCAN_CHUNK)
    trif = jnp.asarray((same & (pos[:, None] <= pos[None, :])).astype(np.float32)).astype(BF16)
    trib = jnp.asarray((same & (pos[:, None] >= pos[None, :])).astype(np.float32)).astype(BF16)
    bidx = (lambda b: b) if per_batch_mod else (lambda b: 0)
    const = lambda shp: pl.BlockSpec(shp, lambda b, i: (0,) * len(shp))
    in_specs = [pl.BlockSpec((1, tm, d), lambda b, i: (b, i, 0)),
                pl.BlockSpec((1, 6, d), lambda b, i: (bidx(b), 0, 0)),
                const((1, d)), const((d, QK_W)), const((QK_W, d)), const((d, V_W)), const((d, V_W)),
                const((N_GATES, d)), const((N_GATES, 1)), const((tm, tm)), const((tm, tm))]
    args = [x, mod, ng.reshape(1, d), wq, wkt, wv, wo, wgt, bg, trif, trib]
    if use_rope:
        cos, sin = _rope_tables(t_len)
        rep = LANES // HEAD_DK
        args += [jnp.asarray(np.tile(cos, (1, rep))), jnp.asarray(np.tile(sin, (1, rep))),
                 jnp.asarray(np.tile(cos.T, (rep, 1))), jnp.asarray(np.tile(sin.T, (rep, 1)))]
        in_specs += [pl.BlockSpec((tm, LANES), lambda b, i: (i, 0)), pl.BlockSpec((tm, LANES), lambda b, i: (i, 0)),
                     pl.BlockSpec((LANES, tm), lambda b, i: (0, i)), pl.BlockSpec((LANES, tm), lambda b, i: (0, i))]
    out_shape = (jax.ShapeDtypeStruct((bsz, t_len, QK_W), BF16),
                 jax.ShapeDtypeStruct((bsz, QK_W, t_len), BF16),
                 jax.ShapeDtypeStruct((bsz, t_len, V_W), BF16),
                 jax.ShapeDtypeStruct((bsz, t_len, V_W), BF16),
                 jax.ShapeDtypeStruct((bsz, 2 * SCAN_ROWS, t_len), F32))
    out_specs = (pl.BlockSpec((1, tm, QK_W), lambda b, i: (b, i, 0)),
                 pl.BlockSpec((1, QK_W, tm), lambda b, i: (b, 0, i)),
                 pl.BlockSpec((1, tm, V_W), lambda b, i: (b, i, 0)),
                 pl.BlockSpec((1, tm, V_W), lambda b, i: (b, i, 0)),
                 pl.BlockSpec((1, 2 * SCAN_ROWS, tm), lambda b, i: (b, 0, i)))
    return pl.pallas_call(
        functools.partial(_inproj_kernel, use_rope=use_rope),
        grid=(bsz, t_len // tm), in_specs=in_specs, out_specs=out_specs, out_shape=out_shape,
        compiler_params=_params("parallel", "parallel"),
        name="mlstm_inproj",
    )(*args)


def _scan_kernel(q_ref, kt_ref, v_ref, r_ref, cn0_ref, m0_ref, spread_ref, h_ref, cn_out_ref, m_out_ref,
                 cn_scr, m_scr, *, reverse, n_chunks):
    L = SCAN_CHUNK
    j = pl.program_id(1)

    @pl.when(j == 0)
    def _():
        cn_scr[...] = cn0_ref[0]
        m_scr[...] = m0_ref[0]

    row_i = lax.broadcasted_iota(jnp.int32, (L, L), 0)
    col_i = lax.broadcasted_iota(jnp.int32, (L, L), 1)
    visible = (col_i >= row_i) if reverse else (col_i <= row_i)
    upper_lanes = lax.broadcasted_iota(jnp.int32, (L, LANES), 1) >= HEAD_DK
    ones = jnp.ones((L, LANES), BF16)
    no_rows = jnp.zeros((HEAD_DK, 2 * HEAD_DV), BF16)
    end_lane = lax.broadcasted_iota(jnp.int32, (1, L), 1) == (0 if reverse else L - 1)
    order = range(n_chunks - 1, -1, -1) if reverse else range(n_chunks)
    head_rows = [slice((h % 2) * HEAD_DK, (h % 2 + 1) * HEAD_DK) for h in range(N_HEADS)]
    states = [(cn_scr[h // 2, head_rows[h], :], m_scr[h:h + 1, 0:1]) for h in range(N_HEADS)]

    def columns(c):
        x = r_ref[0, N_HEADS:3 * N_HEADS, c * L:(c + 1) * L]
        hi = x.astype(BF16).astype(F32)
        mid = (x - hi).astype(BF16).astype(F32)
        lo = x - hi - mid
        parts = jnp.concatenate([hi, mid, lo, jnp.zeros((L - 6 * N_HEADS, L), F32)], axis=0)
        return _dot(parts.T.astype(BF16), spread_ref[...])

    cols = {c: columns(c) for c in order}
    new_states = []
    for h in range(N_HEADS):
        p, e = h // 2, h % 2
        cn, m = states[h]
        for c in order:
            r0 = c * L
            q_pair = q_ref[0, r0:r0 + L, p * LANES:(p + 1) * LANES]
            q_m = jnp.where(upper_lanes if e else jnp.logical_not(upper_lanes), q_pair, jnp.zeros_like(q_pair))
            kt_pair = kt_ref[0, p * LANES:(p + 1) * LANES, r0:r0 + L]
            kt_h = kt_ref[0, h * HEAD_DK:(h + 1) * HEAD_DK, r0:r0 + L]
            a_row = r_ref[0, h:h + 1, r0:r0 + L]
            b_row = r_ref[0, N_HEADS + h:N_HEADS + h + 1, r0:r0 + L]
            v_aug = jnp.concatenate([v_ref[0, r0:r0 + L, h * HEAD_DV:(h + 1) * HEAD_DV], ones], axis=1)
            am = jnp.where(visible, a_row, NEG_INF)
            a_max = jnp.max(a_row, axis=1, keepdims=True)
            kw = (kt_h.astype(F32) * jnp.exp(a_row - a_max)).astype(BF16)
            upd = _dot(kw, v_aug)
            b_col = cols[c][:, h * LANES:(h + 1) * LANES]
            b_end = jnp.sum(jnp.where(end_lane, b_row, 0.0), axis=1, keepdims=True)
            u = jnp.maximum(m, cols[c][:, (N_HEADS + h) * LANES:(N_HEADS + h + 1) * LANES])
            guard = jnp.exp(-(b_col + u))
            s = (_dot(q_m, kt_pair) * jnp.exp(am - u)).astype(BF16)
            q_old = (q_m.astype(F32) * jnp.exp(m - u)).astype(BF16)
            cn_b = cn.astype(BF16)
            cn_pair = jnp.concatenate([no_rows, cn_b] if e else [cn_b, no_rows], axis=0)
            both = _dot(jnp.concatenate([q_old, s], axis=1), jnp.concatenate([cn_pair, v_aug], axis=0))
            num, den = both[:, :HEAD_DV], both[:, HEAD_DV:]
            h_ref[0, r0:r0 + L, h * HEAD_DV:(h + 1) * HEAD_DV] = (num / jnp.maximum(jnp.abs(den), guard)).astype(BF16)
            u_end = jnp.maximum(m, a_max)
            cn = jnp.exp(m - u_end) * cn + jnp.exp(a_max - u_end) * upd
            m = b_end + u_end
        new_states.append((cn, m))
    for h in range(N_HEADS):
        cn_scr[h // 2, head_rows[h], :] = new_states[h][0]
        m_scr[h:h + 1, :] = jnp.broadcast_to(new_states[h][1], (1, LANES))

    @pl.when(j == pl.num_programs(1) - 1)
    def _():
        cn_out_ref[0] = cn_scr[...]
        m_out_ref[0] = m_scr[...]


def _mlstm_scan(q, kt, v, r, cn0, m0, reverse):
    bsz, t_len, _ = q.shape
    tb = _tile(t_len, 4 * SCAN_CHUNK)
    nblk = t_len // tb
    blk = (lambda j: nblk - 1 - j) if reverse else (lambda j: j)
    d = 1 if reverse else 0
    half = N_HEADS // 2
    src = np.arange(LANES)[:, None]
    dst = np.arange(2 * N_HEADS * LANES)[None, :]
    spread = jnp.asarray(((src < 6 * N_HEADS) & (src % (2 * N_HEADS) == dst // LANES)).astype(np.float32)).astype(BF16)
    return pl.pallas_call(
        functools.partial(_scan_kernel, reverse=reverse, n_chunks=tb // SCAN_CHUNK),
        grid=(bsz, nblk),
        in_specs=[pl.BlockSpec((1, tb, QK_W), lambda b, j: (b, blk(j), 0)),
                  pl.BlockSpec((1, QK_W, tb), lambda b, j: (b, 0, blk(j))),
                  pl.BlockSpec((1, tb, V_W), lambda b, j: (b, blk(j), 0)),
                  pl.BlockSpec((1, SCAN_ROWS, tb), lambda b, j: (b, d, blk(j))),
                  pl.BlockSpec((1, half, 2 * HEAD_DK, 2 * HEAD_DV), lambda b, j: (b, 0, 0, 0)),
                  pl.BlockSpec((1, N_HEADS, LANES), lambda b, j: (b, 0, 0)),
                  pl.BlockSpec((LANES, 2 * N_HEADS * LANES), lambda b, j: (0, 0))],
        out_specs=(pl.BlockSpec((1, tb, V_W), lambda b, j: (b, blk(j), 0)),
                   pl.BlockSpec((1, half, 2 * HEAD_DK, 2 * HEAD_DV), lambda b, j: (b, 0, 0, 0)),
                   pl.BlockSpec((1, N_HEADS, LANES), lambda b, j: (b, 0, 0))),
        out_shape=(jax.ShapeDtypeStruct((bsz, t_len, V_W), BF16),
                   jax.ShapeDtypeStruct((bsz, half, 2 * HEAD_DK, 2 * HEAD_DV), F32),
                   jax.ShapeDtypeStruct((bsz, N_HEADS, LANES), F32)),
        scratch_shapes=[pltpu.VMEM((half, 2 * HEAD_DK, 2 * HEAD_DV), F32),
                        pltpu.VMEM((N_HEADS, LANES), F32)],
        compiler_params=_params("parallel", "arbitrary"),
        name="mlstm_scan_bwd" if reverse else "mlstm_scan_fwd",
    )(q, kt, v, r, cn0, m0, spread)


def _pack_state(c, n, m):
    bsz = c.shape[0]
    cn = jnp.concatenate([c, jnp.broadcast_to(n[..., None], n.shape + (HEAD_DV,))], axis=-1)
    cn = cn.reshape(bsz, N_HEADS // 2, 2 * HEAD_DK, 2 * HEAD_DV)
    return cn.astype(F32), jnp.broadcast_to(m[..., None], (bsz, N_HEADS, LANES)).astype(F32)


def _unpack_state(cn, m):
    bsz = cn.shape[0]
    cn = cn.reshape(bsz, N_HEADS, HEAD_DK, 2 * HEAD_DV)
    return cn[..., :HEAD_DV], cn[..., HEAD_DV], m[..., 0]


def _mlstm_out_ffn_kernel(hf_ref, hb_ref, o_ref, x_ref, mod_ref, hg_ref, wo_ref, ng_ref, w1_ref, w3_ref, w2_ref,
                          out_ref, x1_scr, hn_scr, acc_scr):
    f = pl.program_id(2)

    @pl.when(f == 0)
    def _():
        hs = hf_ref[0].astype(F32) + hb_ref[0].astype(F32)
        parts = []
        for h in range(N_HEADS):
            z = hs[:, h * HEAD_DV:(h + 1) * HEAD_DV]
            parts.append(z * lax.rsqrt(jnp.mean(z * z, axis=-1, keepdims=True) + EPS))
        gated = jnp.concatenate(parts, axis=1) * hg_ref[...] * jax.nn.sigmoid(o_ref[0].astype(F32))
        x1 = x_ref[0] + mod_ref[0, 2:3, :] * _dot(gated.astype(BF16), wo_ref[...])
        x1_scr[...] = x1
        hn_scr[...] = _norm_mod(x1, ng_ref[...], mod_ref[0, 4:5, :], mod_ref[0, 3:4, :]).astype(BF16)
        acc_scr[...] = jnp.zeros_like(acc_scr)

    hb = hn_scr[...]
    h1 = _dot(hb, w1_ref[...])
    h3 = _dot(hb, w3_ref[...])
    act = (h1 * jax.nn.sigmoid(h1) * h3).astype(BF16)
    acc_scr[...] += _dot(act, w2_ref[...])

    @pl.when(f == pl.num_programs(2) - 1)
    def _():
        out_ref[0] = x1_scr[...] + mod_ref[0, 5:6, :] * acc_scr[...]


def _mlstm_out_ffn(hf, hb, o, x, mod, head_g, w_out, ng, w1, w3, w2, per_batch_mod):
    bsz, t_len, d = x.shape
    d_ff = w1.shape[1]
    tm = _tile(t_len, 512)
    tf = _tile(d_ff, 1792)
    bidx = (lambda b: b) if per_batch_mod else (lambda b: 0)
    tok = lambda w: pl.BlockSpec((1, tm, w), lambda b, i, f: (b, i, 0))
    return pl.pallas_call(
        _mlstm_out_ffn_kernel,
        grid=(bsz, t_len // tm, d_ff // tf),
        in_specs=[tok(V_W), tok(V_W), tok(V_W), tok(d),
                  pl.BlockSpec((1, 6, d), lambda b, i, f: (bidx(b), 0, 0)),
                  pl.BlockSpec((1, V_W), lambda b, i, f: (0, 0)),
                  pl.BlockSpec((V_W, d), lambda b, i, f: (0, 0)),
                  pl.BlockSpec((1, d), lambda b, i, f: (0, 0)),
                  pl.BlockSpec((d, tf), lambda b, i, f: (0, f)),
                  pl.BlockSpec((d, tf), lambda b, i, f: (0, f)),
                  pl.BlockSpec((tf, d), lambda b, i, f: (f, 0))],
        out_specs=tok(d),
        out_shape=jax.ShapeDtypeStruct((bsz, t_len, d), F32),
        scratch_shapes=[pltpu.VMEM((tm, d), F32), pltpu.VMEM((tm, d), BF16), pltpu.VMEM((tm, d), F32)],
        compiler_params=_params("parallel", "parallel", "arbitrary"),
        name="mlstm_out_swiglu",
    )(hf, hb, o, x, mod, head_g.reshape(1, V_W).astype(F32), w_out.astype(BF16), ng.reshape(1, d),
      w1.astype(BF16), w3.astype(BF16), w2.astype(BF16))


ROUTE_TILE = 512
MOVE_TILE = 1024
GROUP_TILE = 512
INFO_E1, INFO_E2, INFO_P1, INFO_P2, INFO_R1, INFO_R2 = range(6)
INFO_ROWS = 8


def _lane_pick(rec, lane, k):
    return jnp.sum(jnp.where(lane == k, rec, 0.0), axis=-1, keepdims=True)


def _token_tile(x_refs, first_half):
    if len(x_refs) == 1:
        return x_refs[0][0]
    return jnp.where(first_half, x_refs[0][0], x_refs[1][0])


def _half_specs(tm, d, tiles_per_seq, locate):
    nh = tiles_per_seq // 2

    def lo(*g):
        b, i = locate(*g)
        return (b, jnp.minimum(i, nh - 1), 0)

    def hi(*g):
        b, i = locate(*g)
        return (b, jnp.maximum(i - nh, 0), 0)

    return [pl.BlockSpec((1, tm, d), lo), pl.BlockSpec((1, tm, d), hi)]


def _router_kernel(*refs, n_exp, n_x):
    x_refs = refs[:n_x]
    mod_ref, ng_ref, wr_ref, tri_ref, hn_ref, info_ref, infot_ref, cnt_ref, carry = refs[n_x:]

    @pl.when((pl.program_id(0) == 0) & (pl.program_id(1) == 0))
    def _():
        carry[...] = jnp.zeros_like(carry)

    x = _token_tile(x_refs, pl.program_id(1) < pl.num_programs(1) // 2)
    hn = _norm_mod(x, ng_ref[...], mod_ref[0, 4:5, :], mod_ref[0, 3:4, :])
    _store_row_tiles(hn_ref, (0,), hn)
    hn_hi = hn.astype(BF16)
    hn_lo = (hn - hn_hi.astype(F32)).astype(BF16)
    hh = _dot(hn_hi, wr_ref[...])
    logits = hh[:, :LANES] + hh[:, LANES:] + _dot(hn_lo, wr_ref[:, :LANES])
    lane = lax.broadcasted_iota(jnp.int32, logits.shape, 1).astype(F32)
    logits = jnp.where(lane < n_exp, logits, NEG_INF)
    v1 = jnp.max(logits, axis=-1, keepdims=True)
    i1 = jnp.min(jnp.where(logits == v1, lane, float(LANES)), axis=-1, keepdims=True)
    rest = jnp.where(lane == i1, NEG_INF, logits)
    v2 = jnp.max(rest, axis=-1, keepdims=True)
    i2 = jnp.min(jnp.where(rest == v2, lane, float(LANES)), axis=-1, keepdims=True)
    e2 = jnp.exp(v2 - v1)
    den = 1.0 + e2
    chosen = jnp.where((lane == i1) | (lane == i2), 1.0, 0.0)
    before = _dot(tri_ref[...], chosen.astype(BF16)) + carry[...]
    r1 = jnp.sum(jnp.where(lane == i1, before, 0.0), axis=-1, keepdims=True)
    r2 = jnp.sum(jnp.where(lane == i2, before, 0.0), axis=-1, keepdims=True)
    total = carry[...] + jnp.sum(chosen, axis=0, keepdims=True)
    carry[...] = total
    cnt_ref[...] = total
    rec = jnp.zeros_like(logits)
    for k, val in ((INFO_E1, i1), (INFO_E2, i2), (INFO_P1, 1.0 / den), (INFO_P2, e2 / den),
                   (INFO_R1, r1), (INFO_R2, r2)):
        rec = jnp.where(lane == k, val, rec)
    info_ref[0] = rec
    infot_ref[...] = rec.T[0:INFO_ROWS, :]


def _router(xs, mod, ng, w_router, per_batch_mod):
    bsz, t_len, d = xs[0].shape
    t_len *= len(xs)
    n_exp = w_router.shape[1]
    tm = _tile(t_len, ROUTE_TILE)
    wr = jnp.zeros((d, LANES), F32).at[:, :n_exp].set(w_router)
    wr_hi = wr.astype(BF16)
    wr = jnp.concatenate([wr_hi, (wr - wr_hi.astype(F32)).astype(BF16)], axis=1)
    pos = np.arange(tm)
    tri = jnp.asarray((pos[None, :] < pos[:, None]).astype(np.float32)).astype(BF16)
    bidx = (lambda b: b) if per_batch_mod else (lambda b: 0)
    if len(xs) == 2:
        x_specs = _half_specs(tm, d, t_len // tm, lambda b, i: (b, i))
    else:
        x_specs = [pl.BlockSpec((1, tm, d), lambda b, i: (b, i, 0))]
    return pl.pallas_call(
        functools.partial(_router_kernel, n_exp=n_exp, n_x=len(xs)),
        grid=(bsz, t_len // tm),
        in_specs=x_specs + [
                  pl.BlockSpec((1, 6, d), lambda b, i: (bidx(b), 0, 0)),
                  pl.BlockSpec((1, d), lambda b, i: (0, 0)),
                  pl.BlockSpec((d, 2 * LANES), lambda b, i: (0, 0)),
                  pl.BlockSpec((tm, tm), lambda b, i: (0, 0))],
        out_specs=(pl.BlockSpec((1, tm, d // LANES, LANES), lambda b, i: (b, i, 0, 0)),
                   pl.BlockSpec((1, tm, LANES), lambda b, i: (b, i, 0)),
                   pl.BlockSpec((INFO_ROWS, tm), lambda b, i: (0, b * (t_len // tm) + i)),
                   pl.BlockSpec((1, LANES), lambda b, i: (0, 0))),
        out_shape=(jax.ShapeDtypeStruct((bsz, t_len, d // LANES, LANES), F32),
                   jax.ShapeDtypeStruct((bsz, t_len, LANES), F32),
                   jax.ShapeDtypeStruct((INFO_ROWS, bsz * t_len), F32),
                   jax.ShapeDtypeStruct((1, LANES), F32)),
        scratch_shapes=[pltpu.VMEM((1, LANES), F32)],
        compiler_params=_params("arbitrary", "arbitrary"),
        name="moe_router",
    )(*xs, mod, ng.reshape(1, d), wr, tri)


def _store_row_tiles(ref, lead, val):
    chunks = jnp.stack([val[:, c * LANES:(c + 1) * LANES] for c in range(val.shape[1] // LANES)], axis=0)
    ref[lead] = jnp.swapaxes(chunks, 0, 1)


def _load_row_tiles(ref, lead):
    n_c = ref.shape[-2]
    x = jnp.swapaxes(ref[lead], 0, 1)
    return jnp.concatenate([x[c] for c in range(n_c)], axis=1)


def _dispatch_kernel(zmask_ref, pos_ref, hn_ref, xs_ref, zeros, sem, *, tb, n_row_tiles):
    def row_copy(r, slot):
        return pltpu.make_async_copy(hn_ref.at[r], xs_ref.at[slot], sem)

    def tile_fill(r):
        row0 = pl.multiple_of(r * GROUP_TILE, GROUP_TILE)
        return pltpu.make_async_copy(zeros, xs_ref.at[pl.ds(row0, GROUP_TILE)], sem)

    @pl.when(pl.program_id(0) == 0)
    def _():
        zeros[...] = jnp.zeros_like(zeros)

        def fill(r, c):
            @pl.when(zmask_ref[r] != 0)
            def _():
                tile_fill(r).start()
            return c

        def fill_done(r, c):
            @pl.when(zmask_ref[r] != 0)
            def _():
                tile_fill(r).wait()
            return c

        lax.fori_loop(0, n_row_tiles, fill, 0)
        lax.fori_loop(0, n_row_tiles, fill_done, 0)

    def issue(r, c):
        row_copy(r, pos_ref[0, 0, r]).start(priority=0)
        row_copy(r, pos_ref[0, 1, r]).start(priority=1)
        return c

    def drain(r, c):
        row_copy(0, 0).wait()
        row_copy(0, 0).wait()
        return c

    lax.fori_loop(0, tb, issue, 0, unroll=8)
    lax.fori_loop(0, tb, drain, 0, unroll=8)


def _dispatch(zmask, pos, hn, s_max):
    n, n_c, _ = hn.shape
    n_tiles, _, tb = pos.shape
    return pl.pallas_call(
        functools.partial(_dispatch_kernel, tb=tb, n_row_tiles=zmask.shape[0]),
        grid_spec=pltpu.PrefetchScalarGridSpec(
            num_scalar_prefetch=1,
            grid=(n_tiles,),
            in_specs=[pl.BlockSpec((1, 2, tb), lambda i, zm: (i, 0, 0), memory_space=pltpu.SMEM),
                      pl.BlockSpec((tb, n_c, LANES), lambda i, zm: (i, 0, 0))],
            out_specs=pl.BlockSpec(memory_space=pl.ANY),
            scratch_shapes=[pltpu.VMEM((GROUP_TILE, n_c, LANES), F32), pltpu.SemaphoreType.DMA(())]),
        out_shape=jax.ShapeDtypeStruct((s_max, n_c, LANES), F32),
        compiler_params=_params("arbitrary"),
        name="moe_dispatch",
    )(zmask, pos, hn)


def _group_ffn_kernel(te_ref, tx_ref, tv_ref, x_ref, w1_ref, w3_ref, w2_ref, o_ref, xb_scr, acc_scr):
    r, f = pl.program_id(0), pl.program_id(1)
    last = pl.num_programs(1) - 1

    @pl.when(tv_ref[r] != 0)
    def _():
        @pl.when(f == 0)
        def _():
            xb_scr[...] = _load_row_tiles(x_ref, ()).astype(BF16)

        xb = xb_scr[...]
        h1 = _dot(xb, w1_ref[0, 0])
        h3 = _dot(xb, w3_ref[0, 0])
        act = (h1 * jax.nn.sigmoid(h1) * h3).astype(BF16)
        y = _dot(act, w2_ref[0, 0])

        @pl.when(f == 0)
        def _():
            acc_scr[...] = y

        @pl.when(f != 0)
        def _():
            acc_scr[...] += y

        @pl.when(f == last)
        def _():
            _store_row_tiles(o_ref, (), acc_scr[...])

    @pl.when((tv_ref[r] == 0) & (f == last))
    def _():
        o_ref[...] = jnp.zeros_like(o_ref)


def _group_ffn(te, tx, tv, xs, w1, w3, w2, layer):
    s_max, n_c, _ = xs.shape
    d = n_c * LANES
    d_ff = w1.shape[3]
    tf = _tile(d_ff, 1792)
    nf = d_ff // tf
    fidx = lambda r, f, tv: f * tv[r] + (nf - 1) * (1 - tv[r])
    return pl.pallas_call(
        _group_ffn_kernel,
        grid_spec=pltpu.PrefetchScalarGridSpec(
            num_scalar_prefetch=3,
            grid=(s_max // GROUP_TILE, nf),
            in_specs=[pl.BlockSpec((GROUP_TILE, n_c, LANES), lambda r, f, te, tx, tv: (tx[r], 0, 0)),
                      pl.BlockSpec((1, 1, d, tf), lambda r, f, te, tx, tv: (layer, te[r], 0, fidx(r, f, tv))),
                      pl.BlockSpec((1, 1, d, tf), lambda r, f, te, tx, tv: (layer, te[r], 0, fidx(r, f, tv))),
                      pl.BlockSpec((1, 1, tf, d), lambda r, f, te, tx, tv: (layer, te[r], fidx(r, f, tv), 0))],
            out_specs=pl.BlockSpec((GROUP_TILE, n_c, LANES), lambda r, f, te, tx, tv: (r, 0, 0)),
            scratch_shapes=[pltpu.VMEM((GROUP_TILE, d), BF16), pltpu.VMEM((GROUP_TILE, d), F32)]),
        out_shape=jax.ShapeDtypeStruct((s_max, n_c, LANES), F32),
        compiler_params=_params("parallel", "arbitrary"),
        name="moe_group_swiglu",
    )(te, tx, tv, xs, w1, w3, w2)


def _combine_kernel(*refs, tb, final_norm, n_x, tiles_per_seq):
    pos_ref = refs[0]
    x_refs = refs[1:1 + n_x]
    mod_ref, info_ref, fg_ref, ys_ref, out_ref, buf, sem = refs[1 + n_x:]

    def row_copy(slot, k, r):
        return pltpu.make_async_copy(ys_ref.at[slot], buf.at[k, r], sem)

    def issue(r, c):
        row_copy(pos_ref[0, 0, r], 0, r).start(priority=0)
        row_copy(pos_ref[0, 1, r], 1, r).start(priority=1)
        return c

    def drain(r, c):
        row_copy(0, 0, 0).wait()
        row_copy(0, 1, 0).wait()
        return c

    lax.fori_loop(0, tb, issue, 0, unroll=8)
    lax.fori_loop(0, tb, drain, 0, unroll=8)
    rec = info_ref[...]
    lane = lax.broadcasted_iota(jnp.int32, rec.shape, 1)
    y = (_lane_pick(rec, lane, INFO_P1) * _load_row_tiles(buf, (0,))
         + _lane_pick(rec, lane, INFO_P2) * _load_row_tiles(buf, (1,)))
    x = _token_tile(x_refs, (pl.program_id(0) % tiles_per_seq) < tiles_per_seq // 2)
    out = x + mod_ref[0, 5:6, :] * y
    if final_norm:
        out = out * lax.rsqrt(jnp.mean(out * out, axis=-1, keepdims=True) + EPS) * fg_ref[...]
    out_ref[...] = out


def _combine(pos, xs, mod, info, ys, per_batch_mod, final_g):
    bsz, t_len, d = xs[0].shape
    t_len *= len(xs)
    n = bsz * t_len
    n_tiles, _, tb = pos.shape
    tps = t_len // tb
    bidx = (lambda i: i // tps) if per_batch_mod else (lambda i: 0)
    final_norm = final_g is not None
    gain = (final_g if final_norm else jnp.ones((d,), F32)).reshape(1, d).astype(F32)
    if len(xs) == 2:
        x_specs = _half_specs(tb, d, tps, lambda i: (i // tps, i % tps))
    else:
        x_specs = [pl.BlockSpec((1, tb, d), lambda i: (i // tps, i % tps, 0))]
    return pl.pallas_call(
        functools.partial(_combine_kernel, tb=tb, final_norm=final_norm, n_x=len(xs), tiles_per_seq=tps),
        grid=(n_tiles,),
        in_specs=[pl.BlockSpec((1, 2, tb), lambda i: (i, 0, 0), memory_space=pltpu.SMEM)] + x_specs + [
                  pl.BlockSpec((1, 6, d), lambda i: (bidx(i), 0, 0)),
                  pl.BlockSpec((tb, LANES), lambda i: (i, 0)),
                  pl.BlockSpec((1, d), lambda i: (0, 0)),
                  pl.BlockSpec(memory_space=pl.ANY)],
        out_specs=pl.BlockSpec((tb, d), lambda i: (i, 0)),
        out_shape=jax.ShapeDtypeStruct((n, d), F32),
        scratch_shapes=[pltpu.VMEM((2, tb, d // LANES, LANES), F32), pltpu.SemaphoreType.DMA(())],
        compiler_params=_params("arbitrary"),
        name="moe_combine",
    )(pos, *xs, mod, info, gain, ys)


def _moe(xs, mod, ng, w_router, w1, w3, w2, layer, per_batch_mod, final_g=None):
    bsz, t_len, d = xs[0].shape
    t_len *= len(xs)
    n = bsz * t_len
    n_exp = w_router.shape[1]
    hn, info, info_t, cnt = _router(xs, mod, ng, w_router, per_batch_mod)
    info = info.reshape(n, LANES)
    e1, e2 = info_t[INFO_E1].astype(jnp.int32), info_t[INFO_E2].astype(jnp.int32)
    r1, r2 = info_t[INFO_R1].astype(jnp.int32), info_t[INFO_R2].astype(jnp.int32)
    counts = cnt[0, :n_exp].astype(jnp.int32)
    padded = ((counts + GROUP_TILE - 1) // GROUP_TILE) * GROUP_TILE
    ends = jnp.cumsum(padded)
    starts = ends - padded
    tb = _tile(t_len // len(xs), MOVE_TILE)
    pos = jnp.stack([(starts[e1] + r1).reshape(n // tb, tb), (starts[e2] + r2).reshape(n // tb, tb)], axis=1)
    s_max = 2 * n + n_exp * GROUP_TILE
    tile_row = jnp.arange(s_max // GROUP_TILE, dtype=jnp.int32) * GROUP_TILE
    tv = (tile_row < ends[-1]).astype(jnp.int32)
    te = jnp.minimum(jnp.searchsorted(ends, tile_row, side="right"), n_exp - 1).astype(jnp.int32)
    tx = (jnp.minimum(tile_row, ends[-1] - GROUP_TILE) // GROUP_TILE).astype(jnp.int32)
    region_end = ((tile_row + GROUP_TILE)[:, None] == ends[None, :]) & (padded > 0)[None, :]
    zmask = jnp.maximum(1 - tv, jnp.any(region_end, axis=1).astype(jnp.int32))
    rows = _dispatch(zmask, pos, hn.reshape(n, d // LANES, LANES), s_max)
    ys = _group_ffn(te, tx, tv, rows, w1.astype(BF16), w3.astype(BF16), w2.astype(BF16), layer)
    out = _combine(pos, xs, mod, info, ys, per_batch_mod, final_g)
    return out.reshape(bsz, t_len, d)


def _dft_channel_kernel(x_ref, mod_ref, ng_ref, cs_ref, y_ref, *, gw):
    hn = _norm_mod(x_ref[0], ng_ref[...], mod_ref[0, 1:2, :], mod_ref[0, 0:1, :]).astype(BF16)
    for g in range(hn.shape[1] // gw):
        y = _dot(hn[:, g * gw:(g + 1) * gw], cs_ref[...])
        y_ref[0, 0, :, g * gw:(g + 1) * gw] = y[:, :gw].astype(BF16)
        y_ref[0, 1, :, g * gw:(g + 1) * gw] = y[:, gw:].astype(BF16)


REV_TILE = 128


def _reverse_shift(src_tiles, wrap_row, m1):
    n_t = len(src_tiles)
    first = lax.broadcasted_iota(jnp.int32, (REV_TILE, 1), 0) == 0
    out = []
    for a in range(n_t):
        body = _dot(m1, src_tiles[n_t - 1 - a])
        head = wrap_row if a == 0 else src_tiles[n_t - a][0:1, :]
        out.append(jnp.where(first, head.astype(F32), body))
    return out


def _dft_fold_kernel(ya_ref, yb_ref, yn_ref, m1_ref, f_ref):
    n_t = ya_ref.shape[2] // REV_TILE
    keep = jnp.where(pl.program_id(1) == 0, 0.0, 1.0)
    for plane, sign in ((0, 1.0), (1, -1.0)):
        tiles = [yb_ref[0, plane, a * REV_TILE:(a + 1) * REV_TILE, :] for a in range(n_t)]
        wrap = yn_ref[0, plane, 0:1, :].astype(F32) * keep
        rev = _reverse_shift(tiles, wrap, m1_ref[...])
        for a in range(n_t):
            rows = slice(a * REV_TILE, (a + 1) * REV_TILE)
            f_ref[0, plane, rows, :] = (ya_ref[0, plane, rows, :].astype(F32) + sign * rev[a]).astype(BF16)


def _dft_time_kernel(wc_ref, ws_ref, wcx_ref, wsx_ref, f_ref, yh_ref, xlo_ref, xhi_ref, mod_ref, fw_ref, fb_ref,
                     m1_ref, lo_ref, hi_ref):
    rk = wc_ref.shape[0]
    ec, od = f_ref[0, 0], f_ref[0, 1]
    y_half = yh_ref[0, 0, 0:1, :].astype(F32) * ((2 * f_ref.shape[2]) ** -0.5)
    parity = lax.broadcasted_iota(jnp.int32, (rk, 1), 0) % 2
    p = _dot(wc_ref[...], ec) + jnp.where(parity == 0, 1.0, -1.0) * y_half
    q = _dot(ws_ref[...], od)
    gate = mod_ref[0, 2:3, :]

    def project(z, x):
        return x + gate * (_dot(z, fw_ref[...]) + fb_ref[...])

    lo_ref[0] = project((p - q).astype(BF16), xlo_ref[0])
    px = _dot(wcx_ref[...], ec)[0:1, :] + y_half
    qx = _dot(wsx_ref[...], od)[0:1, :]
    src = (p + q).astype(BF16)
    tiles = [src[a * REV_TILE:(a + 1) * REV_TILE, :] for a in range(rk // REV_TILE)]
    rev = _reverse_shift(tiles, (px + qx).astype(BF16), m1_ref[...])
    hi_ref[0] = project(jnp.concatenate(rev, axis=0).astype(BF16), xhi_ref[0])


def _dft_matrix(n, scale):
    idx = (np.arange(n)[:, None] * np.arange(n)[None, :]) % n
    ang = 2.0 * np.pi * idx.astype(np.float64) / n
    return np.cos(ang) * scale, np.sin(ang) * scale


def _fourier_mixer(x, mod, ng, fn_w, fn_b, per_batch_mod):
    bsz, t_len, d = x.shape
    gw = d // N_GROUPS
    cg, sg = _dft_matrix(gw, gw ** -0.5)
    cs = jnp.asarray(np.concatenate([cg, sg], axis=1).astype(np.float32)).astype(BF16)
    tm = _tile(t_len, 512)
    bidx = (lambda b: b) if per_batch_mod else (lambda b: 0)
    y = pl.pallas_call(
        functools.partial(_dft_channel_kernel, gw=gw),
        grid=(bsz, t_len // tm),
        in_specs=[pl.BlockSpec((1, tm, d), lambda b, i: (b, i, 0)),
                  pl.BlockSpec((1, 6, d), lambda b, i: (bidx(b), 0, 0)),
                  pl.BlockSpec((1, d), lambda b, i: (0, 0)),
                  pl.BlockSpec((gw, 2 * gw), lambda b, i: (0, 0))],
        out_specs=pl.BlockSpec((1, 2, tm, d), lambda b, i: (b, 0, i, 0)),
        out_shape=jax.ShapeDtypeStruct((bsz, 2, t_len, d), BF16),
        compiler_params=_params("parallel", "parallel"),
        name="dft_channel",
    )(x, mod, ng.reshape(1, d), cs)
    half = t_len // 2
    assert half % REV_TILE == 0 and half % DFT_SPLIT == 0
    pos = np.arange(REV_TILE)
    m1 = jnp.asarray((pos[None, :] == REV_TILE - pos[:, None]).astype(np.float32)).astype(BF16)
    rf = _tile(half, 512)
    nb_f = t_len // rf
    sub = 16
    folded = pl.pallas_call(
        _dft_fold_kernel,
        grid=(bsz, half // rf),
        in_specs=[pl.BlockSpec((1, 2, rf, d), lambda b, i: (b, 0, i, 0)),
                  pl.BlockSpec((1, 2, rf, d), lambda b, i: (b, 0, nb_f - 1 - i, 0)),
                  pl.BlockSpec((1, 2, sub, d), lambda b, i: (b, 0, ((nb_f - i) % nb_f) * (rf // sub), 0)),
                  pl.BlockSpec((REV_TILE, REV_TILE), lambda b, i: (0, 0))],
        out_specs=pl.BlockSpec((1, 2, rf, d), lambda b, i: (b, 0, i, 0)),
        out_shape=jax.ShapeDtypeStruct((bsz, 2, half, d), BF16),
        compiler_params=_params("parallel", "parallel"),
        name="dft_fold",
    )(y, y, y, m1)
    kk = np.arange(half + sub)[:, None]
    ang_hi = 2.0 * np.pi * ((kk * DFT_SPLIT * np.arange(half // DFT_SPLIT)[None, :]) % t_len) / t_len
    ang_lo = 2.0 * np.pi * ((kk * np.arange(DFT_SPLIT)[None, :]) % t_len) / t_len
    scale = t_len ** -0.5
    c_hi, s_hi = (jnp.asarray((f(ang_hi) * scale).astype(np.float32))[:, :, None] for f in (np.cos, np.sin))
    c_lo, s_lo = (jnp.asarray(f(ang_lo).astype(np.float32))[:, None, :] for f in (np.cos, np.sin))
    wc = (c_hi * c_lo - s_hi * s_lo).reshape(half + sub, half).astype(BF16)
    ws = (s_hi * c_lo + c_hi * s_lo).reshape(half + sub, half).astype(BF16)
    rk = _tile(half, 512)
    nk = half // rk
    lo, hi = pl.pallas_call(
        _dft_time_kernel,
        grid=(bsz, nk),
        in_specs=[pl.BlockSpec((rk, half), lambda b, i: (i, 0)),
                  pl.BlockSpec((rk, half), lambda b, i: (i, 0)),
                  pl.BlockSpec((sub, half), lambda b, i: ((i + 1) * (rk // sub), 0)),
                  pl.BlockSpec((sub, half), lambda b, i: ((i + 1) * (rk // sub), 0)),
                  pl.BlockSpec((1, 2, half, d), lambda b, i: (b, 0, 0, 0)),
                  pl.BlockSpec((1, 1, sub, d), lambda b, i: (b, 0, half // sub, 0)),
                  pl.BlockSpec((1, rk, d), lambda b, i: (b, i, 0)),
                  pl.BlockSpec((1, rk, d), lambda b, i: (b, 2 * nk - 1 - i, 0)),
                  pl.BlockSpec((1, 6, d), lambda b, i: (bidx(b), 0, 0)),
                  pl.BlockSpec((d, d), lambda b, i: (0, 0)),
                  pl.BlockSpec((1, d), lambda b, i: (0, 0)),
                  pl.BlockSpec((REV_TILE, REV_TILE), lambda b, i: (0, 0))],
        out_specs=(pl.BlockSpec((1, rk, d), lambda b, i: (b, i, 0)),
                   pl.BlockSpec((1, rk, d), lambda b, i: (b, nk - 1 - i, 0))),
        out_shape=(jax.ShapeDtypeStruct((bsz, half, d), F32), jax.ShapeDtypeStruct((bsz, half, d), F32)),
        compiler_params=_params("parallel", "parallel"),
        name="dft_time",
    )(wc, ws, wc, ws, folded, y, x, x, mod, fn_w.astype(BF16), fn_b.reshape(1, d).astype(F32), m1)
    return lo, hi


def _final_norm_kernel(x_ref, g_ref, o_ref):
    x = x_ref[0]
    o_ref[0] = x * lax.rsqrt(jnp.mean(x * x, axis=-1, keepdims=True) + EPS) * g_ref[...]


def _final_norm(x, g):
    bsz, t_len, d = x.shape
    tm = _tile(t_len, 1024)
    return pl.pallas_call(
        _final_norm_kernel,
        grid=(bsz, t_len // tm),
        in_specs=[pl.BlockSpec((1, tm, d), lambda b, i: (b, i, 0)), pl.BlockSpec((1, d), lambda b, i: (0, 0))],
        out_specs=pl.BlockSpec((1, tm, d), lambda b, i: (b, i, 0)),
        out_shape=jax.ShapeDtypeStruct((bsz, t_len, d), F32),
        compiler_params=_params("parallel", "parallel"),
        name="final_norm",
    )(x, g.reshape(1, d))


def _trunk(x, mod, cache, use_rope, per_batch_mod, p):
    bsz, t_len, d = x.shape
    depth = mod.shape[0]
    flat = (lambda a: a) if per_batch_mod else (lambda a: a.reshape(1, bsz * t_len, a.shape[-1]))
    unflat = (lambda a: a) if per_batch_mod else (lambda a: a.reshape(bsz, t_len, a.shape[-1]))
    states = []
    for i in range(depth):
        j = i // 2
        m_i = mod[i]
        if i % 2 == 0:
            q, kt, v, o, r = _mlstm_inproj(x, m_i, p["norm_g"][i, 0], p["ml_w_in"][j], p["ml_b_gate"][j],
                                           use_rope, per_batch_mod)
            outs = []
            for direction in range(2):
                if cache is None:
                    cn0 = jnp.zeros((bsz, N_HEADS // 2, 2 * HEAD_DK, 2 * HEAD_DV), F32)
                    m0 = jnp.zeros((bsz, N_HEADS, LANES), F32)
                else:
                    cn0, m0 = _pack_state(cache[0][:, j, direction], cache[1][:, j, direction],
                                          cache[2][:, j, direction])
                outs.append(_mlstm_scan(q, kt, v, r, cn0, m0, reverse=bool(direction)))
            states.append([_unpack_state(cn, m) for (_, cn, m) in outs])
            x = unflat(_mlstm_out_ffn(flat(outs[0][0]), flat(outs[1][0]), flat(o), flat(x), m_i, p["ml_head_g"][j],
                                      p["ml_w_out"][j], p["norm_g"][i, 1], p["ffn_w1"][j], p["ffn_w3"][j],
                                      p["ffn_w2"][j], per_batch_mod))
        else:
            halves = _fourier_mixer(x, m_i, p["norm_g"][i, 0], p["fn_w"][j], p["fn_b"][j], per_batch_mod)
            in_place = per_batch_mod and (t_len // 2) % ROUTE_TILE == 0 and (t_len // 2) % MOVE_TILE == 0
            xs = halves if in_place else (flat(jnp.concatenate(halves, axis=1)),)
            closing = p["final_g"] if i == depth - 1 else None
            x = unflat(_moe(xs, m_i, p["norm_g"][i, 1], p["moe_router"][j], p["moe_w1"], p["moe_w3"],
                            p["moe_w2"], j, per_batch_mod, closing))
    if depth % 2 == 1:
        x = unflat(_final_norm(flat(x), p["final_g"]))
    return x, states


def kernel(x_prompt, x_sample, state_C, state_n, state_m, c, c_ctx, w_mod, b_mod, norm_g, final_g,
           ml_w_in, ml_b_gate, ml_head_g, ml_w_out, fn_w, fn_b, ffn_w1, ffn_w3, ffn_w2,
           moe_router, moe_w1, moe_w3, moe_w2):
    p = dict(norm_g=norm_g, final_g=final_g, ml_w_in=ml_w_in, ml_b_gate=ml_b_gate, ml_head_g=ml_head_g,
             ml_w_out=ml_w_out, fn_w=fn_w, fn_b=fn_b, ffn_w1=ffn_w1, ffn_w3=ffn_w3, ffn_w2=ffn_w2,
             moe_router=moe_router, moe_w1=moe_w1, moe_w3=moe_w3, moe_w2=moe_w2)
    depth, d = w_mod.shape[0], w_mod.shape[1]
    n_dec = c.shape[0]
    rows = ((n_dec + 1 + 7) // 8) * 8
    cond = jnp.zeros((rows, d), F32).at[:n_dec].set(c).at[n_dec].set(c_ctx)
    mod = _mod_table(cond, w_mod, b_mod).reshape(depth, rows, 6, d)
    y_prompt, st = _trunk(x_prompt, mod[:, n_dec:n_dec + 1], None, False, False, p)
    y_sample, _ = _trunk(x_sample, mod[:, :n_dec], (state_C, state_n, state_m), True, True, p)
    new_c = jnp.stack([jnp.stack([s[0][0], s[1][0]], axis=1) for s in st], axis=1)
    new_n = jnp.stack([jnp.stack([s[0][1], s[1][1]], axis=1) for s in st], axis=1)
    new_m = jnp.stack([jnp.stack([s[0][2], s[1][2]], axis=1) for s in st], axis=1)
    return (y_prompt, y_sample, new_c.astype(x_prompt.dtype), new_n.astype(x_prompt.dtype),
            new_m.astype(x_prompt.dtype))
```

```python
import functools

import numpy as np
import jax
import jax.numpy as jnp
from jax import lax
from jax.experimental import pallas as pl
from jax.experimental.pallas import tpu as pltpu

F32 = jnp.float32
BF16 = jnp.bfloat16
HIGHEST = lax.Precision.HIGHEST

EPS = 1e-6
N_HEADS = 8
HEAD_DK = 64
HEAD_DV = 128
QK_W = N_HEADS * HEAD_DK
V_W = N_HEADS * HEAD_DV
GRID_W = 64
ROPE_BASE = 10000.0
N_GROUPS = 4
DFT_SPLIT = 64
N_GATES = 4 * N_HEADS
LANES = 128
SCAN_CHUNK = 128
SCAN_ROWS = 3 * N_HEADS
VMEM_LIMIT = 56 * 1024 * 1024
NEG_INF = float("-inf")


def _params(*sem):
    return pltpu.CompilerParams(dimension_semantics=sem, vmem_limit_bytes=VMEM_LIMIT)


def _tile(n, pref):
    t = min(n, pref)
    assert n % t == 0, (n, pref)
    return t


def _norm_mod(x, gain, scale, shift):
    ms = jnp.mean(x * x, axis=-1, keepdims=True)
    return x * lax.rsqrt(ms + EPS) * gain * (1.0 + scale) + shift


def _dot(a, b):
    return jnp.dot(a, b, preferred_element_type=F32)


def _dot_nt(a, b):
    return lax.dot_general(a, b, (((1,), (1,)), ((), ())), preferred_element_type=F32)


def _exact_rows_dot(x, ones_mat):
    hi = x.astype(BF16)
    rest = x - hi.astype(F32)
    mid = rest.astype(BF16)
    lo = (rest - mid.astype(F32)).astype(BF16)
    n = x.shape[0]
    out = _dot(jnp.concatenate([hi, mid, lo, jnp.zeros_like(hi)], axis=0), ones_mat)
    return out[0:n] + out[n:2 * n] + out[2 * n:3 * n]


def _log_sigmoid(x):
    return jnp.minimum(x, 0.0) - jnp.log1p(jnp.exp(-jnp.abs(x)))


def _mod_kernel(c_ref, w_ref, b_ref, o_ref):
    c = c_ref[...]
    s = c * jax.nn.sigmoid(c)
    o_ref[0] = jnp.dot(s, w_ref[0], preferred_element_type=F32, precision=HIGHEST) + b_ref[0]


def _mod_table(cond, w_mod, b_mod):
    depth, d, n = w_mod.shape
    rows = cond.shape[0]
    tn = _tile(n, 1536)
    return pl.pallas_call(
        _mod_kernel,
        grid=(depth, n // tn),
        in_specs=[pl.BlockSpec((rows, d), lambda l, j: (0, 0)),
                  pl.BlockSpec((1, d, tn), lambda l, j: (l, 0, j)),
                  pl.BlockSpec((1, 1, tn), lambda l, j: (l, 0, j))],
        out_specs=pl.BlockSpec((1, rows, tn), lambda l, j: (l, 0, j)),
        out_shape=jax.ShapeDtypeStruct((depth, rows, n), F32),
        compiler_params=_params("parallel", "parallel"),
        name="adaln_table",
    )(cond, w_mod, b_mod.reshape(depth, 1, n))


def _rope_tables(t_len):
    pos = np.arange(t_len)
    row = (pos // GRID_W).astype(np.float32)
    col = (pos % GRID_W).astype(np.float32)
    nf = HEAD_DK // 4
    inv = (np.float32(ROPE_BASE) ** (-np.arange(nf, dtype=np.float32) / nf)).astype(np.float32)
    d = np.arange(HEAD_DK)
    p = np.where(d[None, :] < HEAD_DK // 2, row[:, None], col[:, None]).astype(np.float32)
    ang = p * inv[d % nf][None, :]
    sign = np.where((d % (2 * nf)) < nf, -1.0, 1.0).astype(np.float32)
    return np.cos(ang).astype(np.float32), (np.sin(ang) * sign[None, :]).astype(np.float32)


def _inproj_kernel(*refs, use_rope):
    if use_rope:
        (x_ref, mod_ref, ng_ref, wq_ref, wkt_ref, wv_ref, wo_ref, wgt_ref, bg_ref, trif_ref, trib_ref,
         cq_ref, sq_ref, ck_ref, sk_ref, q_ref, kt_ref, v_ref, o_ref, r_ref) = refs
    else:
        (x_ref, mod_ref, ng_ref, wq_ref, wkt_ref, wv_ref, wo_ref, wgt_ref, bg_ref, trif_ref, trib_ref,
         q_ref, kt_ref, v_ref, o_ref, r_ref) = refs
    hn = _norm_mod(x_ref[0], ng_ref[...], mod_ref[0, 1:2, :], mod_ref[0, 0:1, :])
    hb = hn.astype(BF16)
    q = _dot(hb, wq_ref[...]) * (HEAD_DK ** -0.5)
    kt = _dot_nt(wkt_ref[...], hb)
    if use_rope:
        nf = HEAD_DK // 4
        lane = lax.broadcasted_iota(jnp.int32, (q.shape[0], LANES), 1)
        first_q = (lane % (2 * nf)) < nf
        sub = lax.broadcasted_iota(jnp.int32, (LANES, kt.shape[1]), 0)
        first_k = (sub % (2 * nf)) < nf
        cq, sq, ck, sk = cq_ref[...], sq_ref[...], ck_ref[...], sk_ref[...]
        for s in range(QK_W // LANES):
            qs = q[:, s * LANES:(s + 1) * LANES]
            sw = jnp.where(first_q, pltpu.roll(qs, LANES - nf, 1), pltpu.roll(qs, nf, 1))
            q_ref[0, :, s * LANES:(s + 1) * LANES] = (qs * cq + sw * sq).astype(BF16)
            ks = kt[s * LANES:(s + 1) * LANES, :]
            sw = jnp.where(first_k, pltpu.roll(ks, LANES - nf, 0), pltpu.roll(ks, nf, 0))
            kt_ref[0, s * LANES:(s + 1) * LANES, :] = (ks * ck + sw * sk).astype(BF16)
    else:
        q_ref[0] = q.astype(BF16)
        kt_ref[0] = kt.astype(BF16)
    v_ref[0] = _dot(hb, wv_ref[...]).astype(BF16)
    o_ref[0] = _dot(hb, wo_ref[...]).astype(BF16)
    gt = _dot_nt(wgt_ref[...], hb) + bg_ref[...]
    h = N_HEADS
    i_f, f_f = gt[0:h], _log_sigmoid(gt[h:2 * h])
    i_b, f_b = gt[2 * h:3 * h], _log_sigmoid(gt[3 * h:4 * h])
    b_f = _exact_rows_dot(f_f, trif_ref[...])
    b_b = _exact_rows_dot(f_b, trib_ref[...])
    a_f, a_b = i_f - b_f, i_b - b_b
    width = a_f.shape[1]
    in_chunk = lax.broadcasted_iota(jnp.int32, a_f.shape, 1) % SCAN_CHUNK
    c_f, c_b = a_f, a_b
    step = 1
    while step < SCAN_CHUNK:
        c_f = jnp.where(in_chunk >= step, jnp.maximum(c_f, pltpu.roll(c_f, step, 1)), c_f)
        c_b = jnp.where(in_chunk < SCAN_CHUNK - step, jnp.maximum(c_b, pltpu.roll(c_b, width - step, 1)), c_b)
        step *= 2
    for k, rows in enumerate((a_f, b_f, c_f, a_b, b_b, c_b)):
        r_ref[0, k * h:(k + 1) * h, :] = rows


def _mlstm_inproj(x, mod, ng, w_in, b_gate, use_rope, per_batch_mod):
    bsz, t_len, d = x.shape
    tm = _tile(t_len, 512)
    wq = w_in[:, :QK_W].astype(BF16)
    wkt = w_in[:, QK_W:2 * QK_W].T.astype(BF16)
    wv = w_in[:, 2 * QK_W:2 * QK_W + V_W].astype(BF16)
    wo = w_in[:, 2 * QK_W + V_W:2 * QK_W + 2 * V_W].astype(BF16)
    wgt = w_in[:, 2 * QK_W + 2 * V_W:].T.astype(BF16)
    bg = b_gate.reshape(N_GATES, 1).astype(F32)
    pos = np.arange(tm)
    same = (pos[:, None] // SCAN_CHUNK) == (pos[None, :] // SCAN_CHUNK)
    trif = jnp.asarray((same & (pos[:, None] <= pos[None, :])).astype(np.float32)).astype(BF16)
    trib = jnp.asarray((same & (pos[:, None] >= pos[None, :])).astype(np.float32)).astype(BF16)
    bidx = (lambda b: b) if per_batch_mod else (lambda b: 0)
    const = lambda shp: pl.BlockSpec(shp, lambda b, i: (0,) * len(shp))
    in_specs = [pl.BlockSpec((1, tm, d), lambda b, i: (b, i, 0)),
                pl.BlockSpec((1, 6, d), lambda b, i: (bidx(b), 0, 0)),
                const((1, d)), const((d, QK_W)), const((QK_W, d)), const((d, V_W)), const((d, V_W)),
                const((N_GATES, d)), const((N_GATES, 1)), const((tm, tm)), const((tm, tm))]
    args = [x, mod, ng.reshape(1, d), wq, wkt, wv, wo, wgt, bg, trif, trib]
    if use_rope:
        cos, sin = _rope_tables(t_len)
        rep = LANES // HEAD_DK
        args += [jnp.asarray(np.tile(cos, (1, rep))), jnp.asarray(np.tile(sin, (1, rep))),
                 jnp.asarray(np.tile(cos.T, (rep, 1))), jnp.asarray(np.tile(sin.T, (rep, 1)))]
        in_specs += [pl.BlockSpec((tm, LANES), lambda b, i: (i, 0)), pl.BlockSpec((tm, LANES), lambda b, i: (i, 0)),
                     pl.BlockSpec((LANES, tm), lambda b, i: (0, i)), pl.BlockSpec((LANES, tm), lambda b, i: (0, i))]
    out_shape = (jax.ShapeDtypeStruct((bsz, t_len, QK_W), BF16),
                 jax.ShapeDtypeStruct((bsz, QK_W, t_len), BF16),
                 jax.ShapeDtypeStruct((bsz, t_len, V_W), BF16),
                 jax.ShapeDtypeStruct((bsz, t_len, V_W), BF16),
                 jax.ShapeDtypeStruct((bsz, 2 * SCAN_ROWS, t_len), F32))
    out_specs = (pl.BlockSpec((1, tm, QK_W), lambda b, i: (b, i, 0)),
                 pl.BlockSpec((1, QK_W, tm), lambda b, i: (b, 0, i)),
                 pl.BlockSpec((1, tm, V_W), lambda b, i: (b, i, 0)),
                 pl.BlockSpec((1, tm, V_W), lambda b, i: (b, i, 0)),
                 pl.BlockSpec((1, 2 * SCAN_ROWS, tm), lambda b, i: (b, 0, i)))
    return pl.pallas_call(
        functools.partial(_inproj_kernel, use_rope=use_rope),
        grid=(bsz, t_len // tm), in_specs=in_specs, out_specs=out_specs, out_shape=out_shape,
        compiler_params=_params("parallel", "parallel"),
        name="mlstm_inproj",
    )(*args)


def _scan_kernel(q_ref, kt_ref, v_ref, r_ref, cn0_ref, m0_ref, spread_ref, h_ref, cn_out_ref, m_out_ref,
                 cn_scr, m_scr, *, reverse, n_chunks):
    L = SCAN_CHUNK
    j = pl.program_id(1)

    @pl.when(j == 0)
    def _():
        cn_scr[...] = cn0_ref[0]
        m_scr[...] = m0_ref[0]

    row_i = lax.broadcasted_iota(jnp.int32, (L, L), 0)
    col_i = lax.broadcasted_iota(jnp.int32, (L, L), 1)
    visible = (col_i >= row_i) if reverse else (col_i <= row_i)
    upper_lanes = lax.broadcasted_iota(jnp.int32, (L, LANES), 1) >= HEAD_DK
    ones = jnp.ones((L, LANES), BF16)
    no_rows = jnp.zeros((HEAD_DK, 2 * HEAD_DV), BF16)
    end_lane = lax.broadcasted_iota(jnp.int32, (1, L), 1) == (0 if reverse else L - 1)
    order = range(n_chunks - 1, -1, -1) if reverse else range(n_chunks)
    head_rows = [slice((h % 2) * HEAD_DK, (h % 2 + 1) * HEAD_DK) for h in range(N_HEADS)]
    states = [(cn_scr[h // 2, head_rows[h], :], m_scr[h:h + 1, 0:1]) for h in range(N_HEADS)]

    def columns(c):
        x = r_ref[0, N_HEADS:3 * N_HEADS, c * L:(c + 1) * L]
        hi = x.astype(BF16).astype(F32)
        mid = (x - hi).astype(BF16).astype(F32)
        lo = x - hi - mid
        parts = jnp.concatenate([hi, mid, lo, jnp.zeros((L - 6 * N_HEADS, L), F32)], axis=0)
        return _dot(parts.T.astype(BF16), spread_ref[...])

    cols = {c: columns(c) for c in order}
    new_states = []
    for h in range(N_HEADS):
        p, e = h // 2, h % 2
        cn, m = states[h]
        for c in order:
            r0 = c * L
            q_pair = q_ref[0, r0:r0 + L, p * LANES:(p + 1) * LANES]
            q_m = jnp.where(upper_lanes if e else jnp.logical_not(upper_lanes), q_pair, jnp.zeros_like(q_pair))
            kt_pair = kt_ref[0, p * LANES:(p + 1) * LANES, r0:r0 + L]
            kt_h = kt_ref[0, h * HEAD_DK:(h + 1) * HEAD_DK, r0:r0 + L]
            a_row = r_ref[0, h:h + 1, r0:r0 + L]
            b_row = r_ref[0, N_HEADS + h:N_HEADS + h + 1, r0:r0 + L]
            v_aug = jnp.concatenate([v_ref[0, r0:r0 + L, h * HEAD_DV:(h + 1) * HEAD_DV], ones], axis=1)
            am = jnp.where(visible, a_row, NEG_INF)
            a_max = jnp.max(a_row, axis=1, keepdims=True)
            kw = (kt_h.astype(F32) * jnp.exp(a_row - a_max)).astype(BF16)
            upd = _dot(kw, v_aug)
            b_col = cols[c][:, h * LANES:(h + 1) * LANES]
            b_end = jnp.sum(jnp.where(end_lane, b_row, 0.0), axis=1, keepdims=True)
            u = jnp.maximum(m, cols[c][:, (N_HEADS + h) * LANES:(N_HEADS + h + 1) * LANES])
            guard = jnp.exp(-(b_col + u))
            s = (_dot(q_m, kt_pair) * jnp.exp(am - u)).astype(BF16)
            q_old = (q_m.astype(F32) * jnp.exp(m - u)).astype(BF16)
            cn_b = cn.astype(BF16)
            cn_pair = jnp.concatenate([no_rows, cn_b] if e else [cn_b, no_rows], axis=0)
            both = _dot(jnp.concatenate([q_old, s], axis=1), jnp.concatenate([cn_pair, v_aug], axis=0))
            num, den = both[:, :HEAD_DV], both[:, HEAD_DV:]
            h_ref[0, r0:r0 + L, h * HEAD_DV:(h + 1) * HEAD_DV] = (num / jnp.maximum(jnp.abs(den), guard)).astype(BF16)
            u_end = jnp.maximum(m, a_max)
            cn = jnp.exp(m - u_end) * cn + jnp.exp(a_max - u_end) * upd
            m = b_end + u_end
        new_states.append((cn, m))
    for h in range(N_HEADS):
        cn_scr[h // 2, head_rows[h], :] = new_states[h][0]
        m_scr[h:h + 1, :] = jnp.broadcast_to(new_states[h][1], (1, LANES))

    @pl.when(j == pl.num_programs(1) - 1)
    def _():
        cn_out_ref[0] = cn_scr[...]
        m_out_ref[0] = m_scr[...]


def _mlstm_scan(q, kt, v, r, cn0, m0, reverse):
    bsz, t_len, _ = q.shape
    tb = _tile(t_len, 4 * SCAN_CHUNK)
    nblk = t_len // tb
    blk = (lambda j: nblk - 1 - j) if reverse else (lambda j: j)
    d = 1 if reverse else 0
    half = N_HEADS // 2
    src = np.arange(LANES)[:, None]
    dst = np.arange(2 * N_HEADS * LANES)[None, :]
    spread = jnp.asarray(((src < 6 * N_HEADS) & (src % (2 * N_HEADS) == dst // LANES)).astype(np.float32)).astype(BF16)
    return pl.pallas_call(
        functools.partial(_scan_kernel, reverse=reverse, n_chunks=tb // SCAN_CHUNK),
        grid=(bsz, nblk),
        in_specs=[pl.BlockSpec((1, tb, QK_W), lambda b, j: (b, blk(j), 0)),
                  pl.BlockSpec((1, QK_W, tb), lambda b, j: (b, 0, blk(j))),
                  pl.BlockSpec((1, tb, V_W), lambda b, j: (b, blk(j), 0)),
                  pl.BlockSpec((1, SCAN_ROWS, tb), lambda b, j: (b, d, blk(j))),
                  pl.BlockSpec((1, half, 2 * HEAD_DK, 2 * HEAD_DV), lambda b, j: (b, 0, 0, 0)),
                  pl.BlockSpec((1, N_HEADS, LANES), lambda b, j: (b, 0, 0)),
                  pl.BlockSpec((LANES, 2 * N_HEADS * LANES), lambda b, j: (0, 0))],
        out_specs=(pl.BlockSpec((1, tb, V_W), lambda b, j: (b, blk(j), 0)),
                   pl.BlockSpec((1, half, 2 * HEAD_DK, 2 * HEAD_DV), lambda b, j: (b, 0, 0, 0)),
                   pl.BlockSpec((1, N_HEADS, LANES), lambda b, j: (b, 0, 0))),
        out_shape=(jax.ShapeDtypeStruct((bsz, t_len, V_W), BF16),
                   jax.ShapeDtypeStruct((bsz, half, 2 * HEAD_DK, 2 * HEAD_DV), F32),
                   jax.ShapeDtypeStruct((bsz, N_HEADS, LANES), F32)),
        scratch_shapes=[pltpu.VMEM((half, 2 * HEAD_DK, 2 * HEAD_DV), F32),
                        pltpu.VMEM((N_HEADS, LANES), F32)],
        compiler_params=_params("parallel", "arbitrary"),
        name="mlstm_scan_bwd" if reverse else "mlstm_scan_fwd",
    )(q, kt, v, r, cn0, m0, spread)


def _pack_state(c, n, m):
    bsz = c.shape[0]
    cn = jnp.concatenate([c, jnp.broadcast_to(n[..., None], n.shape + (HEAD_DV,))], axis=-1)
    cn = cn.reshape(bsz, N_HEADS // 2, 2 * HEAD_DK, 2 * HEAD_DV)
    return cn.astype(F32), jnp.broadcast_to(m[..., None], (bsz, N_HEADS, LANES)).astype(F32)


def _unpack_state(cn, m):
    bsz = cn.shape[0]
    cn = cn.reshape(bsz, N_HEADS, HEAD_DK, 2 * HEAD_DV)
    return cn[..., :HEAD_DV], cn[..., HEAD_DV], m[..., 0]


def _mlstm_out_ffn_kernel(hf_ref, hb_ref, o_ref, x_ref, mod_ref, hg_ref, wo_ref, ng_ref, w1_ref, w3_ref, w2_ref,
                          out_ref, x1_scr, hn_scr, acc_scr):
    f = pl.program_id(2)

    @pl.when(f == 0)
    def _():
        hs = hf_ref[0].astype(F32) + hb_ref[0].astype(F32)
        parts = []
        for h in range(N_HEADS):
            z = hs[:, h * HEAD_DV:(h + 1) * HEAD_DV]
            parts.append(z * lax.rsqrt(jnp.mean(z * z, axis=-1, keepdims=True) + EPS))
        gated = jnp.concatenate(parts, axis=1) * hg_ref[...] * jax.nn.sigmoid(o_ref[0].astype(F32))
        x1 = x_ref[0] + mod_ref[0, 2:3, :] * _dot(gated.astype(BF16), wo_ref[...])
        x1_scr[...] = x1
        hn_scr[...] = _norm_mod(x1, ng_ref[...], mod_ref[0, 4:5, :], mod_ref[0, 3:4, :]).astype(BF16)
        acc_scr[...] = jnp.zeros_like(acc_scr)

    hb = hn_scr[...]
    h1 = _dot(hb, w1_ref[...])
    h3 = _dot(hb, w3_ref[...])
    act = (h1 * jax.nn.sigmoid(h1) * h3).astype(BF16)
    acc_scr[...] += _dot(act, w2_ref[...])

    @pl.when(f == pl.num_programs(2) - 1)
    def _():
        out_ref[0] = x1_scr[...] + mod_ref[0, 5:6, :] * acc_scr[...]


def _mlstm_out_ffn(hf, hb, o, x, mod, head_g, w_out, ng, w1, w3, w2, per_batch_mod):
    bsz, t_len, d = x.shape
    d_ff = w1.shape[1]
    tm = _tile(t_len, 512)
    tf = _tile(d_ff, 1792)
    bidx = (lambda b: b) if per_batch_mod else (lambda b: 0)
    tok = lambda w: pl.BlockSpec((1, tm, w), lambda b, i, f: (b, i, 0))
    return pl.pallas_call(
        _mlstm_out_ffn_kernel,
        grid=(bsz, t_len // tm, d_ff // tf),
        in_specs=[tok(V_W), tok(V_W), tok(V_W), tok(d),
                  pl.BlockSpec((1, 6, d), lambda b, i, f: (bidx(b), 0, 0)),
                  pl.BlockSpec((1, V_W), lambda b, i, f: (0, 0)),
                  pl.BlockSpec((V_W, d), lambda b, i, f: (0, 0)),
                  pl.BlockSpec((1, d), lambda b, i, f: (0, 0)),
                  pl.BlockSpec((d, tf), lambda b, i, f: (0, f)),
                  pl.BlockSpec((d, tf), lambda b, i, f: (0, f)),
                  pl.BlockSpec((tf, d), lambda b, i, f: (f, 0))],
        out_specs=tok(d),
        out_shape=jax.ShapeDtypeStruct((bsz, t_len, d), F32),
        scratch_shapes=[pltpu.VMEM((tm, d), F32), pltpu.VMEM((tm, d), BF16), pltpu.VMEM((tm, d), F32)],
        compiler_params=_params("parallel", "parallel", "arbitrary"),
        name="mlstm_out_swiglu",
    )(hf, hb, o, x, mod, head_g.reshape(1, V_W).astype(F32), w_out.astype(BF16), ng.reshape(1, d),
      w1.astype(BF16), w3.astype(BF16), w2.astype(BF16))


ROUTE_TILE = 512
MOVE_TILE = 1024
GROUP_TILE = 512
INFO_E1, INFO_E2, INFO_P1, INFO_P2, INFO_R1, INFO_R2 = range(6)
INFO_ROWS = 8


def _lane_pick(rec, lane, k):
    return jnp.sum(jnp.where(lane == k, rec, 0.0), axis=-1, keepdims=True)


def _token_tile(x_refs, first_half):
    if len(x_refs) == 1:
        return x_refs[0][0]
    return jnp.where(first_half, x_refs[0][0], x_refs[1][0])


def _half_specs(tm, d, tiles_per_seq, locate):
    nh = tiles_per_seq // 2

    def lo(*g):
        b, i = locate(*g)
        return (b, jnp.minimum(i, nh - 1), 0)

    def hi(*g):
        b, i = locate(*g)
        return (b, jnp.maximum(i - nh, 0), 0)

    return [pl.BlockSpec((1, tm, d), lo), pl.BlockSpec((1, tm, d), hi)]


def _router_kernel(*refs, n_exp, n_x):
    x_refs = refs[:n_x]
    mod_ref, ng_ref, wr_ref, tri_ref, hn_ref, info_ref, infot_ref, cnt_ref, carry = refs[n_x:]

    @pl.when((pl.program_id(0) == 0) & (pl.program_id(1) == 0))
    def _():
        carry[...] = jnp.zeros_like(carry)

    x = _token_tile(x_refs, pl.program_id(1) < pl.num_programs(1) // 2)
    hn = _norm_mod(x, ng_ref[...], mod_ref[0, 4:5, :], mod_ref[0, 3:4, :])
    _store_row_tiles(hn_ref, (0,), hn)
    hn_hi = hn.astype(BF16)
    hn_lo = (hn - hn_hi.astype(F32)).astype(BF16)
    hh = _dot(hn_hi, wr_ref[...])
    logits = hh[:, :LANES] + hh[:, LANES:] + _dot(hn_lo, wr_ref[:, :LANES])
    lane = lax.broadcasted_iota(jnp.int32, logits.shape, 1).astype(F32)
    logits = jnp.where(lane < n_exp, logits, NEG_INF)
    v1 = jnp.max(logits, axis=-1, keepdims=True)
    i1 = jnp.min(jnp.where(logits == v1, lane, float(LANES)), axis=-1, keepdims=True)
    rest = jnp.where(lane == i1, NEG_INF, logits)
    v2 = jnp.max(rest, axis=-1, keepdims=True)
    i2 = jnp.min(jnp.where(rest == v2, lane, float(LANES)), axis=-1, keepdims=True)
    e2 = jnp.exp(v2 - v1)
    den = 1.0 + e2
    chosen = jnp.where((lane == i1) | (lane == i2), 1.0, 0.0)
    before = _dot(tri_ref[...], chosen.astype(BF16)) + carry[...]
    r1 = jnp.sum(jnp.where(lane == i1, before, 0.0), axis=-1, keepdims=True)
    r2 = jnp.sum(jnp.where(lane == i2, before, 0.0), axis=-1, keepdims=True)
    total = carry[...] + jnp.sum(chosen, axis=0, keepdims=True)
    carry[...] = total
    cnt_ref[...] = total
    rec = jnp.zeros_like(logits)
    for k, val in ((INFO_E1, i1), (INFO_E2, i2), (INFO_P1, 1.0 / den), (INFO_P2, e2 / den),
                   (INFO_R1, r1), (INFO_R2, r2)):
        rec = jnp.where(lane == k, val, rec)
    info_ref[0] = rec
    infot_ref[...] = rec.T[0:INFO_ROWS, :]


def _router(xs, mod, ng, w_router, per_batch_mod):
    bsz, t_len, d = xs[0].shape
    t_len *= len(xs)
    n_exp = w_router.shape[1]
    tm = _tile(t_len, ROUTE_TILE)
    wr = jnp.zeros((d, LANES), F32).at[:, :n_exp].set(w_router)
    wr_hi = wr.astype(BF16)
    wr = jnp.concatenate([wr_hi, (wr - wr_hi.astype(F32)).astype(BF16)], axis=1)
    pos = np.arange(tm)
    tri = jnp.asarray((pos[None, :] < pos[:, None]).astype(np.float32)).astype(BF16)
    bidx = (lambda b: b) if per_batch_mod else (lambda b: 0)
    if len(xs) == 2:
        x_specs = _half_specs(tm, d, t_len // tm, lambda b, i: (b, i))
    else:
        x_specs = [pl.BlockSpec((1, tm, d), lambda b, i: (b, i, 0))]
    return pl.pallas_call(
        functools.partial(_router_kernel, n_exp=n_exp, n_x=len(xs)),
        grid=(bsz, t_len // tm),
        in_specs=x_specs + [
                  pl.BlockSpec((1, 6, d), lambda b, i: (bidx(b), 0, 0)),
                  pl.BlockSpec((1, d), lambda b, i: (0, 0)),
                  pl.BlockSpec((d, 2 * LANES), lambda b, i: (0, 0)),
                  pl.BlockSpec((tm, tm), lambda b, i: (0, 0))],
        out_specs=(pl.BlockSpec((1, tm, d // LANES, LANES), lambda b, i: (b, i, 0, 0)),
                   pl.BlockSpec((1, tm, LANES), lambda b, i: (b, i, 0)),
                   pl.BlockSpec((INFO_ROWS, tm), lambda b, i: (0, b * (t_len // tm) + i)),
                   pl.BlockSpec((1, LANES), lambda b, i: (0, 0))),
        out_shape=(jax.ShapeDtypeStruct((bsz, t_len, d // LANES, LANES), F32),
                   jax.ShapeDtypeStruct((bsz, t_len, LANES), F32),
                   jax.ShapeDtypeStruct((INFO_ROWS, bsz * t_len), F32),
                   jax.ShapeDtypeStruct((1, LANES), F32)),
        scratch_shapes=[pltpu.VMEM((1, LANES), F32)],
        compiler_params=_params("arbitrary", "arbitrary"),
        name="moe_router",
    )(*xs, mod, ng.reshape(1, d), wr, tri)


def _store_row_tiles(ref, lead, val):
    chunks = jnp.stack([val[:, c * LANES:(c + 1) * LANES] for c in range(val.shape[1] // LANES)], axis=0)
    ref[lead] = jnp.swapaxes(chunks, 0, 1)


def _load_row_tiles(ref, lead):
    n_c = ref.shape[-2]
    x = jnp.swapaxes(ref[lead], 0, 1)
    return jnp.concatenate([x[c] for c in range(n_c)], axis=1)


def _dispatch_kernel(zmask_ref, pos_ref, hn_ref, xs_ref, zeros, sem, *, tb, n_row_tiles):
    def row_copy(r, slot):
        return pltpu.make_async_copy(hn_ref.at[r], xs_ref.at[slot], sem)

    def tile_fill(r):
        row0 = pl.multiple_of(r * GROUP_TILE, GROUP_TILE)
        return pltpu.make_async_copy(zeros, xs_ref.at[pl.ds(row0, GROUP_TILE)], sem)

    @pl.when(pl.program_id(0) == 0)
    def _():
        zeros[...] = jnp.zeros_like(zeros)

        def fill(r, c):
            @pl.when(zmask_ref[r] != 0)
            def _():
                tile_fill(r).start()
            return c

        def fill_done(r, c):
            @pl.when(zmask_ref[r] != 0)
            def _():
                tile_fill(r).wait()
            return c

        lax.fori_loop(0, n_row_tiles, fill, 0)
        lax.fori_loop(0, n_row_tiles, fill_done, 0)

    def issue(r, c):
        row_copy(r, pos_ref[0, 0, r]).start(priority=0)
        row_copy(r, pos_ref[0, 1, r]).start(priority=1)
        return c

    def drain(r, c):
        row_copy(0, 0).wait()
        row_copy(0, 0).wait()
        return c

    lax.fori_loop(0, tb, issue, 0, unroll=8)
    lax.fori_loop(0, tb, drain, 0, unroll=8)


def _dispatch(zmask, pos, hn, s_max):
    n, n_c, _ = hn.shape
    n_tiles, _, tb = pos.shape
    return pl.pallas_call(
        functools.partial(_dispatch_kernel, tb=tb, n_row_tiles=zmask.shape[0]),
        grid_spec=pltpu.PrefetchScalarGridSpec(
            num_scalar_prefetch=1,
            grid=(n_tiles,),
            in_specs=[pl.BlockSpec((1, 2, tb), lambda i, zm: (i, 0, 0), memory_space=pltpu.SMEM),
                      pl.BlockSpec((tb, n_c, LANES), lambda i, zm: (i, 0, 0))],
            out_specs=pl.BlockSpec(memory_space=pl.ANY),
            scratch_shapes=[pltpu.VMEM((GROUP_TILE, n_c, LANES), F32), pltpu.SemaphoreType.DMA(())]),
        out_shape=jax.ShapeDtypeStruct((s_max, n_c, LANES), F32),
        compiler_params=_params("arbitrary"),
        name="moe_dispatch",
    )(zmask, pos, hn)


def _group_ffn_kernel(te_ref, tx_ref, tv_ref, x_ref, w1_ref, w3_ref, w2_ref, o_ref, acc_scr):
    r, f = pl.program_id(0), pl.program_id(1)
    last = pl.num_programs(1) - 1

    @pl.when(tv_ref[r] != 0)
    def _():
        @pl.when(f == 0)
        def _():
            acc_scr[...] = jnp.zeros_like(acc_scr)

        xb = _load_row_tiles(x_ref, ()).astype(BF16)
        h1 = _dot(xb, w1_ref[0, 0])
        h3 = _dot(xb, w3_ref[0, 0])
        act = (h1 * jax.nn.sigmoid(h1) * h3).astype(BF16)
        total = acc_scr[...] + _dot(act, w2_ref[0, 0])
        acc_scr[...] = total
        _store_row_tiles(o_ref, (), total)

    @pl.when((tv_ref[r] == 0) & (f == last))
    def _():
        o_ref[...] = jnp.zeros_like(o_ref)


def _group_ffn(te, tx, tv, xs, w1, w3, w2, layer):
    s_max, n_c, _ = xs.shape
    d = n_c * LANES
    d_ff = w1.shape[3]
    tf = _tile(d_ff, 1792)
    nf = d_ff // tf
    fidx = lambda r, f, tv: f * tv[r] + (nf - 1) * (1 - tv[r])
    return pl.pallas_call(
        _group_ffn_kernel,
        grid_spec=pltpu.PrefetchScalarGridSpec(
            num_scalar_prefetch=3,
            grid=(s_max // GROUP_TILE, nf),
            in_specs=[pl.BlockSpec((GROUP_TILE, n_c, LANES), lambda r, f, te, tx, tv: (tx[r], 0, 0)),
                      pl.BlockSpec((1, 1, d, tf), lambda r, f, te, tx, tv: (layer, te[r], 0, fidx(r, f, tv))),
                      pl.BlockSpec((1, 1, d, tf), lambda r, f, te, tx, tv: (layer, te[r], 0, fidx(r, f, tv))),
                      pl.BlockSpec((1, 1, tf, d), lambda r, f, te, tx, tv: (layer, te[r], fidx(r, f, tv), 0))],
            out_specs=pl.BlockSpec((GROUP_TILE, n_c, LANES), lambda r, f, te, tx, tv: (r, 0, 0)),
            scratch_shapes=[pltpu.VMEM((GROUP_TILE, d), F32)]),
        out_shape=jax.ShapeDtypeStruct((s_max, n_c, LANES), F32),
        compiler_params=_params("parallel", "arbitrary"),
        name="moe_group_swiglu",
    )(te, tx, tv, xs, w1, w3, w2)


def _combine_kernel(*refs, tb, final_norm, n_x, tiles_per_seq):
    pos_ref = refs[0]
    x_refs = refs[1:1 + n_x]
    mod_ref, info_ref, fg_ref, ys_ref, out_ref, buf, sem = refs[1 + n_x:]

    def row_copy(slot, k, r):
        return pltpu.make_async_copy(ys_ref.at[slot], buf.at[k, r], sem)

    def issue(r, c):
        row_copy(pos_ref[0, 0, r], 0, r).start(priority=0)
        row_copy(pos_ref[0, 1, r], 1, r).start(priority=1)
        return c

    def drain(r, c):
        row_copy(0, 0, 0).wait()
        row_copy(0, 1, 0).wait()
        return c

    lax.fori_loop(0, tb, issue, 0, unroll=8)
    lax.fori_loop(0, tb, drain, 0, unroll=8)
    rec = info_ref[...]
    lane = lax.broadcasted_iota(jnp.int32, rec.shape, 1)
    y = (_lane_pick(rec, lane, INFO_P1) * _load_row_tiles(buf, (0,))
         + _lane_pick(rec, lane, INFO_P2) * _load_row_tiles(buf, (1,)))
    x = _token_tile(x_refs, (pl.program_id(0) % tiles_per_seq) < tiles_per_seq // 2)
    out = x + mod_ref[0, 5:6, :] * y
    if final_norm:
        out = out * lax.rsqrt(jnp.mean(out * out, axis=-1, keepdims=True) + EPS) * fg_ref[...]
    out_ref[...] = out


def _combine(pos, xs, mod, info, ys, per_batch_mod, final_g):
    bsz, t_len, d = xs[0].shape
    t_len *= len(xs)
    n = bsz * t_len
    n_tiles, _, tb = pos.shape
    tps = t_len // tb
    bidx = (lambda i: i // tps) if per_batch_mod else (lambda i: 0)
    final_norm = final_g is not None
    gain = (final_g if final_norm else jnp.ones((d,), F32)).reshape(1, d).astype(F32)
    if len(xs) == 2:
        x_specs = _half_specs(tb, d, tps, lambda i: (i // tps, i % tps))
    else:
        x_specs = [pl.BlockSpec((1, tb, d), lambda i: (i // tps, i % tps, 0))]
    return pl.pallas_call(
        functools.partial(_combine_kernel, tb=tb, final_norm=final_norm, n_x=len(xs), tiles_per_seq=tps),
        grid=(n_tiles,),
        in_specs=[pl.BlockSpec((1, 2, tb), lambda i: (i, 0, 0), memory_space=pltpu.SMEM)] + x_specs + [
                  pl.BlockSpec((1, 6, d), lambda i: (bidx(i), 0, 0)),
                  pl.BlockSpec((tb, LANES), lambda i: (i, 0)),
                  pl.BlockSpec((1, d), lambda i: (0, 0)),
                  pl.BlockSpec(memory_space=pl.ANY)],
        out_specs=pl.BlockSpec((tb, d), lambda i: (i, 0)),
        out_shape=jax.ShapeDtypeStruct((n, d), F32),
        scratch_shapes=[pltpu.VMEM((2, tb, d // LANES, LANES), F32), pltpu.SemaphoreType.DMA(())],
        compiler_params=_params("arbitrary"),
        name="moe_combine",
    )(pos, *xs, mod, info, gain, ys)


def _moe(xs, mod, ng, w_router, w1, w3, w2, layer, per_batch_mod, final_g=None):
    bsz, t_len, d = xs[0].shape
    t_len *= len(xs)
    n = bsz * t_len
    n_exp = w_router.shape[1]
    hn, info, info_t, cnt = _router(xs, mod, ng, w_router, per_batch_mod)
    info = info.reshape(n, LANES)
    e1, e2 = info_t[INFO_E1].astype(jnp.int32), info_t[INFO_E2].astype(jnp.int32)
    r1, r2 = info_t[INFO_R1].astype(jnp.int32), info_t[INFO_R2].astype(jnp.int32)
    counts = cnt[0, :n_exp].astype(jnp.int32)
    padded = ((counts + GROUP_TILE - 1) // GROUP_TILE) * GROUP_TILE
    ends = jnp.cumsum(padded)
    starts = ends - padded
    tb = _tile(t_len // len(xs), MOVE_TILE)
    pos = jnp.stack([(starts[e1] + r1).reshape(n // tb, tb), (starts[e2] + r2).reshape(n // tb, tb)], axis=1)
    s_max = 2 * n + n_exp * GROUP_TILE
    tile_row = jnp.arange(s_max // GROUP_TILE, dtype=jnp.int32) * GROUP_TILE
    tv = (tile_row < ends[-1]).astype(jnp.int32)
    te = jnp.minimum(jnp.searchsorted(ends, tile_row, side="right"), n_exp - 1).astype(jnp.int32)
    tx = (jnp.minimum(tile_row, ends[-1] - GROUP_TILE) // GROUP_TILE).astype(jnp.int32)
    region_end = ((tile_row + GROUP_TILE)[:, None] == ends[None, :]) & (padded > 0)[None, :]
    zmask = jnp.maximum(1 - tv, jnp.any(region_end, axis=1).astype(jnp.int32))
    rows = _dispatch(zmask, pos, hn.reshape(n, d // LANES, LANES), s_max)
    ys = _group_ffn(te, tx, tv, rows, w1.astype(BF16), w3.astype(BF16), w2.astype(BF16), layer)
    out = _combine(pos, xs, mod, info, ys, per_batch_mod, final_g)
    return out.reshape(bsz, t_len, d)


def _dft_channel_kernel(x_ref, mod_ref, ng_ref, cs_ref, y_ref, *, gw):
    hn = _norm_mod(x_ref[0], ng_ref[...], mod_ref[0, 1:2, :], mod_ref[0, 0:1, :]).astype(BF16)
    for g in range(hn.shape[1] // gw):
        y = _dot(hn[:, g * gw:(g + 1) * gw], cs_ref[...])
        y_ref[0, 0, :, g * gw:(g + 1) * gw] = y[:, :gw].astype(BF16)
        y_ref[0, 1, :, g * gw:(g + 1) * gw] = y[:, gw:].astype(BF16)


REV_TILE = 128


def _reverse_shift(src_tiles, wrap_row, m1):
    n_t = len(src_tiles)
    first = lax.broadcasted_iota(jnp.int32, (REV_TILE, 1), 0) == 0
    out = []
    for a in range(n_t):
        body = _dot(m1, src_tiles[n_t - 1 - a])
        head = wrap_row if a == 0 else src_tiles[n_t - a][0:1, :]
        out.append(jnp.where(first, head.astype(F32), body))
    return out


def _dft_fold_kernel(ya_ref, yb_ref, yn_ref, m1_ref, f_ref):
    n_t = ya_ref.shape[2] // REV_TILE
    keep = jnp.where(pl.program_id(1) == 0, 0.0, 1.0)
    for plane, sign in ((0, 1.0), (1, -1.0)):
        tiles = [yb_ref[0, plane, a * REV_TILE:(a + 1) * REV_TILE, :] for a in range(n_t)]
        wrap = yn_ref[0, plane, 0:1, :].astype(F32) * keep
        rev = _reverse_shift(tiles, wrap, m1_ref[...])
        for a in range(n_t):
            rows = slice(a * REV_TILE, (a + 1) * REV_TILE)
            f_ref[0, plane, rows, :] = (ya_ref[0, plane, rows, :].astype(F32) + sign * rev[a]).astype(BF16)


def _dft_time_kernel(wc_ref, ws_ref, wcx_ref, wsx_ref, f_ref, yh_ref, xlo_ref, xhi_ref, mod_ref, fw_ref, fb_ref,
                     m1_ref, lo_ref, hi_ref):
    rk = wc_ref.shape[0]
    ec, od = f_ref[0, 0], f_ref[0, 1]
    y_half = yh_ref[0, 0, 0:1, :].astype(F32) * ((2 * f_ref.shape[2]) ** -0.5)
    parity = lax.broadcasted_iota(jnp.int32, (rk, 1), 0) % 2
    p = _dot(wc_ref[...], ec) + jnp.where(parity == 0, 1.0, -1.0) * y_half
    q = _dot(ws_ref[...], od)
    gate = mod_ref[0, 2:3, :]

    def project(z, x):
        return x + gate * (_dot(z, fw_ref[...]) + fb_ref[...])

    lo_ref[0] = project((p - q).astype(BF16), xlo_ref[0])
    px = _dot(wcx_ref[...], ec)[0:1, :] + y_half
    qx = _dot(wsx_ref[...], od)[0:1, :]
    src = (p + q).astype(BF16)
    tiles = [src[a * REV_TILE:(a + 1) * REV_TILE, :] for a in range(rk // REV_TILE)]
    rev = _reverse_shift(tiles, (px + qx).astype(BF16), m1_ref[...])
    hi_ref[0] = project(jnp.concatenate(rev, axis=0).astype(BF16), xhi_ref[0])


def _dft_matrix(n, scale):
    idx = (np.arange(n)[:, None] * np.arange(n)[None, :]) % n
    ang = 2.0 * np.pi * idx.astype(np.float64) / n
    return np.cos(ang) * scale, np.sin(ang) * scale


def _fourier_mixer(x, mod, ng, fn_w, fn_b, per_batch_mod):
    bsz, t_len, d = x.shape
    gw = d // N_GROUPS
    cg, sg = _dft_matrix(gw, gw ** -0.5)
    cs = jnp.asarray(np.concatenate([cg, sg], axis=1).astype(np.float32)).astype(BF16)
    tm = _tile(t_len, 512)
    bidx = (lambda b: b) if per_batch_mod else (lambda b: 0)
    y = pl.pallas_call(
        functools.partial(_dft_channel_kernel, gw=gw),
        grid=(bsz, t_len // tm),
        in_specs=[pl.BlockSpec((1, tm, d), lambda b, i: (b, i, 0)),
                  pl.BlockSpec((1, 6, d), lambda b, i: (bidx(b), 0, 0)),
                  pl.BlockSpec((1, d), lambda b, i: (0, 0)),
                  pl.BlockSpec((gw, 2 * gw), lambda b, i: (0, 0))],
        out_specs=pl.BlockSpec((1, 2, tm, d), lambda b, i: (b, 0, i, 0)),
        out_shape=jax.ShapeDtypeStruct((bsz, 2, t_len, d), BF16),
        compiler_params=_params("parallel", "parallel"),
        name="dft_channel",
    )(x, mod, ng.reshape(1, d), cs)
    half = t_len // 2
    assert half % REV_TILE == 0 and half % DFT_SPLIT == 0
    pos = np.arange(REV_TILE)
    m1 = jnp.asarray((pos[None, :] == REV_TILE - pos[:, None]).astype(np.float32)).astype(BF16)
    rf = _tile(half, 512)
    nb_f = t_len // rf
    sub = 16
    folded = pl.pallas_call(
        _dft_fold_kernel,
        grid=(bsz, half // rf),
        in_specs=[pl.BlockSpec((1, 2, rf, d), lambda b, i: (b, 0, i, 0)),
                  pl.BlockSpec((1, 2, rf, d), lambda b, i: (b, 0, nb_f - 1 - i, 0)),
                  pl.BlockSpec((1, 2, sub, d), lambda b, i: (b, 0, ((nb_f - i) % nb_f) * (rf // sub), 0)),
                  pl.BlockSpec((REV_TILE, REV_TILE), lambda b, i: (0, 0))],
        out_specs=pl.BlockSpec((1, 2, rf, d), lambda b, i: (b, 0, i, 0)),
        out_shape=jax.ShapeDtypeStruct((bsz, 2, half, d), BF16),
        compiler_params=_params("parallel", "parallel"),
        name="dft_fold",
    )(y, y, y, m1)
    kk = np.arange(half + sub)[:, None]
    ang_hi = 2.0 * np.pi * ((kk * DFT_SPLIT * np.arange(half // DFT_SPLIT)[None, :]) % t_len) / t_len
    ang_lo = 2.0 * np.pi * ((kk * np.arange(DFT_SPLIT)[None, :]) % t_len) / t_len
    scale = t_len ** -0.5
    c_hi, s_hi = (jnp.asarray((f(ang_hi) * scale).astype(np.float32))[:, :, None] for f in (np.cos, np.sin))
    c_lo, s_lo = (jnp.asarray(f(ang_lo).astype(np.float32))[:, None, :] for f in (np.cos, np.sin))
    wc = (c_hi * c_lo - s_hi * s_lo).reshape(half + sub, half).astype(BF16)
    ws = (s_hi * c_lo + c_hi * s_lo).reshape(half + sub, half).astype(BF16)
    rk = _tile(half, 512)
    nk = half // rk
    lo, hi = pl.pallas_call(
        _dft_time_kernel,
        grid=(bsz, nk),
        in_specs=[pl.BlockSpec((rk, half), lambda b, i: (i, 0)),
                  pl.BlockSpec((rk, half), lambda b, i: (i, 0)),
                  pl.BlockSpec((sub, half), lambda b, i: ((i + 1) * (rk // sub), 0)),
                  pl.BlockSpec((sub, half), lambda b, i: ((i + 1) * (rk // sub), 0)),
                  pl.BlockSpec((1, 2, half, d), lambda b, i: (b, 0, 0, 0)),
                  pl.BlockSpec((1, 1, sub, d), lambda b, i: (b, 0, half // sub, 0)),
                  pl.BlockSpec((1, rk, d), lambda b, i: (b, i, 0)),
                  pl.BlockSpec((1, rk, d), lambda b, i: (b, 2 * nk - 1 - i, 0)),
                  pl.BlockSpec((1, 6, d), lambda b, i: (bidx(b), 0, 0)),
                  pl.BlockSpec((d, d), lambda b, i: (0, 0)),
                  pl.BlockSpec((1, d), lambda b, i: (0, 0)),
                  pl.BlockSpec((REV_TILE, REV_TILE), lambda b, i: (0, 0))],
        out_specs=(pl.BlockSpec((1, rk, d), lambda b, i: (b, i, 0)),
                   pl.BlockSpec((1, rk, d), lambda b, i: (b, nk - 1 - i, 0))),
        out_shape=(jax.ShapeDtypeStruct((bsz, half, d), F32), jax.ShapeDtypeStruct((bsz, half, d), F32)),
        compiler_params=_params("parallel", "parallel"),
        name="dft_time",
    )(wc, ws, wc, ws, folded, y, x, x, mod, fn_w.astype(BF16), fn_b.reshape(1, d).astype(F32), m1)
    return lo, hi


def _final_norm_kernel(x_ref, g_ref, o_ref):
    x = x_ref[0]
    o_ref[0] = x * lax.rsqrt(jnp.mean(x * x, axis=-1, keepdims=True) + EPS) * g_ref[...]


def _final_norm(x, g):
    bsz, t_len, d = x.shape
    tm = _tile(t_len, 1024)
    return pl.pallas_call(
        _final_norm_kernel,
        grid=(bsz, t_len // tm),
        in_specs=[pl.BlockSpec((1, tm, d), lambda b, i: (b, i, 0)), pl.BlockSpec((1, d), lambda b, i: (0, 0))],
        out_specs=pl.BlockSpec((1, tm, d), lambda b, i: (b, i, 0)),
        out_shape=jax.ShapeDtypeStruct((bsz, t_len, d), F32),
        compiler_params=_params("parallel", "parallel"),
        name="final_norm",
    )(x, g.reshape(1, d))


def _trunk(x, mod, cache, use_rope, per_batch_mod, p):
    bsz, t_len, d = x.shape
    depth = mod.shape[0]
    flat = (lambda a: a) if per_batch_mod else (lambda a: a.reshape(1, bsz * t_len, a.shape[-1]))
    unflat = (lambda a: a) if per_batch_mod else (lambda a: a.reshape(bsz, t_len, a.shape[-1]))
    states = []
    for i in range(depth):
        j = i // 2
        m_i = mod[i]
        if i % 2 == 0:
            q, kt, v, o, r = _mlstm_inproj(x, m_i, p["norm_g"][i, 0], p["ml_w_in"][j], p["ml_b_gate"][j],
                                           use_rope, per_batch_mod)
            outs = []
            for direction in range(2):
                if cache is None:
                    cn0 = jnp.zeros((bsz, N_HEADS // 2, 2 * HEAD_DK, 2 * HEAD_DV), F32)
                    m0 = jnp.zeros((bsz, N_HEADS, LANES), F32)
                else:
                    cn0, m0 = _pack_state(cache[0][:, j, direction], cache[1][:, j, direction],
                                          cache[2][:, j, direction])
                outs.append(_mlstm_scan(q, kt, v, r, cn0, m0, reverse=bool(direction)))
            states.append([_unpack_state(cn, m) for (_, cn, m) in outs])
            x = unflat(_mlstm_out_ffn(flat(outs[0][0]), flat(outs[1][0]), flat(o), flat(x), m_i, p["ml_head_g"][j],
                                      p["ml_w_out"][j], p["norm_g"][i, 1], p["ffn_w1"][j], p["ffn_w3"][j],
                                      p["ffn_w2"][j], per_batch_mod))
        else:
            halves = _fourier_mixer(x, m_i, p["norm_g"][i, 0], p["fn_w"][j], p["fn_b"][j], per_batch_mod)
            in_place = per_batch_mod and (t_len // 2) % ROUTE_TILE == 0 and (t_len // 2) % MOVE_TILE == 0
            xs = halves if in_place else (flat(jnp.concatenate(halves, axis=1)),)
            closing = p["final_g"] if i == depth - 1 else None
            x = unflat(_moe(xs, m_i, p["norm_g"][i, 1], p["moe_router"][j], p["moe_w1"], p["moe_w3"],
                            p["moe_w2"], j, per_batch_mod, closing))
    if depth % 2 == 1:
        x = unflat(_final_norm(flat(x), p["final_g"]))
    return x, states


def kernel(x_prompt, x_sample, state_C, state_n, state_m, c, c_ctx, w_mod, b_mod, norm_g, final_g,
           ml_w_in, ml_b_gate, ml_head_g, ml_w_out, fn_w, fn_b, ffn_w1, ffn_w3, ffn_w2,
           moe_router, moe_w1, moe_w3, moe_w2):
    p = dict(norm_g=norm_g, final_g=final_g, ml_w_in=ml_w_in, ml_b_gate=ml_b_gate, ml_head_g=ml_head_g,
             ml_w_out=ml_w_out, fn_w=fn_w, fn_b=fn_b, ffn_w1=ffn_w1, ffn_w3=ffn_w3, ffn_w2=ffn_w2,
             moe_router=moe_router, moe_w1=moe_w1, moe_w3=moe_w3, moe_w2=moe_w2)
    depth, d = w_mod.shape[0], w_mod.shape[1]
    n_dec = c.shape[0]
    rows = ((n_dec + 1 + 7) // 8) * 8
    cond = jnp.zeros((rows, d), F32).at[:n_dec].set(c).at[n_dec].set(c_ctx)
    mod = _mod_table(cond, w_mod, b_mod).reshape(depth, rows, 6, d)
    y_prompt, st = _trunk(x_prompt, mod[:, n_dec:n_dec + 1], None, False, False, p)
    y_sample, _ = _trunk(x_sample, mod[:, :n_dec], (state_C, state_n, state_m), True, True, p)
    new_c = jnp.stack([jnp.stack([s[0][0], s[1][0]], axis=1) for s in st], axis=1)
    new_n = jnp.stack([jnp.stack([s[0][1], s[1][1]], axis=1) for s in st], axis=1)
    new_m = jnp.stack([jnp.stack([s[0][2], s[1][2]], axis=1) for s in st], axis=1)
    return (y_prompt, y_sample, new_c.astype(x_prompt.dtype), new_n.astype(x_prompt.dtype),
            new_m.astype(x_prompt.dtype))
```
